```python
import functools
import jax, jax.numpy as jnp
from jax import lax
import numpy as np

D_MODEL = 1024
BATCH = 2
SEQ = 8192
DEPTH = 2

GRID_W = 64
CTX_LEN = 256
CHUNK = 64
ROPE_THETA = 10000.0

RET_HEADS = 4
RET_DH = 64
RET_W = RET_HEADS * RET_DH
GLA_HEADS = 4
GLA_DK = 48
GLA_DV = 96
GLA_KW = GLA_HEADS * GLA_DK
GLA_VW = GLA_HEADS * GLA_DV
GLA_RANK = 16
GLA_TAU = 16.0
RWKV_HEADS = 6
RWKV_DH = 64
RWKV_W = RWKV_HEADS * RWKV_DH
RWKV_W_RANK = 64
RWKV_A_RANK = 64
RWKV_G_RANK = 128

MIX_W = RET_W + GLA_VW + RWKV_W
RET_COLS = (RET_W, RET_W, RET_W, RET_W)
GLA_COLS = (GLA_KW, GLA_KW, GLA_VW, GLA_VW, GLA_RANK, GLA_RANK)
RWKV_COLS = (RWKV_W, RWKV_W, RWKV_W, RWKV_W_RANK, RWKV_W_RANK, RWKV_A_RANK, RWKV_G_RANK)
RET_IN = 4 * RET_W
GLA_IN = 2 * GLA_KW + 2 * GLA_VW + 2 * GLA_RANK
RWKV_IN = 3 * RWKV_W + 2 * RWKV_W_RANK + RWKV_A_RANK + RWKV_G_RANK
IN_W = RET_IN + GLA_IN + RWKV_IN
FFN_HIDDEN = -(-8 * D_MODEL // (3 * 256)) * 256

kernel_name = "hybrid_retnet_gla_rwkv7_prefix_dit_block"


def _split(u, widths):
    return jnp.split(u, [int(i) for i in np.cumsum(widths)[:-1]], axis=-1)


def _heads(u, n):
    b, t, w = u.shape
    return u.reshape(b, t, n, w // n).transpose(0, 2, 1, 3)


def _merge(o):
    b, h, t, d = o.shape
    return o.transpose(0, 2, 1, 3).reshape(b, t, h * d)


def _rmsnorm(x, g, eps=1e-6):
    xf = x.astype(jnp.float32)
    y = xf * lax.rsqrt(jnp.mean(xf * xf, axis=-1, keepdims=True) + eps)
    return (y * g.astype(jnp.float32)).astype(x.dtype)


def _norm_heads(o, eps):
    mu = jnp.mean(o, axis=-1, keepdims=True)
    var = jnp.mean(jnp.square(o - mu), axis=-1, keepdims=True)
    return (o - mu) * lax.rsqrt(var + eps)


def _modulate(h, shift, scale):
    return h * (1.0 + scale) + shift


def _swiglu(h, w_gate, w_up, w_down):
    return (jax.nn.silu(h @ w_gate) * (h @ w_up)) @ w_down


def _rope_2d(x, row, col):
    half = x.shape[-1] // 2
    nf = half // 2
    inv = ROPE_THETA ** (-jnp.arange(nf, dtype=jnp.float32) / nf)

    def rot(xh, pos):
        ang = pos[:, None] * inv
        cos, sin = jnp.cos(ang), jnp.sin(ang)
        x1, x2 = xh[..., :nf], xh[..., nf:]
        return jnp.concatenate([x1 * cos - x2 * sin, x1 * sin + x2 * cos], axis=-1)

    return jnp.concatenate([rot(x[..., :half], row), rot(x[..., half:], col)], axis=-1)


def _centred_conv(u, w):
    return lax.conv_general_dilated(
        u, w.astype(u.dtype)[:, None, :], window_strides=(1,), padding=((1, 1),),
        dimension_numbers=("NWC", "WIO", "NWC"), feature_group_count=u.shape[-1])


def _chunks(a):
    b, h, t, d = a.shape
    return a.reshape(b, h, t // CHUNK, CHUNK, d).transpose(2, 0, 1, 3, 4)


def _unchunk(o):
    n, b, h, c, d = o.shape
    return o.transpose(1, 2, 0, 3, 4).reshape(b, h, n * c, d)


def _flip(a):
    return jnp.flip(a, axis=2)


def _bidirectional(scan_f, scan_b, ctx_f, ctx_b, lat_f, lat_b, s0):
    oc_f, sc_f = scan_f(ctx_f, s0)
    oc_b, sc_b = scan_b(tuple(_flip(a) for a in ctx_b), s0)
    ol_f, _ = scan_f(lat_f, sc_f)
    ol_b, _ = scan_b(tuple(_flip(a) for a in lat_b), sc_b)
    return oc_f + _flip(oc_b), ol_f + _flip(ol_b)


def _retention_scan(seq, s0, log_gamma):
    pos = jnp.arange(CHUNK, dtype=jnp.float32)
    rel = pos[:, None] - pos[None, :]
    intra = jnp.where(rel >= 0, jnp.exp(log_gamma[:, None, None] * jnp.maximum(rel, 0.0)), 0.0)
    q_dec = jnp.exp(log_gamma[:, None] * (pos + 1.0))[:, :, None]
    k_dec = jnp.exp(log_gamma[:, None] * (CHUNK - 1.0 - pos))[:, :, None]
    c_dec = jnp.exp(log_gamma * CHUNK)[:, None, None]

    def step(s, blk):
        qb, kb, vb = blk
        sc = jnp.einsum("bhid,bhjd->bhij", qb, kb) * intra
        o = jnp.einsum("bhij,bhjv->bhiv", sc, vb) + jnp.einsum("bhid,bhdv->bhiv", qb * q_dec, s)
        s = s * c_dec + jnp.einsum("bhjd,bhjv->bhdv", kb * k_dec, vb)
        return s, o

    s, o = lax.scan(step, s0, tuple(_chunks(a) for a in seq))
    return _unchunk(o), s


def _gla_scan(seq, s0):
    causal = jnp.tril(jnp.ones((CHUNK, CHUNK), dtype=bool))

    def step(s, blk):
        qb, kb, vb, gb = blk
        cum = jnp.cumsum(gb, axis=2)
        rel = cum[:, :, :, None, :] - cum[:, :, None, :, :]
        decay = jnp.exp(jnp.where(causal[:, :, None], rel, -jnp.inf))
        sc = jnp.einsum("bhid,bhjd,bhijd->bhij", qb, kb, decay)
        o = jnp.einsum("bhij,bhjv->bhiv", sc, vb) + jnp.einsum("bhid,bhdv->bhiv", qb * jnp.exp(cum), s)
        last = cum[:, :, -1:, :]
        s = s * jnp.exp(last[:, :, 0, :, None]) + jnp.einsum("bhjd,bhjv->bhdv", kb * jnp.exp(last - cum), vb)
        return s, o

    s, o = lax.scan(step, s0, tuple(_chunks(a) for a in seq))
    return _unchunk(o), s


def _rwkv_scan(seq, s0):
    def step(s, inp):
        rt, wt, kt, vt, kkt, at = inp
        sa = jnp.einsum("bhvk,bhk->bhv", s, -kkt)
        s = s * wt[:, :, None, :] + sa[..., None] * (kkt * at)[:, :, None, :] + vt[..., None] * kt[:, :, None, :]
        return s, jnp.einsum("bhvk,bhk->bhv", s, rt)

    s, o = lax.scan(step, s0, tuple(jnp.moveaxis(a, 2, 0) for a in seq))
    return jnp.moveaxis(o, 0, 2), s


def _retention_mixer(u_ctx, u_lat, row, col, ret_norm, ctx_out):
    def prep(u, rotate):
        q, k, v, g = _split(u, RET_COLS)
        q, k, v = (_heads(a, RET_HEADS) for a in (q, k, v))
        if rotate:
            q, k = _rope_2d(q, row, col), _rope_2d(k, row, col)
        return (q, k * RET_DH ** -0.5, v), g

    seq_c, g_c = prep(u_ctx, False)
    seq_l, g_l = prep(u_lat, True)
    log_gamma = jnp.log1p(-jnp.exp2(-5.0 - jnp.arange(RET_HEADS, dtype=jnp.float32)))
    scan_f = functools.partial(_retention_scan, log_gamma=log_gamma)
    scan_b = functools.partial(_retention_scan, log_gamma=jnp.flip(log_gamma))
    s0 = jnp.zeros((u_lat.shape[0], RET_HEADS, RET_DH, RET_DH), jnp.float32)
    o_c, o_l = _bidirectional(scan_f, scan_b, seq_c, seq_c, seq_l, seq_l, s0)

    def finish(o, g):
        return jax.nn.silu(g) * (_merge(_norm_heads(o, 1e-5)) * ret_norm)

    return (finish(o_c, g_c) if ctx_out else None), finish(o_l, g_l)


def _gla_mixer(u_ctx, u_lat, wa2_f, ba_f, wa2_b, ba_b, gla_norm, ctx_out):
    def prep(u):
        q, k, v, g, lr_f, lr_b = _split(u, GLA_COLS)
        q = _heads(q, GLA_HEADS) * GLA_DK ** -0.5
        k = _heads(k, GLA_HEADS)
        v = _heads(v, GLA_HEADS)
        la_f = _heads(jax.nn.log_sigmoid(lr_f @ wa2_f + ba_f) / GLA_TAU, GLA_HEADS)
        la_b = _heads(jax.nn.log_sigmoid(lr_b @ wa2_b + ba_b) / GLA_TAU, GLA_HEADS)
        return (q, k, v, la_f), (q, k, v, la_b), g

    cf, cb, g_c = prep(u_ctx)
    lf, lb, g_l = prep(u_lat)
    s0 = jnp.zeros((u_lat.shape[0], GLA_HEADS, GLA_DK, GLA_DV), jnp.float32)
    o_c, o_l = _bidirectional(_gla_scan, _gla_scan, cf, cb, lf, lb, s0)

    def finish(o, g):
        y = o * lax.rsqrt(jnp.mean(o * o, axis=-1, keepdims=True) + 1e-5) * gla_norm
        return jax.nn.silu(g) * _merge(y)

    return (finish(o_c, g_c) if ctx_out else None), finish(o_l, g_l)


def _rwkv_mixer(u_ctx, u_lat, conv_w, w0_f, w2_f, w0_b, w2_b, a0, a2, g2, k_k, k_a, r_k, ln_w, ln_b, ctx_out):
    def decay(lr, w0, w2):
        log_w = -jax.nn.softplus(-(w0 + jnp.tanh(lr) @ w2)) - 0.5
        return jnp.exp(-jnp.exp(log_w))

    def prep(u):
        u = _centred_conv(u, conv_w)
        r, k, v, lr_wf, lr_wb, lr_a, lr_g = _split(u, RWKV_COLS)
        a = jax.nn.sigmoid(a0 + lr_a @ a2)
        g = jax.nn.sigmoid(lr_g) @ g2
        kk = _heads(k * k_k, RWKV_HEADS)
        kk = kk * lax.rsqrt(jnp.maximum(jnp.sum(kk * kk, axis=-1, keepdims=True), 1e-24))
        k = k * (1.0 + (a - 1.0) * k_a)
        r, k, v, a = (_heads(t, RWKV_HEADS) for t in (r, k, v, a))
        wf = _heads(decay(lr_wf, w0_f, w2_f), RWKV_HEADS)
        wb = _heads(decay(lr_wb, w0_b, w2_b), RWKV_HEADS)
        bonus = jnp.sum(r * k * r_k[:, None, :], axis=-1, keepdims=True) * v
        return (r, wf, k, v, kk, a), (r, wb, k, v, kk, a), g, bonus

    cf, cb, g_c, bo_c = prep(u_ctx)
    lf, lb, g_l, bo_l = prep(u_lat)
    s0 = jnp.zeros((u_lat.shape[0], RWKV_HEADS, RWKV_DH, RWKV_DH), jnp.float32)
    o_c, o_l = _bidirectional(_rwkv_scan, _rwkv_scan, cf, cb, lf, lb, s0)

    def finish(o, g, bonus):
        y = _merge(_norm_heads(o, 64e-5)) * ln_w + ln_b + _merge(bonus)
        return y * g

    return (finish(o_c, g_c, bo_c) if ctx_out else None), finish(o_l, g_l, bo_l)


def setup_inputs(seed: int = 0) -> dict:
    key = jax.random.key(seed)
    keys = iter(jax.random.split(key, 48))
    L, D = DEPTH, D_MODEL

    def rnd(shape, scale):
        return jax.random.normal(next(keys), shape, jnp.float32) * scale

    def gain(shape):
        return 1.0 + rnd(shape, 0.02)

    decay_base = jnp.linspace(-6.5, -1.5, RWKV_W, dtype=jnp.float32)[None, :]
    shift_taps = jnp.array([0.0, 1.0, 0.0], jnp.float32)[None, :, None]
    return {
        "x": rnd((BATCH, SEQ, D), 1.0),
        "c": rnd((BATCH, D), 1.0),
        "ctx": rnd((BATCH, CTX_LEN, D), 1.0),
        "c_ctx": rnd((D,), 1.0),
        "w_mod": rnd((L, D, 6 * D), D ** -0.5),
        "b_mod": rnd((L, 6 * D), 0.02),
        "norm_mix_pre": gain((L, D)),
        "norm_mix_post": gain((L, D)),
        "norm_ffn_pre": gain((L, D)),
        "norm_ffn_post": gain((L, D)),
        "w_in": rnd((L, D, IN_W), D ** -0.5),
        "ret_norm": gain((L, RET_W)),
        "gla_wa2_f": rnd((L, GLA_RANK, GLA_KW), GLA_RANK ** -0.5),
        "gla_ba_f": rnd((L, GLA_KW), 0.1),
        "gla_wa2_b": rnd((L, GLA_RANK, GLA_KW), GLA_RANK ** -0.5),
        "gla_ba_b": rnd((L, GLA_KW), 0.1),
        "gla_norm": gain((L, GLA_DV)),
        "rw_conv": rnd((L, 3, RWKV_IN), 0.2) + shift_taps,
        "rw_w0_f": decay_base + rnd((L, RWKV_W), 0.1),
        "rw_w2_f": rnd((L, RWKV_W_RANK, RWKV_W), 0.1),
        "rw_w0_b": decay_base + rnd((L, RWKV_W), 0.1),
        "rw_w2_b": rnd((L, RWKV_W_RANK, RWKV_W), 0.1),
        "rw_a0": rnd((L, RWKV_W), 0.1),
        "rw_a2": rnd((L, RWKV_A_RANK, RWKV_W), 0.5 * RWKV_A_RANK ** -0.5),
        "rw_g2": rnd((L, RWKV_G_RANK, RWKV_W), RWKV_G_RANK ** -0.5),
        "rw_k_k": 0.85 + rnd((L, RWKV_W), 0.02),
        "rw_k_a": 1.0 + rnd((L, RWKV_W), 0.02),
        "rw_r_k": rnd((L, RWKV_HEADS, RWKV_DH), 0.1),
        "rw_ln_w": gain((L, RWKV_W)),
        "rw_ln_b": rnd((L, RWKV_W), 0.02),
        "w_out": rnd((L, MIX_W, D), MIX_W ** -0.5),
        "w_ffn_gate": rnd((L, D, FFN_HIDDEN), D ** -0.5),
        "w_ffn_up": rnd((L, D, FFN_HIDDEN), D ** -0.5),
        "w_ffn_down": rnd((L, FFN_HIDDEN, D), FFN_HIDDEN ** -0.5),
    }


def reference(x, c, ctx, c_ctx, w_mod, b_mod, norm_mix_pre, norm_mix_post, norm_ffn_pre, norm_ffn_post,
              w_in, ret_norm, gla_wa2_f, gla_ba_f, gla_wa2_b, gla_ba_b, gla_norm, rw_conv,
              rw_w0_f, rw_w2_f, rw_w0_b, rw_w2_b, rw_a0, rw_a2, rw_g2, rw_k_k, rw_k_a, rw_r_k,
              rw_ln_w, rw_ln_b, w_out, w_ffn_gate, w_ffn_up, w_ffn_down):
    t = x.shape[1]
    rows = t // GRID_W
    row = jnp.repeat(jnp.arange(rows, dtype=jnp.float32), GRID_W)
    col = jnp.tile(jnp.arange(GRID_W, dtype=jnp.float32), rows)

    for l in range(DEPTH):
        ctx_out = l < DEPTH - 1
        mod_l = jax.nn.silu(c) @ w_mod[l] + b_mod[l]
        mod_c = jax.nn.silu(c_ctx) @ w_mod[l] + b_mod[l]
        sh1, sc1, gt1, sh2, sc2, gt2 = jnp.split(mod_l[:, None, :], 6, axis=-1)
        csh1, csc1, cgt1, csh2, csc2, cgt2 = jnp.split(mod_c[None, None, :], 6, axis=-1)

        u_lat = (_modulate(_rmsnorm(x, norm_mix_pre[l]), sh1, sc1) @ w_in[l]).astype(jnp.float32)
        u_ctx = (_modulate(_rmsnorm(ctx, norm_mix_pre[l]), csh1, csc1) @ w_in[l]).astype(jnp.float32)
        ua_c, ub_c, uc_c = _split(u_ctx, (RET_IN, GLA_IN, RWKV_IN))
        ua_l, ub_l, uc_l = _split(u_lat, (RET_IN, GLA_IN, RWKV_IN))
        ya_c, ya_l = _retention_mixer(ua_c, ua_l, row, col, ret_norm[l], ctx_out)
        yb_c, yb_l = _gla_mixer(ub_c, ub_l, gla_wa2_f[l], gla_ba_f[l], gla_wa2_b[l], gla_ba_b[l],
                                gla_norm[l], ctx_out)
        yc_c, yc_l = _rwkv_mixer(uc_c, uc_l, rw_conv[l], rw_w0_f[l], rw_w2_f[l], rw_w0_b[l], rw_w2_b[l],
                                 rw_a0[l], rw_a2[l], rw_g2[l], rw_k_k[l], rw_k_a[l], rw_r_k[l],
                                 rw_ln_w[l], rw_ln_b[l], ctx_out)
        y_lat = jnp.concatenate([ya_l, yb_l, yc_l], axis=-1).astype(x.dtype) @ w_out[l]
        x = x + gt1 * _rmsnorm(y_lat, norm_mix_post[l])

        f_lat = _swiglu(_modulate(_rmsnorm(x, norm_ffn_pre[l]), sh2, sc2), w_ffn_gate[l], w_ffn_up[l], w_ffn_down[l])
        x = x + gt2 * _rmsnorm(f_lat, norm_ffn_post[l])

        if ctx_out:
            y_ctx = jnp.concatenate([ya_c, yb_c, yc_c], axis=-1).astype(ctx.dtype) @ w_out[l]
            ctx = ctx + cgt1 * _rmsnorm(y_ctx, norm_mix_post[l])
            f_ctx = _swiglu(_modulate(_rmsnorm(ctx, norm_ffn_pre[l]), csh2, csc2), w_ffn_gate[l], w_ffn_up[l], w_ffn_down[l])
            ctx = ctx + cgt2 * _rmsnorm(f_ctx, norm_ffn_post[l])
    return x
```

```python
import functools

import numpy as np
import jax
import jax.numpy as jnp
from jax import lax
from jax.experimental import pallas as pl
from jax.experimental.pallas import tpu as pltpu

F32 = jnp.float32
BF16 = jnp.bfloat16

D_MODEL = 1024
CTX_LEN = 256
GRID_W = 64
ROPE_THETA = 10000.0

RET_HEADS, RET_DH = 4, 64
RET_W = RET_HEADS * RET_DH
GLA_HEADS, GLA_DK, GLA_DV = 4, 48, 96
GLA_DKP, GLA_DVP = 64, 128
GLA_KWP, GLA_VWP = GLA_HEADS * GLA_DKP, GLA_HEADS * GLA_DVP
GLA_RANK = 16
GLA_TAU = 16.0
RW_HEADS, RW_DH = 6, 64
RW_W = RW_HEADS * RW_DH
FFN_HIDDEN = 2816

URET_W = 4 * RET_W
UGLA_W = 2 * GLA_KWP + 2 * GLA_VWP + 128
URW_W = 3 * RW_W + 128 + 128 + 128
NP_IN = URET_W + UGLA_W + URW_W
MIXP_W = RET_W + GLA_VWP + RW_W

CHUNK = 64
ROW_TILE = 256
VMEM_LIMIT = 56 * 1024 * 1024


def _bf(x):
    return x.astype(BF16)


def _dot(a, b):
    return jnp.dot(a, b, preferred_element_type=F32)


def _dot_nt(a, b):
    return lax.dot_general(a, b, (((1,), (1,)), ((), ())), preferred_element_type=F32)


def _dot_tn(a, b):
    return lax.dot_general(a, b, (((0,), (0,)), ((), ())), preferred_element_type=F32)


def _dot_exact(a, b):
    return jnp.dot(a, b, preferred_element_type=F32, precision=lax.Precision.HIGHEST)


def _split_dot(x, m):
    hi = _bf(x)
    lo = _bf(x - hi.astype(F32))
    return _dot(hi, m) + _dot(lo, m)


def _sigmoid(x):
    return 1.0 / (1.0 + jnp.exp(-x))


def _silu(x):
    return x * _sigmoid(x)


def _softplus(x):
    return jnp.maximum(x, 0.0) + jnp.log(1.0 + jnp.exp(-jnp.abs(x)))


def _rms(x, g, eps=1e-6):
    return x * lax.rsqrt(jnp.mean(x * x, axis=-1, keepdims=True) + eps) * g


def _order_masks(c, reverse):
    t = lax.broadcasted_iota(jnp.int32, (c, c), 0)
    s = lax.broadcasted_iota(jnp.int32, (c, c), 1)
    if reverse:
        return s >= t, s > t
    return s <= t, s < t


def _params(*sem):
    return pltpu.CompilerParams(dimension_semantics=sem, vmem_limit_bytes=VMEM_LIMIT)


def _const_spec(shape):
    nd = len(shape)
    return pl.BlockSpec(shape, lambda *_: (0,) * nd)


def _mod_kernel(c_ref, w_ref, b_ref, o_ref):
    c = c_ref[...]
    o_ref[...] = _dot_exact(_silu(c), w_ref[...]) + b_ref[...]


def _mod_call(cc, w_mod, b_mod):
    depth, d, n = w_mod.shape
    tn = 1024
    return pl.pallas_call(
        _mod_kernel,
        grid=(depth, n // tn),
        in_specs=[
            pl.BlockSpec((8, d), lambda l, j: (0, 0)),
            pl.BlockSpec((None, d, tn), lambda l, j: (l, 0, j)),
            pl.BlockSpec((None, 1, tn), lambda l, j: (l, 0, j)),
        ],
        out_specs=pl.BlockSpec((None, 8, tn), lambda l, j: (l, 0, j)),
        out_shape=jax.ShapeDtypeStruct((depth, 8, n), F32),
        compiler_params=_params("arbitrary", "arbitrary"),
        name="mod",
    )(cc, w_mod, b_mod.reshape(depth, 1, n))


def _proj_kernel(x_ref, mod_ref, g_ref, w_ref, ur_ref, ug_ref, uw_ref):
    d = D_MODEL
    h = _rms(x_ref[...], g_ref[...])
    h = _bf(h * (1.0 + mod_ref[:, d:2 * d]) + mod_ref[:, 0:d])
    ur_ref[...] = _dot(h, w_ref[:, 0:URET_W])
    ug_ref[...] = _dot(h, w_ref[:, URET_W:URET_W + UGLA_W])
    uw_ref[...] = _dot(h, w_ref[:, URET_W + UGLA_W:NP_IN])


def _mod_spec():
    return pl.BlockSpec((None, None, 1, 6 * D_MODEL), lambda b, i: (b, jnp.minimum(i, 1), 0, 0))


def _proj_call(xa, modsel, g, w):
    bsz, ta, d = xa.shape
    nt = ta // ROW_TILE
    row = lambda w_: pl.BlockSpec((None, ROW_TILE, w_), lambda b, i: (b, i, 0))
    return pl.pallas_call(
        _proj_kernel,
        grid=(bsz, nt),
        in_specs=[row(d), _mod_spec(), _const_spec((1, d)), _const_spec((d, NP_IN))],
        out_specs=[row(URET_W), row(UGLA_W), row(URW_W)],
        out_shape=[jax.ShapeDtypeStruct((bsz, ta, w_), F32) for w_ in (URET_W, UGLA_W, URW_W)],
        compiler_params=_params("arbitrary", "arbitrary"),
        name="proj",
    )(xa, modsel, g, w)


def _chunk_maps(ta):
    ns = ta // CHUNK
    nc = CTX_LEN // CHUNK

    def fwd(b, s):
        return (b, s, 0)

    def bwd(b, s):
        return (b, jnp.where(s < nc, nc - 1 - s, ns + nc - 1 - s), 0)

    return ns, fwd, bwd


def _chunk_specs(ta, width):
    ns, fwd, bwd = _chunk_maps(ta)
    return (pl.BlockSpec((None, CHUNK, width), fwd), pl.BlockSpec((None, CHUNK, width), bwd))


def _rope_swap(x):
    lane = lax.broadcasted_iota(jnp.int32, x.shape, 1)
    w = x.shape[1]
    return jnp.where(lane % 32 < 16, pltpu.roll(x, w - 16, 1), pltpu.roll(x, 16, 1))


def _ret_kernel(uf_ref, ub_ref, cf_ref, sf_ref, cb_ref, sb_ref, dm_ref, qd_ref, kd_ref, cd_ref,
                of_ref, ob_ref, st_ref):
    @pl.when(pl.program_id(1) == 0)
    def _():
        st_ref[...] = jnp.zeros_like(st_ref)

    dirs = ((uf_ref, cf_ref, sf_ref, of_ref), (ub_ref, cb_ref, sb_ref, ob_ref))
    for d, (u_ref, cos_ref, sin_ref, o_ref) in enumerate(dirs):
        cos, sin = cos_ref[...], sin_ref[...]
        q = u_ref[:, 0:RET_W]
        k = u_ref[:, RET_W:2 * RET_W]
        v = _bf(u_ref[:, 2 * RET_W:3 * RET_W])
        q = q * cos + _rope_swap(q) * sin
        k = (k * cos + _rope_swap(k) * sin) * (RET_DH ** -0.5)
        qb, kb = _bf(q), _bf(k)
        qd, kd = _bf(q * qd_ref[d]), _bf(k * kd_ref[d])
        outs = []
        for h in range(RET_HEADS):
            sl = slice(h * RET_DH, (h + 1) * RET_DH)
            sc = _dot_nt(qb[:, sl], kb[:, sl]) * dm_ref[d, h]
            s0 = st_ref[d, h]
            outs.append(_dot(_bf(sc), v[:, sl]) + _dot_nt(qd[:, sl], _bf(s0)))
            st_ref[d, h] = s0 * cd_ref[d, h] + _dot_tn(v[:, sl], kd[:, sl])
        o_ref[...] = jnp.concatenate(outs, axis=1)


def _ret_consts():
    c = CHUNK
    lg = np.log1p(-np.exp2(-5.0 - np.arange(RET_HEADS, dtype=np.float64)))
    pos = np.arange(c, dtype=np.float64)
    rel = pos[:, None] - pos[None, :]
    dm = np.zeros((2, RET_HEADS, c, c))
    qd = np.zeros((2, c, RET_W))
    kd = np.zeros((2, c, RET_W))
    cd = np.zeros((2, RET_HEADS, 1, RET_DH))
    for d in range(2):
        lgd = lg if d == 0 else lg[::-1]
        p = pos if d == 0 else c - 1.0 - pos
        for h in range(RET_HEADS):
            intra = np.where(rel >= 0, np.exp(lgd[h] * np.maximum(rel, 0.0)), 0.0)
            dm[d, h] = intra if d == 0 else intra.T
            qd[d, :, h * RET_DH:(h + 1) * RET_DH] = np.exp(lgd[h] * (p + 1.0))[:, None]
            kd[d, :, h * RET_DH:(h + 1) * RET_DH] = np.exp(lgd[h] * (c - 1.0 - p))[:, None]
            cd[d, h] = np.exp(lgd[h] * c)
    return tuple(jnp.asarray(a, F32) for a in (dm, qd, kd, cd))


def _ret_call(u_ret, cos, sin):
    bsz, ta, _ = u_ret.shape
    ns, _, _ = _chunk_maps(ta)
    uf, ub = _chunk_specs(ta, URET_W)
    tf, tb = (pl.BlockSpec((CHUNK, RET_W), lambda b, s, m=m: m(b, s)[1:]) for m in _chunk_maps(ta)[1:])
    of, ob = _chunk_specs(ta, RET_W)
    dm, qd, kd, cd = _ret_consts()
    return pl.pallas_call(
        _ret_kernel,
        grid=(bsz, ns),
        in_specs=[uf, ub, tf, tf, tb, tb, _const_spec(dm.shape), _const_spec(qd.shape),
                  _const_spec(kd.shape), _const_spec(cd.shape)],
        out_specs=[of, ob],
        out_shape=[jax.ShapeDtypeStruct((bsz, ta, RET_W), F32)] * 2,
        scratch_shapes=[pltpu.VMEM((2, RET_HEADS, RET_DH, RET_DH), F32)],
        compiler_params=_params("arbitrary", "arbitrary"),
        name="ret_scan",
    )(u_ret, u_ret, cos, sin, cos, sin, dm, qd, kd, cd)


def _gla_kernel(uf_ref, ub_ref, wa_ref, ba_ref, of_ref, ob_ref, st_ref):
    @pl.when(pl.program_id(1) == 0)
    def _():
        st_ref[...] = jnp.zeros_like(st_ref)

    c = CHUNK
    for d, (u_ref, o_ref) in enumerate(((uf_ref, of_ref), (ub_ref, ob_ref))):
        incl, _ = _order_masks(c, d == 1)
        lr = _bf(u_ref[:, 2 * GLA_KWP + 2 * GLA_VWP:UGLA_W])
        z = _dot(lr, wa_ref[d]) + ba_ref[d]
        la = -_softplus(-z) * (1.0 / GLA_TAU)
        g = _dot_exact(incl.astype(F32), la)
        g_end = jnp.sum(la, axis=0, keepdims=True)
        mid = c // 2 if d == 1 else c // 2 - 1
        g_mid = g[mid:mid + 1, :]
        q = u_ref[:, 0:GLA_KWP] * (GLA_DK ** -0.5)
        k = u_ref[:, GLA_KWP:2 * GLA_KWP]
        v = _bf(u_ref[:, 2 * GLA_KWP:2 * GLA_KWP + GLA_VWP])
        qt, kt = _bf(q * jnp.exp(g - g_mid)), _bf(k * jnp.exp(g_mid - g))
        qs, ks = _bf(q * jnp.exp(g)), _bf(k * jnp.exp(g_end - g))
        e_end = jnp.exp(g_end)
        outs = []
        for h in range(GLA_HEADS):
            sk = slice(h * GLA_DKP, (h + 1) * GLA_DKP)
            sv = slice(h * GLA_DVP, (h + 1) * GLA_DVP)
            sc = jnp.where(incl, _dot_nt(qt[:, sk], kt[:, sk]), 0.0)
            s0 = st_ref[d, h]
            outs.append(_dot(_bf(sc), v[:, sv]) + _dot_nt(qs[:, sk], _bf(s0)))
            st_ref[d, h] = s0 * e_end[:, sk] + _dot_tn(v[:, sv], ks[:, sk])
        o_ref[...] = jnp.concatenate(outs, axis=1)


def _gla_call(u_gla, wa, ba):
    bsz, ta, _ = u_gla.shape
    ns, _, _ = _chunk_maps(ta)
    uf, ub = _chunk_specs(ta, UGLA_W)
    of, ob = _chunk_specs(ta, GLA_VWP)
    return pl.pallas_call(
        _gla_kernel,
        grid=(bsz, ns),
        in_specs=[uf, ub, _const_spec(wa.shape), _const_spec(ba.shape)],
        out_specs=[of, ob],
        out_shape=[jax.ShapeDtypeStruct((bsz, ta, GLA_VWP), F32)] * 2,
        scratch_shapes=[pltpu.VMEM((2, GLA_HEADS, GLA_DVP, GLA_DKP), F32)],
        compiler_params=_params("arbitrary", "arbitrary"),
        name="gla_scan",
    )(u_gla, u_gla, wa, ba)


RW_S1_W = 7 * RW_W
RW_S2_W = 2 * RW_W


def _rw_prep_kernel(u_ref, up_ref, un_ref, cw_ref, a0_ref, a2_ref, g2_ref, w0_ref, w2_ref,
                    kk_ref, ka_ref, rk_ref, bsum_ref, s1_ref, s2_ref, *, n_tiles):
    i = pl.program_id(1)
    tm = ROW_TILE
    u = u_ref[...]
    rows = lax.broadcasted_iota(jnp.int32, (tm, 1), 0)
    prev_row = jnp.where(i >= 2, up_ref[7:8, :], 0.0)
    next_row = jnp.where(jnp.logical_and(i >= 1, i < n_tiles - 1), un_ref[0:1, :], 0.0)
    u_prev = jnp.where(rows == 0, prev_row, pltpu.roll(u, 1, 0))
    u_next = jnp.where(rows == tm - 1, next_row, pltpu.roll(u, tm - 1, 0))
    uc = u_prev * cw_ref[0:1, :] + u * cw_ref[1:2, :] + u_next * cw_ref[2:3, :]

    w = RW_W
    r, k, v = uc[:, 0:w], uc[:, w:2 * w], uc[:, 2 * w:3 * w]
    lr_w = uc[:, 3 * w:3 * w + 128]
    lr_a = uc[:, 3 * w + 128:3 * w + 256]
    lr_g = uc[:, 3 * w + 256:3 * w + 384]
    a = _sigmoid(a0_ref[...] + _dot(_bf(lr_a), a2_ref[...]))
    gate = _dot(_bf(_sigmoid(lr_g)), g2_ref[...])
    kk = k * kk_ref[...]
    bsum = bsum_ref[...]
    kk = kk * lax.rsqrt(jnp.maximum(_split_dot(kk * kk, bsum), 1e-24))
    k2 = k * (1.0 + (a - 1.0) * ka_ref[...])
    log_w = -_softplus(-(w0_ref[...] + _dot(_bf(jnp.tanh(lr_w)), w2_ref[...]))) - 0.5
    ld = -jnp.exp(log_w)
    bonus = _split_dot(r * k2 * rk_ref[...], bsum) * v
    s1_ref[:, 0:w] = r
    s1_ref[:, w:2 * w] = k2
    s1_ref[:, 2 * w:3 * w] = v
    s1_ref[:, 3 * w:4 * w] = -kk
    s1_ref[:, 4 * w:5 * w] = kk * a
    s1_ref[:, 5 * w:7 * w] = ld
    s2_ref[:, 0:w] = gate
    s2_ref[:, w:2 * w] = bonus


def _rw_prep_call(u_rw, p):
    bsz, ta, _ = u_rw.shape
    nt = ta // ROW_TILE
    per = ROW_TILE // 8
    nb8 = ta // 8
    row = lambda w_: pl.BlockSpec((None, ROW_TILE, w_), lambda b, i: (b, i, 0))
    prev = pl.BlockSpec((None, 8, URW_W), lambda b, i: (b, jnp.maximum(i * per - 1, 0), 0))
    nxt = pl.BlockSpec((None, 8, URW_W), lambda b, i: (b, jnp.minimum((i + 1) * per, nb8 - 1), 0))
    consts = [p["conv"], p["a0"], p["a2"], p["g2"], p["w0"], p["w2"], p["k_k"], p["k_a"], p["r_k"], p["bsum"]]
    return pl.pallas_call(
        functools.partial(_rw_prep_kernel, n_tiles=nt),
        grid=(bsz, nt),
        in_specs=[row(URW_W), prev, nxt] + [_const_spec(a.shape) for a in consts],
        out_specs=[row(RW_S1_W), row(RW_S2_W)],
        out_shape=[jax.ShapeDtypeStruct((bsz, ta, RW_S1_W), F32),
                   jax.ShapeDtypeStruct((bsz, ta, RW_S2_W), F32)],
        compiler_params=_params("arbitrary", "arbitrary"),
        name="rw_prep",
    )(u_rw, u_rw, u_rw, *consts)


def _rw_kernel(sf_ref, sb_ref, of_ref, ob_ref, st_ref):
    @pl.when(pl.program_id(1) == 0)
    def _():
        st_ref[...] = jnp.zeros_like(st_ref)

    c, w, dh = CHUNK, RW_W, RW_DH
    for d, (s_ref, o_ref) in enumerate(((sf_ref, of_ref), (sb_ref, ob_ref))):
        incl, _ = _order_masks(c, d == 1)
        tt = lax.broadcasted_iota(jnp.int32, (2 * c, 2 * c), 0)
        ss = lax.broadcasted_iota(jnp.int32, (2 * c, 2 * c), 1) & (c - 1)
        t_in = tt & (c - 1)
        before = (ss > t_in) if d == 1 else (ss < t_in)
        pair_mask = jnp.logical_or(before, jnp.logical_and(tt >= c, ss == t_in))
        t64 = lax.broadcasted_iota(jnp.int32, (c, c), 0)
        s64 = lax.broadcasted_iota(jnp.int32, (c, c), 1)
        eye = (t64 == s64).astype(F32)
        late, early = (s64, t64) if d == 1 else (t64, s64)
        level_masks = [
            jnp.logical_and((t64 >> (j + 1)) == (s64 >> (j + 1)),
                            jnp.logical_and(((late >> j) & 1) == 1, ((early >> j) & 1) == 0))
            for j in range(6)]
        la = s_ref[:, (5 + d) * w:(6 + d) * w]
        g = _dot_exact(incl.astype(F32), la)
        g_end = jnp.sum(la, axis=0, keepdims=True)
        e_g = jnp.exp(g)
        e_gp = jnp.exp(g - la)
        e_inv = jnp.exp(-g_end)
        e_k = jnp.exp(g_end - g)
        e_end = jnp.exp(g_end)
        r0 = s_ref[:, 0:w] * e_g
        a0 = s_ref[:, 3 * w:4 * w] * e_gp
        rt, at = r0 * e_inv, a0 * e_inv
        kt = s_ref[:, w:2 * w] * e_k
        bt = s_ref[:, 4 * w:5 * w] * e_k
        v = _bf(s_ref[:, 2 * w:3 * w])
        outs = []
        for h in range(RW_HEADS):
            sl = slice(h * dh, (h + 1) * dh)
            lhs = _bf(jnp.concatenate([at[:, sl], rt[:, sl]], axis=0))
            rhs = _bf(jnp.concatenate([bt[:, sl], kt[:, sl]], axis=0))
            pair = jnp.where(pair_mask, _dot_nt(lhs, rhs), 0.0)
            n = pair[0:c, 0:c]
            x = jnp.concatenate([_dot(_bf(pair[0:c, c:2 * c]), v[:, sl]), a0[:, sl]], axis=1)
            tm = jnp.where(level_masks[0], n, eye)
            for lm in level_masks[1:]:
                tb = _bf(tm)
                tm = tm + _dot(_bf(_dot(tb, _bf(jnp.where(lm, n, 0.0)))), tb)
            x = _dot(_bf(tm), _bf(x))
            s0 = st_ref[d, h]
            s0b = _bf(s0)
            u = x[:, 0:dh] + _dot_nt(_bf(x[:, dh:2 * dh]), s0b)
            ub = _bf(u)
            outs.append(_dot_nt(_bf(r0[:, sl]), s0b) + _dot(_bf(pair[c:2 * c, 0:c]), ub)
                        + _dot(_bf(pair[c:2 * c, c:2 * c]), v[:, sl]))
            st_ref[d, h] = (s0 * e_end[:, sl] + _dot_tn(ub, _bf(bt[:, sl]))
                            + _dot_tn(v[:, sl], _bf(kt[:, sl])))
        o_ref[...] = jnp.concatenate(outs, axis=1)


def _rw_call(s1):
    bsz, ta, _ = s1.shape
    ns, _, _ = _chunk_maps(ta)
    sf, sb = _chunk_specs(ta, RW_S1_W)
    of, ob = _chunk_specs(ta, RW_W)
    return pl.pallas_call(
        _rw_kernel,
        grid=(bsz, ns),
        in_specs=[sf, sb],
        out_specs=[of, ob],
        out_shape=[jax.ShapeDtypeStruct((bsz, ta, RW_W), F32)] * 2,
        scratch_shapes=[pltpu.VMEM((2, RW_HEADS, RW_DH, RW_DH), F32)],
        compiler_params=_params("arbitrary", "arbitrary"),
        name="rw_scan",
    )(s1, s1)


def _mix_out_kernel(x_ref, mod_ref, raf_ref, rab_ref, rg_ref, gaf_ref, gab_ref, gg_ref,
                    waf_ref, wab_ref, s2_ref, rn_ref, gn_ref, lw_ref, lb_ref,
                    avg_r_ref, sum_g_ref, avg_w_ref, wo_ref, pn_ref, o_ref):
    d = D_MODEL
    o = raf_ref[...] + rab_ref[...]
    avg = avg_r_ref[...]
    mu = _split_dot(o, avg)
    oc = o - mu
    var = _split_dot(oc * oc, avg)
    ya = _silu(rg_ref[...]) * (oc * lax.rsqrt(var + 1e-5) * rn_ref[...])
    o = gaf_ref[...] + gab_ref[...]
    ms = _split_dot(o * o, sum_g_ref[...]) * (1.0 / GLA_DV)
    yb = _silu(gg_ref[...]) * (o * lax.rsqrt(ms + 1e-5) * gn_ref[...])
    o = waf_ref[...] + wab_ref[...]
    avg = avg_w_ref[...]
    mu = _split_dot(o, avg)
    oc = o - mu
    var = _split_dot(oc * oc, avg)
    yc = (oc * lax.rsqrt(var + 64e-5) * lw_ref[...] + lb_ref[...] + s2_ref[:, RW_W:2 * RW_W]) * s2_ref[:, 0:RW_W]
    y = (_dot(_bf(ya), wo_ref[0:RET_W, :]) + _dot(_bf(yb), wo_ref[RET_W:RET_W + GLA_VWP, :])
         + _dot(_bf(yc), wo_ref[RET_W + GLA_VWP:MIXP_W, :]))
    o_ref[...] = x_ref[...] + mod_ref[:, 2 * d:3 * d] * _rms(y, pn_ref[...])


def _mix_out_call(xa, modsel, ra, u_ret, ga, u_gla, wa, s2, p, tile_off):
    bsz, ta, d = xa.shape
    nt = ta // ROW_TILE - tile_off
    row = lambda w_, cb=0: pl.BlockSpec((None, ROW_TILE, w_), lambda b, i: (b, i + tile_off, cb))
    mod = pl.BlockSpec((None, None, 1, 6 * d), lambda b, i: (b, jnp.minimum(i + tile_off, 1), 0, 0))
    consts = [p["ret_norm"], p["gla_norm"], p["ln_w"], p["ln_b"], p["avg_r"], p["sum_g"], p["avg_w"],
              p["w_out"], p["norm_mix_post"]]
    return pl.pallas_call(
        _mix_out_kernel,
        grid=(bsz, nt),
        in_specs=[row(d), mod, row(RET_W), row(RET_W), row(RET_W, 3),
                  row(GLA_VWP), row(GLA_VWP), row(GLA_VWP, 2),
                  row(RW_W), row(RW_W), row(RW_S2_W)] + [_const_spec(a.shape) for a in consts],
        out_specs=pl.BlockSpec((None, ROW_TILE, d), lambda b, i: (b, i, 0)),
        out_shape=jax.ShapeDtypeStruct((bsz, nt * ROW_TILE, d), F32),
        compiler_params=_params("arbitrary", "arbitrary"),
        name="mix_out",
    )(xa, modsel, ra[0], ra[1], u_ret, ga[0], ga[1], u_gla, wa[0], wa[1], s2, *consts)


def _ffn_kernel(x_ref, mod_ref, pre_ref, wg_ref, wu_ref, wd_ref, post_ref, o_ref):
    d = D_MODEL
    x = x_ref[...]
    h = _rms(x, pre_ref[...])
    h = _bf(h * (1.0 + mod_ref[:, 4 * d:5 * d]) + mod_ref[:, 3 * d:4 * d])
    act = _bf(_silu(_dot(h, wg_ref[...])) * _dot(h, wu_ref[...]))
    f = _dot(act, wd_ref[...])
    o_ref[...] = x + mod_ref[:, 5 * d:6 * d] * _rms(f, post_ref[...])


def _ffn_call(x1, modsel, p, tile_off):
    bsz, t, d = x1.shape
    nt = t // ROW_TILE
    row = pl.BlockSpec((None, ROW_TILE, d), lambda b, i: (b, i, 0))
    mod = pl.BlockSpec((None, None, 1, 6 * d), lambda b, i: (b, jnp.minimum(i + tile_off, 1), 0, 0))
    consts = [p["norm_ffn_pre"], p["w_gate"], p["w_up"], p["w_down"], p["norm_ffn_post"]]
    return pl.pallas_call(
        _ffn_kernel,
        grid=(bsz, nt),
        in_specs=[row, mod] + [_const_spec(a.shape) for a in consts],
        out_specs=row,
        out_shape=jax.ShapeDtypeStruct((bsz, t, d), F32),
        compiler_params=_params("arbitrary", "arbitrary"),
        name="ffn",
    )(x1, modsel, *consts)


def _pad_last(a, n):
    return jnp.pad(a, [(0, 0)] * (a.ndim - 1) + [(0, n - a.shape[-1])])


def _pad_heads(a, heads, dh, dhp):
    lead = a.shape[:-1]
    a = a.reshape(lead + (heads, dh))
    return _pad_last(a, dhp).reshape(lead + (heads * dhp,))


def _arrange_rw(a):
    w = RW_W
    return jnp.concatenate([a[..., 0:3 * w + 128], _pad_last(a[..., 3 * w + 128:3 * w + 192], 128),
                            a[..., 3 * w + 192:3 * w + 320]], axis=-1)


def _arrange_in(a):
    g0 = URET_W
    kw, vw = GLA_HEADS * GLA_DK, GLA_HEADS * GLA_DV
    r0 = g0 + 2 * kw + 2 * vw + 2 * GLA_RANK
    return jnp.concatenate([
        a[..., 0:g0],
        _pad_heads(a[..., g0:g0 + kw], GLA_HEADS, GLA_DK, GLA_DKP),
        _pad_heads(a[..., g0 + kw:g0 + 2 * kw], GLA_HEADS, GLA_DK, GLA_DKP),
        _pad_heads(a[..., g0 + 2 * kw:g0 + 2 * kw + vw], GLA_HEADS, GLA_DV, GLA_DVP),
        _pad_heads(a[..., g0 + 2 * kw + vw:g0 + 2 * kw + 2 * vw], GLA_HEADS, GLA_DV, GLA_DVP),
        _pad_last(a[..., g0 + 2 * kw + 2 * vw:r0], 128),
        _arrange_rw(a[..., r0:]),
    ], axis=-1)


def _block_diag_ones(n, blk, val=1.0):
    i = np.arange(n)
    return jnp.asarray(np.where(i[:, None] // blk == i[None, :] // blk, val, 0.0), BF16)


def _rope_tables(t_lat):
    rows = t_lat // GRID_W
    row = jnp.repeat(jnp.arange(rows, dtype=F32), GRID_W)
    col = jnp.tile(jnp.arange(GRID_W, dtype=F32), rows)
    nf = RET_DH // 4
    inv = ROPE_THETA ** (-jnp.arange(nf, dtype=F32) / nf)
    ar, ac = row[:, None] * inv, col[:, None] * inv
    cos = jnp.concatenate([jnp.cos(ar), jnp.cos(ar), jnp.cos(ac), jnp.cos(ac)], axis=1)
    sin = jnp.concatenate([-jnp.sin(ar), jnp.sin(ar), -jnp.sin(ac), jnp.sin(ac)], axis=1)
    cos = jnp.concatenate([jnp.ones((CTX_LEN, RET_DH), F32), cos], axis=0)
    sin = jnp.concatenate([jnp.zeros((CTX_LEN, RET_DH), F32), sin], axis=0)
    return jnp.tile(cos, (1, RET_HEADS)), jnp.tile(sin, (1, RET_HEADS))


def kernel(x, c, ctx, c_ctx, w_mod, b_mod, norm_mix_pre, norm_mix_post, norm_ffn_pre, norm_ffn_post,
           w_in, ret_norm, gla_wa2_f, gla_ba_f, gla_wa2_b, gla_ba_b, gla_norm, rw_conv,
           rw_w0_f, rw_w2_f, rw_w0_b, rw_w2_b, rw_a0, rw_a2, rw_g2, rw_k_k, rw_k_a, rw_r_k,
           rw_ln_w, rw_ln_b, w_out, w_ffn_gate, w_ffn_up, w_ffn_down):
    bsz, t_lat, d = x.shape
    depth = w_mod.shape[0]
    assert d == D_MODEL and ctx.shape[1] == CTX_LEN and t_lat % ROW_TILE == 0 and bsz <= 7

    cc = jnp.zeros((8, d), F32).at[0:bsz].set(c).at[bsz].set(c_ctx)
    mods = _mod_call(cc, w_mod, b_mod)
    cos, sin = _rope_tables(t_lat)
    avg_r = _block_diag_ones(RET_W, RET_DH, 1.0 / RET_DH)
    avg_w = _block_diag_ones(RW_W, RW_DH, 1.0 / RW_DH)
    sum_w = _block_diag_ones(RW_W, RW_DH)
    sum_g = _block_diag_ones(GLA_VWP, GLA_DVP)

    xa = jnp.concatenate([ctx, x], axis=1)
    for l in range(depth):
        last = l == depth - 1
        modsel = jnp.stack([jnp.broadcast_to(mods[l, bsz], (bsz, 6 * d)), mods[l, 0:bsz]], axis=1)[:, :, None, :]
        row = lambda a: a[l].reshape(1, -1)

        u_ret, u_gla, u_rw = _proj_call(xa, modsel, row(norm_mix_pre), _bf(_arrange_in(w_in[l])))

        ra = _ret_call(u_ret, cos, sin)

        wa = jnp.stack([_pad_last(jnp.pad(_pad_heads(gla_wa2_f[l], GLA_HEADS, GLA_DK, GLA_DKP),
                                          [(0, 128 - GLA_RANK), (0, 0)]), GLA_KWP),
                        _pad_last(jnp.pad(_pad_heads(gla_wa2_b[l], GLA_HEADS, GLA_DK, GLA_DKP),
                                          [(GLA_RANK, 128 - 2 * GLA_RANK), (0, 0)]), GLA_KWP)])
        ba = jnp.stack([_pad_heads(gla_ba_f[l], GLA_HEADS, GLA_DK, GLA_DKP),
                        _pad_heads(gla_ba_b[l], GLA_HEADS, GLA_DK, GLA_DKP)])[:, None, :]
        ga = _gla_call(u_gla, _bf(wa), ba)

        w2 = jnp.zeros((128, 2 * RW_W), F32).at[0:64, 0:RW_W].set(rw_w2_f[l]).at[64:128, RW_W:].set(rw_w2_b[l])
        prep = {
            "conv": _arrange_rw(rw_conv[l]),
            "a0": row(rw_a0), "a2": _bf(jnp.pad(rw_a2[l], [(0, 64), (0, 0)])), "g2": _bf(rw_g2[l]),
            "w0": jnp.concatenate([rw_w0_f[l], rw_w0_b[l]]).reshape(1, -1), "w2": _bf(w2),
            "k_k": row(rw_k_k), "k_a": row(rw_k_a), "r_k": row(rw_r_k), "bsum": sum_w,
        }
        s1, s2 = _rw_prep_call(u_rw, prep)
        wo = _rw_call(s1)

        wout = w_out[l]
        outp = {
            "ret_norm": row(ret_norm),
            "gla_norm": jnp.tile(_pad_last(gla_norm[l], GLA_DVP), GLA_HEADS).reshape(1, -1),
            "ln_w": row(rw_ln_w), "ln_b": row(rw_ln_b),
            "avg_r": avg_r, "sum_g": sum_g, "avg_w": avg_w,
            "w_out": _bf(jnp.concatenate([
                wout[0:RET_W],
                jnp.pad(wout[RET_W:RET_W + GLA_HEADS * GLA_DV].reshape(GLA_HEADS, GLA_DV, d),
                        [(0, 0), (0, GLA_DVP - GLA_DV), (0, 0)]).reshape(GLA_VWP, d),
                wout[RET_W + GLA_HEADS * GLA_DV:]], axis=0)),
            "norm_mix_post": row(norm_mix_post),
        }
        tile_off = CTX_LEN // ROW_TILE if last else 0
        x1 = _mix_out_call(xa, modsel, ra, u_ret, ga, u_gla, wo, s2, outp, tile_off)
        ffnp = {
            "norm_ffn_pre": row(norm_ffn_pre), "w_gate": _bf(w_ffn_gate[l]), "w_up": _bf(w_ffn_up[l]),
            "w_down": _bf(w_ffn_down[l]), "norm_ffn_post": row(norm_ffn_post),
        }
        xa = _ffn_call(x1, modsel, ffnp, tile_off)
    return xa
```

```python
import functools

import numpy as np
import jax
import jax.numpy as jnp
from jax import lax
from jax.experimental import pallas as pl
from jax.experimental.pallas import tpu as pltpu

F32 = jnp.float32
BF16 = jnp.bfloat16

D_MODEL = 1024
CTX_LEN = 256
GRID_W = 64
ROPE_THETA = 10000.0

RET_HEADS, RET_DH = 4, 64
RET_W = RET_HEADS * RET_DH
GLA_HEADS, GLA_DK, GLA_DV = 4, 48, 96
GLA_DKP, GLA_DVP = 64, 128
GLA_KWP, GLA_VWP = GLA_HEADS * GLA_DKP, GLA_HEADS * GLA_DVP
GLA_RANK = 16
GLA_TAU = 16.0
RW_HEADS, RW_DH = 6, 64
RW_W = RW_HEADS * RW_DH
FFN_HIDDEN = 2816

URET_W = 4 * RET_W
UGLA_W = 2 * GLA_KWP + 2 * GLA_VWP + 128
URW_W = 3 * RW_W + 128 + 128 + 128
NP_IN = URET_W + UGLA_W + URW_W
MIXP_W = RET_W + GLA_VWP + RW_W

CHUNK = 64
ROW_TILE = 256
VMEM_LIMIT = 56 * 1024 * 1024


def _bf(x):
    return x.astype(BF16)


def _dot(a, b):
    return jnp.dot(a, b, preferred_element_type=F32)


def _dot_nt(a, b):
    return lax.dot_general(a, b, (((1,), (1,)), ((), ())), preferred_element_type=F32)


def _dot_tn(a, b):
    return lax.dot_general(a, b, (((0,), (0,)), ((), ())), preferred_element_type=F32)


def _dot_exact(a, b):
    return jnp.dot(a, b, preferred_element_type=F32, precision=lax.Precision.HIGHEST)


def _split_dot(x, m):
    hi = _bf(x)
    lo = _bf(x - hi.astype(F32))
    return _dot(hi, m) + _dot(lo, m)


def _sigmoid(x):
    return 1.0 / (1.0 + jnp.exp(-x))


def _silu(x):
    return x * _sigmoid(x)


def _softplus(x):
    return jnp.maximum(x, 0.0) + jnp.log(1.0 + jnp.exp(-jnp.abs(x)))


def _rms(x, g, eps=1e-6):
    return x * lax.rsqrt(jnp.mean(x * x, axis=-1, keepdims=True) + eps) * g


def _order_masks(c, reverse):
    t = lax.broadcasted_iota(jnp.int32, (c, c), 0)
    s = lax.broadcasted_iota(jnp.int32, (c, c), 1)
    if reverse:
        return s >= t, s > t
    return s <= t, s < t


def _params(*sem):
    return pltpu.CompilerParams(dimension_semantics=sem, vmem_limit_bytes=VMEM_LIMIT)


def _const_spec(shape):
    nd = len(shape)
    return pl.BlockSpec(shape, lambda *_: (0,) * nd)


def _mod_kernel(c_ref, w_ref, b_ref, o_ref):
    c = c_ref[...]
    o_ref[...] = _dot_exact(_silu(c), w_ref[...]) + b_ref[...]


def _mod_call(cc, w_mod, b_mod):
    depth, d, n = w_mod.shape
    tn = 1024
    return pl.pallas_call(
        _mod_kernel,
        grid=(depth, n // tn),
        in_specs=[
            pl.BlockSpec((8, d), lambda l, j: (0, 0)),
            pl.BlockSpec((None, d, tn), lambda l, j: (l, 0, j)),
            pl.BlockSpec((None, 1, tn), lambda l, j: (l, 0, j)),
        ],
        out_specs=pl.BlockSpec((None, 8, tn), lambda l, j: (l, 0, j)),
        out_shape=jax.ShapeDtypeStruct((depth, 8, n), F32),
        compiler_params=_params("arbitrary", "arbitrary"),
        name="mod",
    )(cc, w_mod, b_mod.reshape(depth, 1, n))


def _proj_kernel(x_ref, mod_ref, g_ref, w_ref, ur_ref, ug_ref, uw_ref):
    d = D_MODEL
    h = _rms(x_ref[...], g_ref[...])
    h = _bf(h * (1.0 + mod_ref[:, d:2 * d]) + mod_ref[:, 0:d])
    ur_ref[...] = _dot(h, w_ref[:, 0:URET_W])
    ug_ref[...] = _dot(h, w_ref[:, URET_W:URET_W + UGLA_W])
    uw_ref[...] = _dot(h, w_ref[:, URET_W + UGLA_W:NP_IN])


def _mod_spec():
    return pl.BlockSpec((None, None, 1, 6 * D_MODEL), lambda b, i: (b, jnp.minimum(i, 1), 0, 0))


def _proj_call(xa, modsel, g, w):
    bsz, ta, d = xa.shape
    nt = ta // ROW_TILE
    row = lambda w_: pl.BlockSpec((None, ROW_TILE, w_), lambda b, i: (b, i, 0))
    return pl.pallas_call(
        _proj_kernel,
        grid=(bsz, nt),
        in_specs=[row(d), _mod_spec(), _const_spec((1, d)), _const_spec((d, NP_IN))],
        out_specs=[row(URET_W), row(UGLA_W), row(URW_W)],
        out_shape=[jax.ShapeDtypeStruct((bsz, ta, w_), F32) for w_ in (URET_W, UGLA_W, URW_W)],
        compiler_params=_params("arbitrary", "arbitrary"),
        name="proj",
    )(xa, modsel, g, w)


def _chunk_maps(ta):
    ns = ta // CHUNK
    nc = CTX_LEN // CHUNK

    def fwd(b, s):
        return (b, s, 0)

    def bwd(b, s):
        return (b, jnp.where(s < nc, nc - 1 - s, ns + nc - 1 - s), 0)

    return ns, fwd, bwd


def _chunk_specs(ta, width):
    ns, fwd, bwd = _chunk_maps(ta)
    return (pl.BlockSpec((None, CHUNK, width), fwd), pl.BlockSpec((None, CHUNK, width), bwd))


def _rope_swap(x):
    lane = lax.broadcasted_iota(jnp.int32, x.shape, 1)
    w = x.shape[1]
    return jnp.where(lane % 32 < 16, pltpu.roll(x, w - 16, 1), pltpu.roll(x, 16, 1))


def _ret_kernel(uf_ref, ub_ref, cf_ref, sf_ref, cb_ref, sb_ref, dm_ref, qd_ref, kd_ref, cd_ref,
                of_ref, ob_ref, st_ref):
    @pl.when(pl.program_id(1) == 0)
    def _():
        st_ref[...] = jnp.zeros_like(st_ref)

    dirs = []
    for d, (u_ref, cos_ref, sin_ref) in enumerate(((uf_ref, cf_ref, sf_ref), (ub_ref, cb_ref, sb_ref))):
        cos, sin = cos_ref[...], sin_ref[...]
        q = u_ref[:, 0:RET_W]
        k = u_ref[:, RET_W:2 * RET_W]
        q = q * cos + _rope_swap(q) * sin
        k = (k * cos + _rope_swap(k) * sin) * (RET_DH ** -0.5)
        dirs.append(dict(q=_bf(q), k=_bf(k), qd=_bf(q * qd_ref[d]), kd=_bf(k * kd_ref[d]),
                         v=_bf(u_ref[:, 2 * RET_W:3 * RET_W])))
    chains = [(d, h, dirs[d], slice(h * RET_DH, (h + 1) * RET_DH)) for d in range(2) for h in range(RET_HEADS)]
    sc = [_bf(_dot_nt(p["q"][:, sl], p["k"][:, sl]) * dm_ref[d, h]) for d, h, p, sl in chains]
    s0 = [st_ref[d, h] for d, h, _, _ in chains]
    outs = [_dot(sci, p["v"][:, sl]) + _dot_nt(p["qd"][:, sl], _bf(s)) for sci, s, (_, _, p, sl) in zip(sc, s0, chains)]
    for s, (d, h, p, sl) in zip(s0, chains):
        st_ref[d, h] = s * cd_ref[d, h] + _dot_tn(p["v"][:, sl], p["kd"][:, sl])
    of_ref[...] = jnp.concatenate(outs[0:RET_HEADS], axis=1)
    ob_ref[...] = jnp.concatenate(outs[RET_HEADS:], axis=1)


def _ret_consts():
    c = CHUNK
    lg = np.log1p(-np.exp2(-5.0 - np.arange(RET_HEADS, dtype=np.float64)))
    pos = np.arange(c, dtype=np.float64)
    rel = pos[:, None] - pos[None, :]
    dm = np.zeros((2, RET_HEADS, c, c))
    qd = np.zeros((2, c, RET_W))
    kd = np.zeros((2, c, RET_W))
    cd = np.zeros((2, RET_HEADS, 1, RET_DH))
    for d in range(2):
        lgd = lg if d == 0 else lg[::-1]
        p = pos if d == 0 else c - 1.0 - pos
        for h in range(RET_HEADS):
            intra = np.where(rel >= 0, np.exp(lgd[h] * np.maximum(rel, 0.0)), 0.0)
            dm[d, h] = intra if d == 0 else intra.T
            qd[d, :, h * RET_DH:(h + 1) * RET_DH] = np.exp(lgd[h] * (p + 1.0))[:, None]
            kd[d, :, h * RET_DH:(h + 1) * RET_DH] = np.exp(lgd[h] * (c - 1.0 - p))[:, None]
            cd[d, h] = np.exp(lgd[h] * c)
    return tuple(jnp.asarray(a, F32) for a in (dm, qd, kd, cd))


def _ret_call(u_ret, cos, sin):
    bsz, ta, _ = u_ret.shape
    ns, _, _ = _chunk_maps(ta)
    uf, ub = _chunk_specs(ta, URET_W)
    tf, tb = (pl.BlockSpec((CHUNK, RET_W), lambda b, s, m=m: m(b, s)[1:]) for m in _chunk_maps(ta)[1:])
    of, ob = _chunk_specs(ta, RET_W)
    dm, qd, kd, cd = _ret_consts()
    return pl.pallas_call(
        _ret_kernel,
        grid=(bsz, ns),
        in_specs=[uf, ub, tf, tf, tb, tb, _const_spec(dm.shape), _const_spec(qd.shape),
                  _const_spec(kd.shape), _const_spec(cd.shape)],
        out_specs=[of, ob],
        out_shape=[jax.ShapeDtypeStruct((bsz, ta, RET_W), F32)] * 2,
        scratch_shapes=[pltpu.VMEM((2, RET_HEADS, RET_DH, RET_DH), F32)],
        compiler_params=_params("arbitrary", "arbitrary"),
        name="ret_scan",
    )(u_ret, u_ret, cos, sin, cos, sin, dm, qd, kd, cd)


def _gla_kernel(uf_ref, ub_ref, wa_ref, ba_ref, of_ref, ob_ref, st_ref):
    @pl.when(pl.program_id(1) == 0)
    def _():
        st_ref[...] = jnp.zeros_like(st_ref)

    c = CHUNK
    half = c // 2
    dirs = []
    for d, u_ref in enumerate((uf_ref, ub_ref)):
        rev = d == 1
        incl, _ = _order_masks(c, rev)
        lr = _bf(u_ref[:, 2 * GLA_KWP + 2 * GLA_VWP:UGLA_W])
        z = _dot(lr, wa_ref[d]) + ba_ref[d]
        la = -_softplus(-z) * (1.0 / GLA_TAU)
        g = _dot_exact(incl.astype(F32), la)
        g_end = jnp.sum(la, axis=0, keepdims=True)
        pos = lambda p: c - 1 - p if rev else p
        row = lax.broadcasted_iota(jnp.int32, (c, 1), 0)
        first = (row >= half) if rev else (row < half)
        t_first = lax.broadcasted_iota(jnp.int32, (c, c), 0)
        s_first = lax.broadcasted_iota(jnp.int32, (c, c), 1)
        if rev:
            straddle = jnp.logical_and(t_first < half, s_first >= half)
        else:
            straddle = jnp.logical_and(t_first >= half, s_first < half)
        g_row = lambda p: g[pos(p):pos(p) + 1, :]
        piv = jnp.where(first, g_row(half // 2 - 1), g_row(half + half // 2 - 1))
        g_cut = g_row(half - 1)
        q = u_ref[:, 0:GLA_KWP] * (GLA_DK ** -0.5)
        k = u_ref[:, GLA_KWP:2 * GLA_KWP]
        dirs.append(dict(
            incl=incl, straddle=straddle, e_end=jnp.exp(g_end),
            qt=_bf(q * jnp.exp(g - piv)), kt=_bf(k * jnp.exp(piv - g)),
            qo=_bf(q * jnp.exp(jnp.minimum(g - g_cut, 0.0))), ko=_bf(k * jnp.exp(jnp.minimum(g_cut - g, 0.0))),
            qs=_bf(q * jnp.exp(g)), ks=_bf(k * jnp.exp(g_end - g)),
            v=_bf(u_ref[:, 2 * GLA_KWP:2 * GLA_KWP + GLA_VWP])))
    chains = [(d, h, dirs[d], slice(h * GLA_DKP, (h + 1) * GLA_DKP), slice(h * GLA_DVP, (h + 1) * GLA_DVP))
              for d in range(2) for h in range(GLA_HEADS)]
    same = [_dot_nt(p["qt"][:, sk], p["kt"][:, sk]) for _, _, p, sk, _ in chains]
    cross = [_dot_nt(p["qo"][:, sk], p["ko"][:, sk]) for _, _, p, sk, _ in chains]
    sc = [_bf(jnp.where(p["incl"], jnp.where(p["straddle"], x, s), 0.0))
          for s, x, (_, _, p, _, _) in zip(same, cross, chains)]
    s0 = [st_ref[d, h] for d, h, _, _, _ in chains]
    outs = [_dot(sci, p["v"][:, sv]) + _dot_nt(p["qs"][:, sk], _bf(s))
            for sci, s, (_, _, p, sk, sv) in zip(sc, s0, chains)]
    for s, (d, h, p, sk, sv) in zip(s0, chains):
        st_ref[d, h] = s * p["e_end"][:, sk] + _dot_tn(p["v"][:, sv], p["ks"][:, sk])
    of_ref[...] = jnp.concatenate(outs[0:GLA_HEADS], axis=1)
    ob_ref[...] = jnp.concatenate(outs[GLA_HEADS:], axis=1)


def _gla_call(u_gla, wa, ba):
    bsz, ta, _ = u_gla.shape
    ns, _, _ = _chunk_maps(ta)
    uf, ub = _chunk_specs(ta, UGLA_W)
    of, ob = _chunk_specs(ta, GLA_VWP)
    return pl.pallas_call(
        _gla_kernel,
        grid=(bsz, ns),
        in_specs=[uf, ub, _const_spec(wa.shape), _const_spec(ba.shape)],
        out_specs=[of, ob],
        out_shape=[jax.ShapeDtypeStruct((bsz, ta, GLA_VWP), F32)] * 2,
        scratch_shapes=[pltpu.VMEM((2, GLA_HEADS, GLA_DVP, GLA_DKP), F32)],
        compiler_params=_params("arbitrary", "arbitrary"),
        name="gla_scan",
    )(u_gla, u_gla, wa, ba)


RW_S1_W = 7 * RW_W
RW_S2_W = 2 * RW_W


def _rw_prep_kernel(u_ref, up_ref, un_ref, cw_ref, a0_ref, a2_ref, g2_ref, w0_ref, w2_ref,
                    kk_ref, ka_ref, rk_ref, bsum_ref, s1_ref, s2_ref, *, n_tiles):
    i = pl.program_id(1)
    tm = ROW_TILE
    u = u_ref[...]
    rows = lax.broadcasted_iota(jnp.int32, (tm, 1), 0)
    prev_row = jnp.where(i >= 2, up_ref[7:8, :], 0.0)
    next_row = jnp.where(jnp.logical_and(i >= 1, i < n_tiles - 1), un_ref[0:1, :], 0.0)
    u_prev = jnp.where(rows == 0, prev_row, pltpu.roll(u, 1, 0))
    u_next = jnp.where(rows == tm - 1, next_row, pltpu.roll(u, tm - 1, 0))
    uc = u_prev * cw_ref[0:1, :] + u * cw_ref[1:2, :] + u_next * cw_ref[2:3, :]

    w = RW_W
    r, k, v = uc[:, 0:w], uc[:, w:2 * w], uc[:, 2 * w:3 * w]
    lr_w = uc[:, 3 * w:3 * w + 128]
    lr_a = uc[:, 3 * w + 128:3 * w + 256]
    lr_g = uc[:, 3 * w + 256:3 * w + 384]
    a = _sigmoid(a0_ref[...] + _dot(_bf(lr_a), a2_ref[...]))
    gate = _dot(_bf(_sigmoid(lr_g)), g2_ref[...])
    kk = k * kk_ref[...]
    bsum = bsum_ref[...]
    kk = kk * lax.rsqrt(jnp.maximum(_split_dot(kk * kk, bsum), 1e-24))
    k2 = k * (1.0 + (a - 1.0) * ka_ref[...])
    log_w = -_softplus(-(w0_ref[...] + _dot(_bf(jnp.tanh(lr_w)), w2_ref[...]))) - 0.5
    ld = -jnp.exp(log_w)
    bonus = _split_dot(r * k2 * rk_ref[...], bsum) * v
    s1_ref[:, 0:w] = r
    s1_ref[:, w:2 * w] = k2
    s1_ref[:, 2 * w:3 * w] = v
    s1_ref[:, 3 * w:4 * w] = -kk
    s1_ref[:, 4 * w:5 * w] = kk * a
    s1_ref[:, 5 * w:7 * w] = ld
    s2_ref[:, 0:w] = gate
    s2_ref[:, w:2 * w] = bonus


def _rw_prep_call(u_rw, p):
    bsz, ta, _ = u_rw.shape
    nt = ta // ROW_TILE
    per = ROW_TILE // 8
    nb8 = ta // 8
    row = lambda w_: pl.BlockSpec((None, ROW_TILE, w_), lambda b, i: (b, i, 0))
    prev = pl.BlockSpec((None, 8, URW_W), lambda b, i: (b, jnp.maximum(i * per - 1, 0), 0))
    nxt = pl.BlockSpec((None, 8, URW_W), lambda b, i: (b, jnp.minimum((i + 1) * per, nb8 - 1), 0))
    consts = [p["conv"], p["a0"], p["a2"], p["g2"], p["w0"], p["w2"], p["k_k"], p["k_a"], p["r_k"], p["bsum"]]
    return pl.pallas_call(
        functools.partial(_rw_prep_kernel, n_tiles=nt),
        grid=(bsz, nt),
        in_specs=[row(URW_W), prev, nxt] + [_const_spec(a.shape) for a in consts],
        out_specs=[row(RW_S1_W), row(RW_S2_W)],
        out_shape=[jax.ShapeDtypeStruct((bsz, ta, RW_S1_W), F32),
                   jax.ShapeDtypeStruct((bsz, ta, RW_S2_W), F32)],
        compiler_params=_params("arbitrary", "arbitrary"),
        name="rw_prep",
    )(u_rw, u_rw, u_rw, *consts)


def _rw_kernel(sf_ref, sb_ref, of_ref, ob_ref, st_ref):
    @pl.when(pl.program_id(1) == 0)
    def _():
        st_ref[...] = jnp.zeros_like(st_ref)

    c, w, dh = CHUNK, RW_W, RW_DH
    dirs = []
    for d, s_ref in enumerate((sf_ref, sb_ref)):
        incl, _ = _order_masks(c, d == 1)
        tt = lax.broadcasted_iota(jnp.int32, (2 * c, 2 * c), 0)
        ss = lax.broadcasted_iota(jnp.int32, (2 * c, 2 * c), 1) & (c - 1)
        t_in = tt & (c - 1)
        before = (ss > t_in) if d == 1 else (ss < t_in)
        pair_mask = jnp.logical_or(before, jnp.logical_and(tt >= c, ss == t_in))
        t64 = lax.broadcasted_iota(jnp.int32, (c, c), 0)
        s64 = lax.broadcasted_iota(jnp.int32, (c, c), 1)
        eye = (t64 == s64).astype(F32)
        late, early = (s64, t64) if d == 1 else (t64, s64)
        level_masks = [
            jnp.logical_and((t64 >> (j + 1)) == (s64 >> (j + 1)),
                            jnp.logical_and(((late >> j) & 1) == 1, ((early >> j) & 1) == 0))
            for j in range(6)]
        la = s_ref[:, (5 + d) * w:(6 + d) * w]
        g = _dot_exact(incl.astype(F32), la)
        g_end = jnp.sum(la, axis=0, keepdims=True)
        e_g = jnp.exp(g)
        e_gp = jnp.exp(g - la)
        e_inv = jnp.exp(-g_end)
        e_k = jnp.exp(g_end - g)
        e_end = jnp.exp(g_end)
        r0 = s_ref[:, 0:w] * e_g
        a0 = s_ref[:, 3 * w:4 * w] * e_gp
        rt, at = r0 * e_inv, a0 * e_inv
        kt = s_ref[:, w:2 * w] * e_k
        bt = s_ref[:, 4 * w:5 * w] * e_k
        dirs.append(dict(
            pair_mask=pair_mask, level_masks=level_masks, eye=eye, e_end=e_end, a0=a0, r0=_bf(r0),
            lhs=_bf(jnp.concatenate([at, rt], axis=0)),
            rhs=_bf(jnp.concatenate([bt, kt], axis=0)),
            bt=_bf(bt), kt=_bf(kt), v=_bf(s_ref[:, 2 * w:3 * w])))

    chains = [(d, h, dirs[d], slice(h * dh, (h + 1) * dh)) for d in range(2) for h in range(RW_HEADS)]
    pair = [jnp.where(q["pair_mask"], _dot_nt(q["lhs"][:, sl], q["rhs"][:, sl]), 0.0)
            for _, _, q, sl in chains]
    n = [p[0:c, 0:c] for p in pair]
    x = [jnp.concatenate([_dot(_bf(p[0:c, c:2 * c]), q["v"][:, sl]), q["a0"][:, sl]], axis=1)
         for p, (_, _, q, sl) in zip(pair, chains)]
    tm = [jnp.where(q["level_masks"][0], ni, q["eye"]) for ni, (_, _, q, _) in zip(n, chains)]
    for lvl in range(1, 6):
        tb = [_bf(t) for t in tm]
        left = [_bf(_dot(tbi, _bf(jnp.where(q["level_masks"][lvl], ni, 0.0))))
                for tbi, ni, (_, _, q, _) in zip(tb, n, chains)]
        tm = [t + _dot(li, tbi) for t, li, tbi in zip(tm, left, tb)]
    x = [_dot(_bf(t), _bf(xi)) for t, xi in zip(tm, x)]
    s0 = [st_ref[d, h] for d, h, _, _ in chains]
    s0b = [_bf(s) for s in s0]
    ub = [_bf(xi[:, 0:dh] + _dot_nt(_bf(xi[:, dh:2 * dh]), sb)) for xi, sb in zip(x, s0b)]
    outs = [_dot_nt(q["r0"][:, sl], sb) + _dot(_bf(p[c:2 * c, 0:c]), u) + _dot(_bf(p[c:2 * c, c:2 * c]), q["v"][:, sl])
            for p, u, sb, (_, _, q, sl) in zip(pair, ub, s0b, chains)]
    for s, u, (d, h, q, sl) in zip(s0, ub, chains):
        st_ref[d, h] = s * q["e_end"][:, sl] + _dot_tn(u, q["bt"][:, sl]) + _dot_tn(q["v"][:, sl], q["kt"][:, sl])
    of_ref[...] = jnp.concatenate(outs[0:RW_HEADS], axis=1)
    ob_ref[...] = jnp.concatenate(outs[RW_HEADS:], axis=1)


def _rw_call(s1):
    bsz, ta, _ = s1.shape
    ns, _, _ = _chunk_maps(ta)
    sf, sb = _chunk_specs(ta, RW_S1_W)
    of, ob = _chunk_specs(ta, RW_W)
    return pl.pallas_call(
        _rw_kernel,
        grid=(bsz, ns),
        in_specs=[sf, sb],
        out_specs=[of, ob],
        out_shape=[jax.ShapeDtypeStruct((bsz, ta, RW_W), F32)] * 2,
        scratch_shapes=[pltpu.VMEM((2, RW_HEADS, RW_DH, RW_DH), F32)],
        compiler_params=_params("arbitrary", "arbitrary"),
        name="rw_scan",
    )(s1, s1)


def _mix_out_kernel(x_ref, mod_ref, raf_ref, rab_ref, rg_ref, gaf_ref, gab_ref, gg_ref,
                    waf_ref, wab_ref, s2_ref, rn_ref, gn_ref, lw_ref, lb_ref,
                    avg_r_ref, sum_g_ref, avg_w_ref, wo_ref, pn_ref, o_ref):
    d = D_MODEL
    o = raf_ref[...] + rab_ref[...]
    avg = avg_r_ref[...]
    mu = _split_dot(o, avg)
    oc = o - mu
    var = _split_dot(oc * oc, avg)
    ya = _silu(rg_ref[...]) * (oc * lax.rsqrt(var + 1e-5) * rn_ref[...])
    o = gaf_ref[...] + gab_ref[...]
    ms = _split_dot(o * o, sum_g_ref[...]) * (1.0 / GLA_DV)
    yb = _silu(gg_ref[...]) * (o * lax.rsqrt(ms + 1e-5) * gn_ref[...])
    o = waf_ref[...] + wab_ref[...]
    avg = avg_w_ref[...]
    mu = _split_dot(o, avg)
    oc = o - mu
    var = _split_dot(oc * oc, avg)
    yc = (oc * lax.rsqrt(var + 64e-5) * lw_ref[...] + lb_ref[...] + s2_ref[:, RW_W:2 * RW_W]) * s2_ref[:, 0:RW_W]
    y = (_dot(_bf(ya), wo_ref[0:RET_W, :]) + _dot(_bf(yb), wo_ref[RET_W:RET_W + GLA_VWP, :])
         + _dot(_bf(yc), wo_ref[RET_W + GLA_VWP:MIXP_W, :]))
    o_ref[...] = x_ref[...] + mod_ref[:, 2 * d:3 * d] * _rms(y, pn_ref[...])


def _mix_out_call(xa, modsel, ra, u_ret, ga, u_gla, wa, s2, p, tile_off):
    bsz, ta, d = xa.shape
    nt = ta // ROW_TILE - tile_off
    row = lambda w_, cb=0: pl.BlockSpec((None, ROW_TILE, w_), lambda b, i: (b, i + tile_off, cb))
    mod = pl.BlockSpec((None, None, 1, 6 * d), lambda b, i: (b, jnp.minimum(i + tile_off, 1), 0, 0))
    consts = [p["ret_norm"], p["gla_norm"], p["ln_w"], p["ln_b"], p["avg_r"], p["sum_g"], p["avg_w"],
              p["w_out"], p["norm_mix_post"]]
    return pl.pallas_call(
        _mix_out_kernel,
        grid=(bsz, nt),
        in_specs=[row(d), mod, row(RET_W), row(RET_W), row(RET_W, 3),
                  row(GLA_VWP), row(GLA_VWP), row(GLA_VWP, 2),
                  row(RW_W), row(RW_W), row(RW_S2_W)] + [_const_spec(a.shape) for a in consts],
        out_specs=pl.BlockSpec((None, ROW_TILE, d), lambda b, i: (b, i, 0)),
        out_shape=jax.ShapeDtypeStruct((bsz, nt * ROW_TILE, d), F32),
        compiler_params=_params("arbitrary", "arbitrary"),
        name="mix_out",
    )(xa, modsel, ra[0], ra[1], u_ret, ga[0], ga[1], u_gla, wa[0], wa[1], s2, *consts)


def _ffn_kernel(x_ref, mod_ref, pre_ref, wg_ref, wu_ref, wd_ref, post_ref, o_ref):
    d = D_MODEL
    x = x_ref[...]
    h = _rms(x, pre_ref[...])
    h = _bf(h * (1.0 + mod_ref[:, 4 * d:5 * d]) + mod_ref[:, 3 * d:4 * d])
    act = _bf(_silu(_dot(h, wg_ref[...])) * _dot(h, wu_ref[...]))
    f = _dot(act, wd_ref[...])
    o_ref[...] = x + mod_ref[:, 5 * d:6 * d] * _rms(f, post_ref[...])


def _ffn_call(x1, modsel, p, tile_off):
    bsz, t, d = x1.shape
    nt = t // ROW_TILE
    row = pl.BlockSpec((None, ROW_TILE, d), lambda b, i: (b, i, 0))
    mod = pl.BlockSpec((None, None, 1, 6 * d), lambda b, i: (b, jnp.minimum(i + tile_off, 1), 0, 0))
    consts = [p["norm_ffn_pre"], p["w_gate"], p["w_up"], p["w_down"], p["norm_ffn_post"]]
    return pl.pallas_call(
        _ffn_kernel,
        grid=(bsz, nt),
        in_specs=[row, mod] + [_const_spec(a.shape) for a in consts],
        out_specs=row,
        out_shape=jax.ShapeDtypeStruct((bsz, t, d), F32),
        compiler_params=_params("arbitrary", "arbitrary"),
        name="ffn",
    )(x1, modsel, *consts)


def _pad_last(a, n):
    return jnp.pad(a, [(0, 0)] * (a.ndim - 1) + [(0, n - a.shape[-1])])


def _pad_heads(a, heads, dh, dhp):
    lead = a.shape[:-1]
    a = a.reshape(lead + (heads, dh))
    return _pad_last(a, dhp).reshape(lead + (heads * dhp,))


def _arrange_rw(a):
    w = RW_W
    return jnp.concatenate([a[..., 0:3 * w + 128], _pad_last(a[..., 3 * w + 128:3 * w + 192], 128),
                            a[..., 3 * w + 192:3 * w + 320]], axis=-1)


def _arrange_in(a):
    g0 = URET_W
    kw, vw = GLA_HEADS * GLA_DK, GLA_HEADS * GLA_DV
    r0 = g0 + 2 * kw + 2 * vw + 2 * GLA_RANK
    return jnp.concatenate([
        a[..., 0:g0],
        _pad_heads(a[..., g0:g0 + kw], GLA_HEADS, GLA_DK, GLA_DKP),
        _pad_heads(a[..., g0 + kw:g0 + 2 * kw], GLA_HEADS, GLA_DK, GLA_DKP),
        _pad_heads(a[..., g0 + 2 * kw:g0 + 2 * kw + vw], GLA_HEADS, GLA_DV, GLA_DVP),
        _pad_heads(a[..., g0 + 2 * kw + vw:g0 + 2 * kw + 2 * vw], GLA_HEADS, GLA_DV, GLA_DVP),
        _pad_last(a[..., g0 + 2 * kw + 2 * vw:r0], 128),
        _arrange_rw(a[..., r0:]),
    ], axis=-1)


def _block_diag_ones(n, blk, val=1.0):
    i = np.arange(n)
    return jnp.asarray(np.where(i[:, None] // blk == i[None, :] // blk, val, 0.0), BF16)


def _rope_tables(t_lat):
    rows = t_lat // GRID_W
    row = jnp.repeat(jnp.arange(rows, dtype=F32), GRID_W)
    col = jnp.tile(jnp.arange(GRID_W, dtype=F32), rows)
    nf = RET_DH // 4
    inv = ROPE_THETA ** (-jnp.arange(nf, dtype=F32) / nf)
    ar, ac = row[:, None] * inv, col[:, None] * inv
    cos = jnp.concatenate([jnp.cos(ar), jnp.cos(ar), jnp.cos(ac), jnp.cos(ac)], axis=1)
    sin = jnp.concatenate([-jnp.sin(ar), jnp.sin(ar), -jnp.sin(ac), jnp.sin(ac)], axis=1)
    cos = jnp.concatenate([jnp.ones((CTX_LEN, RET_DH), F32), cos], axis=0)
    sin = jnp.concatenate([jnp.zeros((CTX_LEN, RET_DH), F32), sin], axis=0)
    return jnp.tile(cos, (1, RET_HEADS)), jnp.tile(sin, (1, RET_HEADS))


def kernel(x, c, ctx, c_ctx, w_mod, b_mod, norm_mix_pre, norm_mix_post, norm_ffn_pre, norm_ffn_post,
           w_in, ret_norm, gla_wa2_f, gla_ba_f, gla_wa2_b, gla_ba_b, gla_norm, rw_conv,
           rw_w0_f, rw_w2_f, rw_w0_b, rw_w2_b, rw_a0, rw_a2, rw_g2, rw_k_k, rw_k_a, rw_r_k,
           rw_ln_w, rw_ln_b, w_out, w_ffn_gate, w_ffn_up, w_ffn_down):
    bsz, t_lat, d = x.shape
    depth = w_mod.shape[0]
    assert d == D_MODEL and ctx.shape[1] == CTX_LEN and t_lat % ROW_TILE == 0 and bsz <= 7

    cc = jnp.zeros((8, d), F32).at[0:bsz].set(c).at[bsz].set(c_ctx)
    mods = _mod_call(cc, w_mod, b_mod)
    cos, sin = _rope_tables(t_lat)
    avg_r = _block_diag_ones(RET_W, RET_DH, 1.0 / RET_DH)
    avg_w = _block_diag_ones(RW_W, RW_DH, 1.0 / RW_DH)
    sum_w = _block_diag_ones(RW_W, RW_DH)
    sum_g = _block_diag_ones(GLA_VWP, GLA_DVP)

    xa = jnp.concatenate([ctx, x], axis=1)
    for l in range(depth):
        last = l == depth - 1
        modsel = jnp.stack([jnp.broadcast_to(mods[l, bsz], (bsz, 6 * d)), mods[l, 0:bsz]], axis=1)[:, :, None, :]
        row = lambda a: a[l].reshape(1, -1)

        u_ret, u_gla, u_rw = _proj_call(xa, modsel, row(norm_mix_pre), _bf(_arrange_in(w_in[l])))

        ra = _ret_call(u_ret, cos, sin)

        wa = jnp.stack([_pad_last(jnp.pad(_pad_heads(gla_wa2_f[l], GLA_HEADS, GLA_DK, GLA_DKP),
                                          [(0, 128 - GLA_RANK), (0, 0)]), GLA_KWP),
                        _pad_last(jnp.pad(_pad_heads(gla_wa2_b[l], GLA_HEADS, GLA_DK, GLA_DKP),
                                          [(GLA_RANK, 128 - 2 * GLA_RANK), (0, 0)]), GLA_KWP)])
        ba = jnp.stack([_pad_heads(gla_ba_f[l], GLA_HEADS, GLA_DK, GLA_DKP),
                        _pad_heads(gla_ba_b[l], GLA_HEADS, GLA_DK, GLA_DKP)])[:, None, :]
        ga = _gla_call(u_gla, _bf(wa), ba)

        w2 = jnp.zeros((128, 2 * RW_W), F32).at[0:64, 0:RW_W].set(rw_w2_f[l]).at[64:128, RW_W:].set(rw_w2_b[l])
        prep = {
            "conv": _arrange_rw(rw_conv[l]),
            "a0": row(rw_a0), "a2": _bf(jnp.pad(rw_a2[l], [(0, 64), (0, 0)])), "g2": _bf(rw_g2[l]),
            "w0": jnp.concatenate([rw_w0_f[l], rw_w0_b[l]]).reshape(1, -1), "w2": _bf(w2),
            "k_k": row(rw_k_k), "k_a": row(rw_k_a), "r_k": row(rw_r_k), "bsum": sum_w,
        }
        s1, s2 = _rw_prep_call(u_rw, prep)
        wo = _rw_call(s1)

        wout = w_out[l]
        outp = {
            "ret_norm": row(ret_norm),
            "gla_norm": jnp.tile(_pad_last(gla_norm[l], GLA_DVP), GLA_HEADS).reshape(1, -1),
            "ln_w": row(rw_ln_w), "ln_b": row(rw_ln_b),
            "avg_r": avg_r, "sum_g": sum_g, "avg_w": avg_w,
            "w_out": _bf(jnp.concatenate([
                wout[0:RET_W],
                jnp.pad(wout[RET_W:RET_W + GLA_HEADS * GLA_DV].reshape(GLA_HEADS, GLA_DV, d),
                        [(0, 0), (0, GLA_DVP - GLA_DV), (0, 0)]).reshape(GLA_VWP, d),
                wout[RET_W + GLA_HEADS * GLA_DV:]], axis=0)),
            "norm_mix_post": row(norm_mix_post),
        }
        tile_off = CTX_LEN // ROW_TILE if last else 0
        x1 = _mix_out_call(xa, modsel, ra, u_ret, ga, u_gla, wo, s2, outp, tile_off)
        ffnp = {
            "norm_ffn_pre": row(norm_ffn_pre), "w_gate": _bf(w_ffn_gate[l]), "w_up": _bf(w_ffn_up[l]),
            "w_down": _bf(w_ffn_down[l]), "norm_ffn_post": row(norm_ffn_post),
        }
        xa = _ffn_call(x1, modsel, ffnp, tile_off)
    return xa
```

```python
import functools

import numpy as np
import jax
import jax.numpy as jnp
from jax import lax
from jax.experimental import pallas as pl
from jax.experimental.pallas import tpu as pltpu

F32 = jnp.float32
BF16 = jnp.bfloat16

D_MODEL = 1024
CTX_LEN = 256
GRID_W = 64
ROPE_THETA = 10000.0

RET_HEADS, RET_DH = 4, 64
RET_W = RET_HEADS * RET_DH
GLA_HEADS, GLA_DK, GLA_DV = 4, 48, 96
GLA_DKP, GLA_DVP = 64, 128
GLA_KWP, GLA_VWP = GLA_HEADS * GLA_DKP, GLA_HEADS * GLA_DVP
GLA_RANK = 16
GLA_TAU = 16.0
RW_HEADS, RW_DH = 6, 64
RW_W = RW_HEADS * RW_DH
FFN_HIDDEN = 2816

URET_W = 4 * RET_W
UGLA_W = 2 * GLA_KWP + 2 * GLA_VWP + 128
URW_W = 3 * RW_W + 128 + 128 + 128
NP_IN = URET_W + UGLA_W + URW_W
MIXP_W = RET_W + GLA_VWP + RW_W

CHUNK = 64
RET_CHUNK = 256
ROW_TILE = 256
VMEM_LIMIT = 56 * 1024 * 1024


def _bf(x):
    return x.astype(BF16)


def _dot(a, b):
    return jnp.dot(a, b, preferred_element_type=F32)


def _dot_nt(a, b):
    return lax.dot_general(a, b, (((1,), (1,)), ((), ())), preferred_element_type=F32)


def _dot_tn(a, b):
    return lax.dot_general(a, b, (((0,), (0,)), ((), ())), preferred_element_type=F32)


def _split2(x):
    hi = _bf(x)
    return hi, _bf(x - hi.astype(F32))


def _split_dot(x, m):
    hi, lo = _split2(x)
    return _dot(hi, m) + _dot(lo, m)


def _cumsum_dot(m, x):
    hi = _bf(x)
    r1 = x - hi.astype(F32)
    mid = _bf(r1)
    lo = _bf(r1 - mid.astype(F32))
    return _dot(m, hi) + (_dot(m, mid) + _dot(m, lo))


def _dot_3pass(a, b):
    ah, al = _split2(a)
    bh, bl = _split2(b)
    return _dot(ah, bh) + (_dot(ah, bl) + _dot(al, bh))


def _sigmoid(x):
    return 1.0 / (1.0 + jnp.exp(-x))


def _silu(x):
    return x * _sigmoid(x)


def _softplus(x):
    return jnp.maximum(x, 0.0) + jnp.log(1.0 + jnp.exp(-jnp.abs(x)))


def _rms(x, g, eps=1e-6):
    return x * lax.rsqrt(jnp.mean(x * x, axis=-1, keepdims=True) + eps) * g


def _order_masks(c, reverse):
    t = lax.broadcasted_iota(jnp.int32, (c, c), 0)
    s = lax.broadcasted_iota(jnp.int32, (c, c), 1)
    if reverse:
        return s >= t, s > t
    return s <= t, s < t


def _params(*sem):
    return pltpu.CompilerParams(dimension_semantics=sem, vmem_limit_bytes=VMEM_LIMIT)


def _const_spec(shape):
    nd = len(shape)
    return pl.BlockSpec(shape, lambda *_: (0,) * nd)


def _mod_kernel(c_ref, w_ref, b_ref, o_ref):
    c = c_ref[...]
    o_ref[...] = _dot_3pass(_silu(c), w_ref[...]) + b_ref[...]


def _mod_call(cc, w_mod, b_mod):
    depth, d, n = w_mod.shape
    tn = 1024
    return pl.pallas_call(
        _mod_kernel,
        grid=(depth, n // tn),
        in_specs=[
            pl.BlockSpec((8, d), lambda l, j: (0, 0)),
            pl.BlockSpec((None, d, tn), lambda l, j: (l, 0, j)),
            pl.BlockSpec((None, 1, tn), lambda l, j: (l, 0, j)),
        ],
        out_specs=pl.BlockSpec((None, 8, tn), lambda l, j: (l, 0, j)),
        out_shape=jax.ShapeDtypeStruct((depth, 8, n), F32),
        compiler_params=_params("arbitrary", "arbitrary"),
        name="mod",
    )(cc, w_mod, b_mod.reshape(depth, 1, n))


def _proj_kernel(x_ref, mod_ref, g_ref, w_ref, ur_ref, ug_ref, uw_ref):
    d = D_MODEL
    h = _rms(x_ref[...], g_ref[...])
    h = _bf(h * (1.0 + mod_ref[:, d:2 * d]) + mod_ref[:, 0:d])
    ur_ref[...] = _dot(h, w_ref[:, 0:URET_W])
    ug_ref[...] = _dot(h, w_ref[:, URET_W:URET_W + UGLA_W])
    uw_ref[...] = _dot(h, w_ref[:, URET_W + UGLA_W:NP_IN])


def _mod_spec():
    return pl.BlockSpec((None, None, 1, 6 * D_MODEL), lambda b, i: (b, jnp.minimum(i, 1), 0, 0))


def _proj_call(xa, modsel, g, w):
    bsz, ta, d = xa.shape
    nt = ta // ROW_TILE
    row = lambda w_: pl.BlockSpec((None, ROW_TILE, w_), lambda b, i: (b, i, 0))
    return pl.pallas_call(
        _proj_kernel,
        grid=(bsz, nt),
        in_specs=[row(d), _mod_spec(), _const_spec((1, d)), _const_spec((d, NP_IN))],
        out_specs=[row(URET_W), row(UGLA_W), row(URW_W)],
        out_shape=[jax.ShapeDtypeStruct((bsz, ta, w_), F32) for w_ in (URET_W, UGLA_W, URW_W)],
        compiler_params=_params("arbitrary", "arbitrary"),
        name="proj",
    )(xa, modsel, g, w)


def _chunk_maps(ta, chunk):
    ns = ta // chunk
    nc = CTX_LEN // chunk

    def fwd(s):
        return s

    def bwd(s):
        return jnp.where(s < nc, nc - 1 - s, ns + nc - 1 - s)

    return ns, fwd, bwd


def _chunk_specs(bsz, ta, chunk, width):
    _, fwd, bwd = _chunk_maps(ta, chunk)
    return tuple(pl.BlockSpec((bsz, chunk, width), lambda s, m=m: (0, m(s), 0)) for m in (fwd, bwd))


def _rope_swap(x):
    lane = lax.broadcasted_iota(jnp.int32, x.shape, 1)
    w = x.shape[1]
    return jnp.where(lane % 32 < 16, pltpu.roll(x, w - 16, 1), pltpu.roll(x, 16, 1))


def _ret_kernel(uf_ref, ub_ref, cf_ref, sf_ref, cb_ref, sb_ref, dm_ref, qd_ref, kd_ref, cd_ref,
                of_ref, ob_ref, st_ref):
    @pl.when(pl.program_id(0) == 0)
    def _():
        st_ref[...] = jnp.zeros_like(st_ref)

    bsz = uf_ref.shape[0]
    groups = []
    for d, (u_ref, cos_ref, sin_ref) in enumerate(((uf_ref, cf_ref, sf_ref), (ub_ref, cb_ref, sb_ref))):
        cos, sin = cos_ref[...], sin_ref[...]
        for b in range(bsz):
            q = u_ref[b, :, 0:RET_W]
            k = u_ref[b, :, RET_W:2 * RET_W]
            q = q * cos + _rope_swap(q) * sin
            k = (k * cos + _rope_swap(k) * sin) * (RET_DH ** -0.5)
            groups.append(dict(d=d, b=b, q=_bf(q), k=_bf(k), qd=_bf(q * qd_ref[d]), kd=_bf(k * kd_ref[d]),
                               v=_bf(u_ref[b, :, 2 * RET_W:3 * RET_W])))
    chains = [(p, h, slice(h * RET_DH, (h + 1) * RET_DH)) for p in groups for h in range(RET_HEADS)]
    sc = [_bf(_dot_nt(p["q"][:, sl], p["k"][:, sl]) * dm_ref[p["d"], h]) for p, h, sl in chains]
    s0 = [st_ref[p["d"], p["b"], h] for p, h, _ in chains]
    outs = [_dot(sci, p["v"][:, sl]) + _dot_nt(p["qd"][:, sl], _bf(s)) for sci, s, (p, _, sl) in zip(sc, s0, chains)]
    for s, (p, h, sl) in zip(s0, chains):
        st_ref[p["d"], p["b"], h] = s * cd_ref[p["d"], h] + _dot_tn(p["v"][:, sl], p["kd"][:, sl])
    for gi, p in enumerate(groups):
        o_ref = ob_ref if p["d"] else of_ref
        o_ref[p["b"]] = jnp.concatenate(outs[gi * RET_HEADS:(gi + 1) * RET_HEADS], axis=1)


def _ret_consts():
    c = RET_CHUNK
    lg = np.log1p(-np.exp2(-5.0 - np.arange(RET_HEADS, dtype=np.float64)))
    pos = np.arange(c, dtype=np.float64)
    rel = pos[:, None] - pos[None, :]
    dm = np.zeros((2, RET_HEADS, c, c))
    qd = np.zeros((2, c, RET_W))
    kd = np.zeros((2, c, RET_W))
    cd = np.zeros((2, RET_HEADS, 1, RET_DH))
    for d in range(2):
        lgd = lg if d == 0 else lg[::-1]
        p = pos if d == 0 else c - 1.0 - pos
        for h in range(RET_HEADS):
            intra = np.where(rel >= 0, np.exp(lgd[h] * np.maximum(rel, 0.0)), 0.0)
            dm[d, h] = intra if d == 0 else intra.T
            qd[d, :, h * RET_DH:(h + 1) * RET_DH] = np.exp(lgd[h] * (p + 1.0))[:, None]
            kd[d, :, h * RET_DH:(h + 1) * RET_DH] = np.exp(lgd[h] * (c - 1.0 - p))[:, None]
            cd[d, h] = np.exp(lgd[h] * c)
    return tuple(jnp.asarray(a, F32) for a in (dm, qd, kd, cd))


def _ret_call(u_ret, cos, sin):
    bsz, ta, _ = u_ret.shape
    c = RET_CHUNK
    ns, fwd, bwd = _chunk_maps(ta, c)
    uf, ub = _chunk_specs(bsz, ta, c, URET_W)
    tf, tb = (pl.BlockSpec((c, RET_W), lambda s, m=m: (m(s), 0)) for m in (fwd, bwd))
    of, ob = _chunk_specs(bsz, ta, c, RET_W)
    dm, qd, kd, cd = _ret_consts()
    return pl.pallas_call(
        _ret_kernel,
        grid=(ns,),
        in_specs=[uf, ub, tf, tf, tb, tb, _const_spec(dm.shape), _const_spec(qd.shape),
                  _const_spec(kd.shape), _const_spec(cd.shape)],
        out_specs=[of, ob],
        out_shape=[jax.ShapeDtypeStruct((bsz, ta, RET_W), F32)] * 2,
        scratch_shapes=[pltpu.VMEM((2, bsz, RET_HEADS, RET_DH, RET_DH), F32)],
        compiler_params=_params("arbitrary"),
        name="ret_scan",
    )(u_ret, u_ret, cos, sin, cos, sin, dm, qd, kd, cd)


def _gla_kernel(uf_ref, ub_ref, wa_ref, ba_ref, of_ref, ob_ref, st_ref):
    @pl.when(pl.program_id(0) == 0)
    def _():
        st_ref[...] = jnp.zeros_like(st_ref)

    bsz = uf_ref.shape[0]
    c = CHUNK
    half = c // 2
    groups = []
    for d, u_ref in enumerate((uf_ref, ub_ref)):
        rev = d == 1
        incl, _ = _order_masks(c, rev)
        incl_b = incl.astype(BF16)
        pos = lambda p: c - 1 - p if rev else p
        row = lax.broadcasted_iota(jnp.int32, (c, 1), 0)
        first = (row >= half) if rev else (row < half)
        ti = lax.broadcasted_iota(jnp.int32, (c, c), 0)
        si = lax.broadcasted_iota(jnp.int32, (c, c), 1)
        if rev:
            straddle = jnp.logical_and(ti < half, si >= half)
        else:
            straddle = jnp.logical_and(ti >= half, si < half)
        for b in range(bsz):
            lr = _bf(u_ref[b, :, 2 * GLA_KWP + 2 * GLA_VWP:UGLA_W])
            z = _dot(lr, wa_ref[d]) + ba_ref[d]
            la = -_softplus(-z) * (1.0 / GLA_TAU)
            g = _cumsum_dot(incl_b, la)
            g_end = jnp.sum(la, axis=0, keepdims=True)
            g_row = lambda p: g[pos(p):pos(p) + 1, :]
            piv = jnp.where(first, g_row(half // 2 - 1), g_row(half + half // 2 - 1))
            g_cut = g_row(half - 1)
            q = u_ref[b, :, 0:GLA_KWP] * (GLA_DK ** -0.5)
            k = u_ref[b, :, GLA_KWP:2 * GLA_KWP]
            groups.append(dict(
                d=d, b=b, incl=incl, straddle=straddle, e_end=jnp.exp(g_end),
                qt=_bf(q * jnp.exp(g - piv)), kt=_bf(k * jnp.exp(piv - g)),
                qo=_bf(q * jnp.exp(jnp.minimum(g - g_cut, 0.0))), ko=_bf(k * jnp.exp(jnp.minimum(g_cut - g, 0.0))),
                qs=_bf(q * jnp.exp(g)), ks=_bf(k * jnp.exp(g_end - g)),
                v=_bf(u_ref[b, :, 2 * GLA_KWP:2 * GLA_KWP + GLA_VWP])))
    chains = [(p, h, slice(h * GLA_DKP, (h + 1) * GLA_DKP), slice(h * GLA_DVP, (h + 1) * GLA_DVP))
              for p in groups for h in range(GLA_HEADS)]
    same = [_dot_nt(p["qt"][:, sk], p["kt"][:, sk]) for p, _, sk, _ in chains]
    cross = [_dot_nt(p["qo"][:, sk], p["ko"][:, sk]) for p, _, sk, _ in chains]
    sc = [_bf(jnp.where(p["incl"], jnp.where(p["straddle"], x, s), 0.0))
          for s, x, (p, _, _, _) in zip(same, cross, chains)]
    s0 = [st_ref[p["d"], p["b"], h] for p, h, _, _ in chains]
    outs = [_dot(sci, p["v"][:, sv]) + _dot_nt(p["qs"][:, sk], _bf(s))
            for sci, s, (p, _, sk, sv) in zip(sc, s0, chains)]
    for s, (p, h, sk, sv) in zip(s0, chains):
        st_ref[p["d"], p["b"], h] = s * p["e_end"][:, sk] + _dot_tn(p["v"][:, sv], p["ks"][:, sk])
    for gi, p in enumerate(groups):
        o_ref = ob_ref if p["d"] else of_ref
        o_ref[p["b"]] = jnp.concatenate(outs[gi * GLA_HEADS:(gi + 1) * GLA_HEADS], axis=1)


def _gla_call(u_gla, wa, ba):
    bsz, ta, _ = u_gla.shape
    ns, _, _ = _chunk_maps(ta, CHUNK)
    uf, ub = _chunk_specs(bsz, ta, CHUNK, UGLA_W)
    of, ob = _chunk_specs(bsz, ta, CHUNK, GLA_VWP)
    return pl.pallas_call(
        _gla_kernel,
        grid=(ns,),
        in_specs=[uf, ub, _const_spec(wa.shape), _const_spec(ba.shape)],
        out_specs=[of, ob],
        out_shape=[jax.ShapeDtypeStruct((bsz, ta, GLA_VWP), F32)] * 2,
        scratch_shapes=[pltpu.VMEM((2, bsz, GLA_HEADS, GLA_DVP, GLA_DKP), F32)],
        compiler_params=_params("arbitrary"),
        name="gla_scan",
    )(u_gla, u_gla, wa, ba)


RW_S1_W = 7 * RW_W
RW_S2_W = 2 * RW_W


def _rw_prep_kernel(u_ref, up_ref, un_ref, cw_ref, a0_ref, a2_ref, g2_ref, w0_ref, w2_ref,
                    kk_ref, ka_ref, rk_ref, bsum_ref, s1_ref, s2_ref, *, n_tiles):
    i = pl.program_id(1)
    tm = ROW_TILE
    u = u_ref[...]
    rows = lax.broadcasted_iota(jnp.int32, (tm, 1), 0)
    prev_row = jnp.where(i >= 2, up_ref[7:8, :], 0.0)
    next_row = jnp.where(jnp.logical_and(i >= 1, i < n_tiles - 1), un_ref[0:1, :], 0.0)
    u_prev = jnp.where(rows == 0, prev_row, pltpu.roll(u, 1, 0))
    u_next = jnp.where(rows == tm - 1, next_row, pltpu.roll(u, tm - 1, 0))
    uc = u_prev * cw_ref[0:1, :] + u * cw_ref[1:2, :] + u_next * cw_ref[2:3, :]

    w = RW_W
    r, k, v = uc[:, 0:w], uc[:, w:2 * w], uc[:, 2 * w:3 * w]
    lr_w = uc[:, 3 * w:3 * w + 128]
    lr_a = uc[:, 3 * w + 128:3 * w + 256]
    lr_g = uc[:, 3 * w + 256:3 * w + 384]
    a = _sigmoid(a0_ref[...] + _dot(_bf(lr_a), a2_ref[...]))
    gate = _dot(_bf(_sigmoid(lr_g)), g2_ref[...])
    kk = k * kk_ref[...]
    bsum = bsum_ref[...]
    kk = kk * lax.rsqrt(jnp.maximum(_split_dot(kk * kk, bsum), 1e-24))
    k2 = k * (1.0 + (a - 1.0) * ka_ref[...])
    log_w = -_softplus(-(w0_ref[...] + _dot(_bf(jnp.tanh(lr_w)), w2_ref[...]))) - 0.5
    ld = -jnp.exp(log_w)
    bonus = _split_dot(r * k2 * rk_ref[...], bsum) * v
    s1_ref[:, 0:w] = r
    s1_ref[:, w:2 * w] = k2
    s1_ref[:, 2 * w:3 * w] = v
    s1_ref[:, 3 * w:4 * w] = -kk
    s1_ref[:, 4 * w:5 * w] = kk * a
    s1_ref[:, 5 * w:7 * w] = ld
    s2_ref[:, 0:w] = gate
    s2_ref[:, w:2 * w] = bonus


def _rw_prep_call(u_rw, p):
    bsz, ta, _ = u_rw.shape
    nt = ta // ROW_TILE
    per = ROW_TILE // 8
    nb8 = ta // 8
    row = lambda w_: pl.BlockSpec((None, ROW_TILE, w_), lambda b, i: (b, i, 0))
    prev = pl.BlockSpec((None, 8, URW_W), lambda b, i: (b, jnp.maximum(i * per - 1, 0), 0))
    nxt = pl.BlockSpec((None, 8, URW_W), lambda b, i: (b, jnp.minimum((i + 1) * per, nb8 - 1), 0))
    consts = [p["conv"], p["a0"], p["a2"], p["g2"], p["w0"], p["w2"], p["k_k"], p["k_a"], p["r_k"], p["bsum"]]
    return pl.pallas_call(
        functools.partial(_rw_prep_kernel, n_tiles=nt),
        grid=(bsz, nt),
        in_specs=[row(URW_W), prev, nxt] + [_const_spec(a.shape) for a in consts],
        out_specs=[row(RW_S1_W), row(RW_S2_W)],
        out_shape=[jax.ShapeDtypeStruct((bsz, ta, RW_S1_W), F32),
                   jax.ShapeDtypeStruct((bsz, ta, RW_S2_W), F32)],
        compiler_params=_params("arbitrary", "arbitrary"),
        name="rw_prep",
    )(u_rw, u_rw, u_rw, *consts)


def _rw_kernel(sf_ref, sb_ref, of_ref, ob_ref, st_ref):
    @pl.when(pl.program_id(0) == 0)
    def _():
        st_ref[...] = jnp.zeros_like(st_ref)

    bsz = sf_ref.shape[0]
    c, w, dh = CHUNK, RW_W, RW_DH
    groups = []
    for d, s_ref in enumerate((sf_ref, sb_ref)):
        incl, _ = _order_masks(c, d == 1)
        incl_b = incl.astype(BF16)
        tt = lax.broadcasted_iota(jnp.int32, (2 * c, 2 * c), 0)
        ss = lax.broadcasted_iota(jnp.int32, (2 * c, 2 * c), 1) & (c - 1)
        t_in = tt & (c - 1)
        before = (ss > t_in) if d == 1 else (ss < t_in)
        pair_mask = jnp.logical_or(before, jnp.logical_and(tt >= c, ss == t_in))
        t64 = lax.broadcasted_iota(jnp.int32, (c, c), 0)
        s64 = lax.broadcasted_iota(jnp.int32, (c, c), 1)
        eye = (t64 == s64).astype(F32)
        late, early = (s64, t64) if d == 1 else (t64, s64)
        level_masks = [
            jnp.logical_and((t64 >> (j + 1)) == (s64 >> (j + 1)),
                            jnp.logical_and(((late >> j) & 1) == 1, ((early >> j) & 1) == 0))
            for j in range(6)]
        for b in range(bsz):
            la = s_ref[b, :, (5 + d) * w:(6 + d) * w]
            g = _cumsum_dot(incl_b, la)
            g_end = jnp.sum(la, axis=0, keepdims=True)
            e_g = jnp.exp(g)
            e_gp = jnp.exp(g - la)
            e_inv = jnp.exp(-g_end)
            e_k = jnp.exp(g_end - g)
            e_end = jnp.exp(g_end)
            r0 = s_ref[b, :, 0:w] * e_g
            a0 = s_ref[b, :, 3 * w:4 * w] * e_gp
            rt, at = r0 * e_inv, a0 * e_inv
            kt = s_ref[b, :, w:2 * w] * e_k
            bt = s_ref[b, :, 4 * w:5 * w] * e_k
            groups.append(dict(
                d=d, b=b, pair_mask=pair_mask, level_masks=level_masks, eye=eye, e_end=e_end, a0=a0, r0=_bf(r0),
                lhs=_bf(jnp.concatenate([at, rt], axis=0)),
                rhs=_bf(jnp.concatenate([bt, kt], axis=0)),
                bt=_bf(bt), kt=_bf(kt), v=_bf(s_ref[b, :, 2 * w:3 * w])))

    chains = [(q, h, slice(h * dh, (h + 1) * dh)) for q in groups for h in range(RW_HEADS)]
    pair = [jnp.where(q["pair_mask"], _dot_nt(q["lhs"][:, sl], q["rhs"][:, sl]), 0.0)
            for q, _, sl in chains]
    n = [p[0:c, 0:c] for p in pair]
    x = [jnp.concatenate([_dot(_bf(p[0:c, c:2 * c]), q["v"][:, sl]), q["a0"][:, sl]], axis=1)
         for p, (q, _, sl) in zip(pair, chains)]
    tm = [jnp.where(q["level_masks"][0], ni, q["eye"]) for ni, (q, _, _) in zip(n, chains)]
    for lvl in range(1, 6):
        tb = [_bf(t) for t in tm]
        left = [_bf(_dot(tbi, _bf(jnp.where(q["level_masks"][lvl], ni, 0.0))))
                for tbi, ni, (q, _, _) in zip(tb, n, chains)]
        tm = [t + _dot(li, tbi) for t, li, tbi in zip(tm, left, tb)]
    x = [_dot(_bf(t), _bf(xi)) for t, xi in zip(tm, x)]
    s0 = [st_ref[q["d"], q["b"], h] for q, h, _ in chains]
    s0b = [_bf(s) for s in s0]
    ub = [_bf(xi[:, 0:dh] + _dot_nt(_bf(xi[:, dh:2 * dh]), sb)) for xi, sb in zip(x, s0b)]
    outs = [_dot_nt(q["r0"][:, sl], sb) + _dot(_bf(p[c:2 * c, 0:c]), u) + _dot(_bf(p[c:2 * c, c:2 * c]), q["v"][:, sl])
            for p, u, sb, (q, _, sl) in zip(pair, ub, s0b, chains)]
    for s, u, (q, h, sl) in zip(s0, ub, chains):
        st_ref[q["d"], q["b"], h] = (s * q["e_end"][:, sl] + _dot_tn(u, q["bt"][:, sl])
                                     + _dot_tn(q["v"][:, sl], q["kt"][:, sl]))
    for gi, q in enumerate(groups):
        o_ref = ob_ref if q["d"] else of_ref
        o_ref[q["b"]] = jnp.concatenate(outs[gi * RW_HEADS:(gi + 1) * RW_HEADS], axis=1)


def _rw_call(s1):
    bsz, ta, _ = s1.shape
    ns, _, _ = _chunk_maps(ta, CHUNK)
    sf, sb = _chunk_specs(bsz, ta, CHUNK, RW_S1_W)
    of, ob = _chunk_specs(bsz, ta, CHUNK, RW_W)
    return pl.pallas_call(
        _rw_kernel,
        grid=(ns,),
        in_specs=[sf, sb],
        out_specs=[of, ob],
        out_shape=[jax.ShapeDtypeStruct((bsz, ta, RW_W), F32)] * 2,
        scratch_shapes=[pltpu.VMEM((2, bsz, RW_HEADS, RW_DH, RW_DH), F32)],
        compiler_params=_params("arbitrary"),
        name="rw_scan",
    )(s1, s1)


def _mix_out_kernel(x_ref, mod_ref, raf_ref, rab_ref, rg_ref, gaf_ref, gab_ref, gg_ref,
                    waf_ref, wab_ref, s2_ref, rn_ref, gn_ref, lw_ref, lb_ref,
                    avg_r_ref, sum_g_ref, avg_w_ref, wo_ref, pn_ref, o_ref):
    d = D_MODEL
    o = raf_ref[...] + rab_ref[...]
    avg = avg_r_ref[...]
    mu = _split_dot(o, avg)
    oc = o - mu
    var = _split_dot(oc * oc, avg)
    ya = _silu(rg_ref[...]) * (oc * lax.rsqrt(var + 1e-5) * rn_ref[...])
    o = gaf_ref[...] + gab_ref[...]
    ms = _split_dot(o * o, sum_g_ref[...]) * (1.0 / GLA_DV)
    yb = _silu(gg_ref[...]) * (o * lax.rsqrt(ms + 1e-5) * gn_ref[...])
    o = waf_ref[...] + wab_ref[...]
    avg = avg_w_ref[...]
    mu = _split_dot(o, avg)
    oc = o - mu
    var = _split_dot(oc * oc, avg)
    yc = (oc * lax.rsqrt(var + 64e-5) * lw_ref[...] + lb_ref[...] + s2_ref[:, RW_W:2 * RW_W]) * s2_ref[:, 0:RW_W]
    y = (_dot(_bf(ya), wo_ref[0:RET_W, :]) + _dot(_bf(yb), wo_ref[RET_W:RET_W + GLA_VWP, :])
         + _dot(_bf(yc), wo_ref[RET_W + GLA_VWP:MIXP_W, :]))
    o_ref[...] = x_ref[...] + mod_ref[:, 2 * d:3 * d] * _rms(y, pn_ref[...])


def _mix_out_call(xa, modsel, ra, u_ret, ga, u_gla, wa, s2, p, tile_off):
    bsz, ta, d = xa.shape
    nt = ta // ROW_TILE - tile_off
    row = lambda w_, cb=0: pl.BlockSpec((None, ROW_TILE, w_), lambda b, i: (b, i + tile_off, cb))
    mod = pl.BlockSpec((None, None, 1, 6 * d), lambda b, i: (b, jnp.minimum(i + tile_off, 1), 0, 0))
    consts = [p["ret_norm"], p["gla_norm"], p["ln_w"], p["ln_b"], p["avg_r"], p["sum_g"], p["avg_w"],
              p["w_out"], p["norm_mix_post"]]
    return pl.pallas_call(
        _mix_out_kernel,
        grid=(bsz, nt),
        in_specs=[row(d), mod, row(RET_W), row(RET_W), row(RET_W, 3),
                  row(GLA_VWP), row(GLA_VWP), row(GLA_VWP, 2),
                  row(RW_W), row(RW_W), row(RW_S2_W)] + [_const_spec(a.shape) for a in consts],
        out_specs=pl.BlockSpec((None, ROW_TILE, d), lambda b, i: (b, i, 0)),
        out_shape=jax.ShapeDtypeStruct((bsz, nt * ROW_TILE, d), F32),
        compiler_params=_params("arbitrary", "arbitrary"),
        name="mix_out",
    )(xa, modsel, ra[0], ra[1], u_ret, ga[0], ga[1], u_gla, wa[0], wa[1], s2, *consts)


def _ffn_kernel(x_ref, mod_ref, pre_ref, wg_ref, wu_ref, wd_ref, post_ref, o_ref):
    d = D_MODEL
    x = x_ref[...]
    h = _rms(x, pre_ref[...])
    h = _bf(h * (1.0 + mod_ref[:, 4 * d:5 * d]) + mod_ref[:, 3 * d:4 * d])
    act = _bf(_silu(_dot(h, wg_ref[...])) * _dot(h, wu_ref[...]))
    f = _dot(act, wd_ref[...])
    o_ref[...] = x + mod_ref[:, 5 * d:6 * d] * _rms(f, post_ref[...])


def _ffn_call(x1, modsel, p, tile_off):
    bsz, t, d = x1.shape
    nt = t // ROW_TILE
    row = pl.BlockSpec((None, ROW_TILE, d), lambda b, i: (b, i, 0))
    mod = pl.BlockSpec((None, None, 1, 6 * d), lambda b, i: (b, jnp.minimum(i + tile_off, 1), 0, 0))
    consts = [p["norm_ffn_pre"], p["w_gate"], p["w_up"], p["w_down"], p["norm_ffn_post"]]
    return pl.pallas_call(
        _ffn_kernel,
        grid=(bsz, nt),
        in_specs=[row, mod] + [_const_spec(a.shape) for a in consts],
        out_specs=row,
        out_shape=jax.ShapeDtypeStruct((bsz, t, d), F32),
        compiler_params=_params("arbitrary", "arbitrary"),
        name="ffn",
    )(x1, modsel, *consts)


def _pad_last(a, n):
    return jnp.pad(a, [(0, 0)] * (a.ndim - 1) + [(0, n - a.shape[-1])])


def _pad_heads(a, heads, dh, dhp):
    lead = a.shape[:-1]
    a = a.reshape(lead + (heads, dh))
    return _pad_last(a, dhp).reshape(lead + (heads * dhp,))


def _arrange_rw(a):
    w = RW_W
    return jnp.concatenate([a[..., 0:3 * w + 128], _pad_last(a[..., 3 * w + 128:3 * w + 192], 128),
                            a[..., 3 * w + 192:3 * w + 320]], axis=-1)


def _arrange_in(a):
    g0 = URET_W
    kw, vw = GLA_HEADS * GLA_DK, GLA_HEADS * GLA_DV
    r0 = g0 + 2 * kw + 2 * vw + 2 * GLA_RANK
    return jnp.concatenate([
        a[..., 0:g0],
        _pad_heads(a[..., g0:g0 + kw], GLA_HEADS, GLA_DK, GLA_DKP),
        _pad_heads(a[..., g0 + kw:g0 + 2 * kw], GLA_HEADS, GLA_DK, GLA_DKP),
        _pad_heads(a[..., g0 + 2 * kw:g0 + 2 * kw + vw], GLA_HEADS, GLA_DV, GLA_DVP),
        _pad_heads(a[..., g0 + 2 * kw + vw:g0 + 2 * kw + 2 * vw], GLA_HEADS, GLA_DV, GLA_DVP),
        _pad_last(a[..., g0 + 2 * kw + 2 * vw:r0], 128),
        _arrange_rw(a[..., r0:]),
    ], axis=-1)


def _block_diag_ones(n, blk, val=1.0):
    i = np.arange(n)
    return jnp.asarray(np.where(i[:, None] // blk == i[None, :] // blk, val, 0.0), BF16)


def _rope_tables(t_lat):
    rows = t_lat // GRID_W
    row = jnp.repeat(jnp.arange(rows, dtype=F32), GRID_W)
    col = jnp.tile(jnp.arange(GRID_W, dtype=F32), rows)
    nf = RET_DH // 4
    inv = ROPE_THETA ** (-jnp.arange(nf, dtype=F32) / nf)
    ar, ac = row[:, None] * inv, col[:, None] * inv
    cos = jnp.concatenate([jnp.cos(ar), jnp.cos(ar), jnp.cos(ac), jnp.cos(ac)], axis=1)
    sin = jnp.concatenate([-jnp.sin(ar), jnp.sin(ar), -jnp.sin(ac), jnp.sin(ac)], axis=1)
    cos = jnp.concatenate([jnp.ones((CTX_LEN, RET_DH), F32), cos], axis=0)
    sin = jnp.concatenate([jnp.zeros((CTX_LEN, RET_DH), F32), sin], axis=0)
    return jnp.tile(cos, (1, RET_HEADS)), jnp.tile(sin, (1, RET_HEADS))


def kernel(x, c, ctx, c_ctx, w_mod, b_mod, norm_mix_pre, norm_mix_post, norm_ffn_pre, norm_ffn_post,
           w_in, ret_norm, gla_wa2_f, gla_ba_f, gla_wa2_b, gla_ba_b, gla_norm, rw_conv,
           rw_w0_f, rw_w2_f, rw_w0_b, rw_w2_b, rw_a0, rw_a2, rw_g2, rw_k_k, rw_k_a, rw_r_k,
           rw_ln_w, rw_ln_b, w_out, w_ffn_gate, w_ffn_up, w_ffn_down):
    bsz, t_lat, d = x.shape
    depth = w_mod.shape[0]
    assert d == D_MODEL and ctx.shape[1] == CTX_LEN and t_lat % ROW_TILE == 0 and bsz <= 7

    cc = jnp.zeros((8, d), F32).at[0:bsz].set(c).at[bsz].set(c_ctx)
    mods = _mod_call(cc, w_mod, b_mod)
    cos, sin = _rope_tables(t_lat)
    avg_r = _block_diag_ones(RET_W, RET_DH, 1.0 / RET_DH)
    avg_w = _block_diag_ones(RW_W, RW_DH, 1.0 / RW_DH)
    sum_w = _block_diag_ones(RW_W, RW_DH)
    sum_g = _block_diag_ones(GLA_VWP, GLA_DVP)

    xa = jnp.concatenate([ctx, x], axis=1)
    for l in range(depth):
        last = l == depth - 1
        modsel = jnp.stack([jnp.broadcast_to(mods[l, bsz], (bsz, 6 * d)), mods[l, 0:bsz]], axis=1)[:, :, None, :]
        row = lambda a: a[l].reshape(1, -1)

        u_ret, u_gla, u_rw = _proj_call(xa, modsel, row(norm_mix_pre), _bf(_arrange_in(w_in[l])))

        ra = _ret_call(u_ret, cos, sin)

        wa = jnp.stack([_pad_last(jnp.pad(_pad_heads(gla_wa2_f[l], GLA_HEADS, GLA_DK, GLA_DKP),
                                          [(0, 128 - GLA_RANK), (0, 0)]), GLA_KWP),
                        _pad_last(jnp.pad(_pad_heads(gla_wa2_b[l], GLA_HEADS, GLA_DK, GLA_DKP),
                                          [(GLA_RANK, 128 - 2 * GLA_RANK), (0, 0)]), GLA_KWP)])
        ba = jnp.stack([_pad_heads(gla_ba_f[l], GLA_HEADS, GLA_DK, GLA_DKP),
                        _pad_heads(gla_ba_b[l], GLA_HEADS, GLA_DK, GLA_DKP)])[:, None, :]
        ga = _gla_call(u_gla, _bf(wa), ba)

        w2 = jnp.zeros((128, 2 * RW_W), F32).at[0:64, 0:RW_W].set(rw_w2_f[l]).at[64:128, RW_W:].set(rw_w2_b[l])
        prep = {
            "conv": _arrange_rw(rw_conv[l]),
            "a0": row(rw_a0), "a2": _bf(jnp.pad(rw_a2[l], [(0, 64), (0, 0)])), "g2": _bf(rw_g2[l]),
            "w0": jnp.concatenate([rw_w0_f[l], rw_w0_b[l]]).reshape(1, -1), "w2": _bf(w2),
            "k_k": row(rw_k_k), "k_a": row(rw_k_a), "r_k": row(rw_r_k), "bsum": sum_w,
        }
        s1, s2 = _rw_prep_call(u_rw, prep)
        wo = _rw_call(s1)

        wout = w_out[l]
        outp = {
            "ret_norm": row(ret_norm),
            "gla_norm": jnp.tile(_pad_last(gla_norm[l], GLA_DVP), GLA_HEADS).reshape(1, -1),
            "ln_w": row(rw_ln_w), "ln_b": row(rw_ln_b),
            "avg_r": avg_r, "sum_g": sum_g, "avg_w": avg_w,
            "w_out": _bf(jnp.concatenate([
                wout[0:RET_W],
                jnp.pad(wout[RET_W:RET_W + GLA_HEADS * GLA_DV].reshape(GLA_HEADS, GLA_DV, d),
                        [(0, 0), (0, GLA_DVP - GLA_DV), (0, 0)]).reshape(GLA_VWP, d),
                wout[RET_W + GLA_HEADS * GLA_DV:]], axis=0)),
            "norm_mix_post": row(norm_mix_post),
        }
        tile_off = CTX_LEN // ROW_TILE if last else 0
        x1 = _mix_out_call(xa, modsel, ra, u_ret, ga, u_gla, wo, s2, outp, tile_off)
        ffnp = {
            "norm_ffn_pre": row(norm_ffn_pre), "w_gate": _bf(w_ffn_gate[l]), "w_up": _bf(w_ffn_up[l]),
            "w_down": _bf(w_ffn_down[l]), "norm_ffn_post": row(norm_ffn_post),
        }
        xa = _ffn_call(x1, modsel, ffnp, tile_off)
    return xa
```

```python
import functools

import numpy as np
import jax
import jax.numpy as jnp
from jax import lax
from jax.experimental import pallas as pl
from jax.experimental.pallas import tpu as pltpu

F32 = jnp.float32
BF16 = jnp.bfloat16

D_MODEL = 1024
CTX_LEN = 256
GRID_W = 64
ROPE_THETA = 10000.0

RET_HEADS, RET_DH = 4, 64
RET_W = RET_HEADS * RET_DH
GLA_HEADS, GLA_DK, GLA_DV = 4, 48, 96
GLA_DKP, GLA_DVP = 64, 128
GLA_KWP, GLA_VWP = GLA_HEADS * GLA_DKP, GLA_HEADS * GLA_DVP
GLA_RANK = 16
GLA_TAU = 16.0
RW_HEADS, RW_DH = 6, 64
RW_W = RW_HEADS * RW_DH
FFN_HIDDEN = 2816

URET_W = 4 * RET_W
UGLA_W = 2 * GLA_KWP + 2 * GLA_VWP + 128
URW_W = 3 * RW_W + 128 + 128 + 128
NP_IN = URET_W + UGLA_W + URW_W
MIXP_W = RET_W + GLA_VWP + RW_W

CHUNK = 64
RET_CHUNK = 256
SCAN_BLOCK = 256
ROW_TILE = 256
VMEM_LIMIT = 56 * 1024 * 1024


def _bf(x):
    return x.astype(BF16)


def _dot(a, b):
    return jnp.dot(a, b, preferred_element_type=F32)


def _dot_nt(a, b):
    return lax.dot_general(a, b, (((1,), (1,)), ((), ())), preferred_element_type=F32)


def _dot_tn(a, b):
    return lax.dot_general(a, b, (((0,), (0,)), ((), ())), preferred_element_type=F32)


def _split2(x):
    hi = _bf(x)
    return hi, _bf(x - hi.astype(F32))


def _split_dot(x, m):
    hi, lo = _split2(x)
    return _dot(hi, m) + _dot(lo, m)


def _cumsum_dot(m, x):
    hi = _bf(x)
    r1 = x - hi.astype(F32)
    mid = _bf(r1)
    lo = _bf(r1 - mid.astype(F32))
    return _dot(m, hi) + (_dot(m, mid) + _dot(m, lo))


def _dot_3pass(a, b):
    ah, al = _split2(a)
    bh, bl = _split2(b)
    return _dot(ah, bh) + (_dot(ah, bl) + _dot(al, bh))


def _sigmoid(x):
    return 1.0 / (1.0 + jnp.exp(-x))


def _silu(x):
    return x * _sigmoid(x)


def _softplus(x):
    return jnp.maximum(x, 0.0) + jnp.log(1.0 + jnp.exp(-jnp.abs(x)))


def _rms(x, g, eps=1e-6):
    return x * lax.rsqrt(jnp.mean(x * x, axis=-1, keepdims=True) + eps) * g


def _order_masks(c, reverse):
    t = lax.broadcasted_iota(jnp.int32, (c, c), 0)
    s = lax.broadcasted_iota(jnp.int32, (c, c), 1)
    if reverse:
        return s >= t, s > t
    return s <= t, s < t


def _params(*sem):
    return pltpu.CompilerParams(dimension_semantics=sem, vmem_limit_bytes=VMEM_LIMIT)


def _const_spec(shape):
    nd = len(shape)
    return pl.BlockSpec(shape, lambda *_: (0,) * nd)


def _resident_spec(shape):
    nd = len(shape)
    return pl.BlockSpec(shape, lambda *_: (0,) * nd, pipeline_mode=pl.Buffered(1))


def _mod_kernel(c_ref, w_ref, b_ref, o_ref):
    c = c_ref[...]
    o_ref[...] = _dot_3pass(_silu(c), w_ref[...]) + b_ref[...]


def _mod_call(cc, w_mod, b_mod):
    depth, d, n = w_mod.shape
    tn = 1024
    return pl.pallas_call(
        _mod_kernel,
        grid=(depth, n // tn),
        in_specs=[
            pl.BlockSpec((8, d), lambda l, j: (0, 0)),
            pl.BlockSpec((None, d, tn), lambda l, j: (l, 0, j)),
            pl.BlockSpec((None, 1, tn), lambda l, j: (l, 0, j)),
        ],
        out_specs=pl.BlockSpec((None, 8, tn), lambda l, j: (l, 0, j)),
        out_shape=jax.ShapeDtypeStruct((depth, 8, n), F32),
        compiler_params=_params("arbitrary", "arbitrary"),
        name="mod",
    )(cc, w_mod, b_mod.reshape(depth, 1, n))


def _proj_kernel(x_ref, mod_ref, g_ref, w_ref, ur_ref, ug_ref, uw_ref):
    d = D_MODEL
    h = _rms(x_ref[...], g_ref[...])
    h = _bf(h * (1.0 + mod_ref[:, d:2 * d]) + mod_ref[:, 0:d])
    ur_ref[...] = _dot(h, w_ref[:, 0:URET_W])
    ug_ref[...] = _dot(h, w_ref[:, URET_W:URET_W + UGLA_W])
    uw_ref[...] = _dot(h, w_ref[:, URET_W + UGLA_W:NP_IN])


def _mod_spec():
    return pl.BlockSpec((None, None, 1, 6 * D_MODEL), lambda b, i: (b, jnp.minimum(i, 1), 0, 0))


def _proj_call(xa, modsel, g, w):
    bsz, ta, d = xa.shape
    nt = ta // ROW_TILE
    row = lambda w_: pl.BlockSpec((None, ROW_TILE, w_), lambda b, i: (b, i, 0))
    return pl.pallas_call(
        _proj_kernel,
        grid=(bsz, nt),
        in_specs=[row(d), _mod_spec(), _const_spec((1, d)), _const_spec((d, NP_IN))],
        out_specs=[row(URET_W), row(UGLA_W), row(URW_W)],
        out_shape=[jax.ShapeDtypeStruct((bsz, ta, w_), F32) for w_ in (URET_W, UGLA_W, URW_W)],
        compiler_params=_params("arbitrary", "arbitrary"),
        name="proj",
    )(xa, modsel, g, w)


def _chunk_maps(ta, chunk):
    ns = ta // chunk
    nc = CTX_LEN // chunk

    def fwd(s):
        return s

    def bwd(s):
        return jnp.where(s < nc, nc - 1 - s, ns + nc - 1 - s)

    return ns, fwd, bwd


def _chunk_specs(bsz, ta, chunk, width):
    _, fwd, bwd = _chunk_maps(ta, chunk)
    return tuple(pl.BlockSpec((bsz, chunk, width), lambda s, m=m: (0, m(s), 0)) for m in (fwd, bwd))


def _rope_swap(x):
    lane = lax.broadcasted_iota(jnp.int32, x.shape, 1)
    w = x.shape[1]
    return jnp.where(lane % 32 < 16, pltpu.roll(x, w - 16, 1), pltpu.roll(x, 16, 1))


def _ret_kernel(uf_ref, ub_ref, cf_ref, sf_ref, cb_ref, sb_ref, dm_ref, qd_ref, kd_ref, cd_ref,
                of_ref, ob_ref, st_ref):
    @pl.when(pl.program_id(0) == 0)
    def _():
        st_ref[...] = jnp.zeros_like(st_ref)

    bsz = uf_ref.shape[0]
    groups = []
    for d, (u_ref, cos_ref, sin_ref) in enumerate(((uf_ref, cf_ref, sf_ref), (ub_ref, cb_ref, sb_ref))):
        cos, sin = cos_ref[...], sin_ref[...]
        for b in range(bsz):
            q = u_ref[b, :, 0:RET_W]
            k = u_ref[b, :, RET_W:2 * RET_W]
            q = q * cos + _rope_swap(q) * sin
            k = (k * cos + _rope_swap(k) * sin) * (RET_DH ** -0.5)
            groups.append(dict(d=d, b=b, q=_bf(q), k=_bf(k), qd=_bf(q * qd_ref[d]), kd=_bf(k * kd_ref[d]),
                               v=_bf(u_ref[b, :, 2 * RET_W:3 * RET_W])))
    chains = [(p, h, slice(h * RET_DH, (h + 1) * RET_DH)) for p in groups for h in range(RET_HEADS)]
    sc = [_bf(_dot_nt(p["q"][:, sl], p["k"][:, sl]) * dm_ref[p["d"], h]) for p, h, sl in chains]
    s0 = [st_ref[p["d"], p["b"], h] for p, h, _ in chains]
    outs = [_dot(sci, p["v"][:, sl]) + _dot_nt(p["qd"][:, sl], _bf(s)) for sci, s, (p, _, sl) in zip(sc, s0, chains)]
    for s, (p, h, sl) in zip(s0, chains):
        st_ref[p["d"], p["b"], h] = s * cd_ref[p["d"], h] + _dot_tn(p["v"][:, sl], p["kd"][:, sl])
    for gi, p in enumerate(groups):
        o_ref = ob_ref if p["d"] else of_ref
        o_ref[p["b"]] = jnp.concatenate(outs[gi * RET_HEADS:(gi + 1) * RET_HEADS], axis=1)


def _ret_consts():
    c = RET_CHUNK
    lg = np.log1p(-np.exp2(-5.0 - np.arange(RET_HEADS, dtype=np.float64)))
    pos = np.arange(c, dtype=np.float64)
    rel = pos[:, None] - pos[None, :]
    dm = np.zeros((2, RET_HEADS, c, c))
    qd = np.zeros((2, c, RET_W))
    kd = np.zeros((2, c, RET_W))
    cd = np.zeros((2, RET_HEADS, 1, RET_DH))
    for d in range(2):
        lgd = lg if d == 0 else lg[::-1]
        p = pos if d == 0 else c - 1.0 - pos
        for h in range(RET_HEADS):
            intra = np.where(rel >= 0, np.exp(lgd[h] * np.maximum(rel, 0.0)), 0.0)
            dm[d, h] = intra if d == 0 else intra.T
            qd[d, :, h * RET_DH:(h + 1) * RET_DH] = np.exp(lgd[h] * (p + 1.0))[:, None]
            kd[d, :, h * RET_DH:(h + 1) * RET_DH] = np.exp(lgd[h] * (c - 1.0 - p))[:, None]
            cd[d, h] = np.exp(lgd[h] * c)
    return tuple(jnp.asarray(a, F32) for a in (dm, qd, kd, cd))


def _ret_call(u_ret, cos, sin):
    bsz, ta, _ = u_ret.shape
    c = RET_CHUNK
    ns, fwd, bwd = _chunk_maps(ta, c)
    uf, ub = _chunk_specs(bsz, ta, c, URET_W)
    tf, tb = (pl.BlockSpec((c, RET_W), lambda s, m=m: (m(s), 0)) for m in (fwd, bwd))
    of, ob = _chunk_specs(bsz, ta, c, RET_W)
    dm, qd, kd, cd = _ret_consts()
    return pl.pallas_call(
        _ret_kernel,
        grid=(ns,),
        in_specs=[uf, ub, tf, tf, tb, tb, _const_spec(dm.shape), _const_spec(qd.shape),
                  _const_spec(kd.shape), _const_spec(cd.shape)],
        out_specs=[of, ob],
        out_shape=[jax.ShapeDtypeStruct((bsz, ta, RET_W), F32)] * 2,
        scratch_shapes=[pltpu.VMEM((2, bsz, RET_HEADS, RET_DH, RET_DH), F32)],
        compiler_params=_params("arbitrary"),
        name="ret_scan",
    )(u_ret, u_ret, cos, sin, cos, sin, dm, qd, kd, cd)


def _gla_kernel(uf_ref, ub_ref, wa_ref, ba_ref, of_ref, ob_ref, st_ref):
    @pl.when(pl.program_id(0) == 0)
    def _():
        st_ref[...] = jnp.zeros_like(st_ref)

    bsz = uf_ref.shape[0]
    c = CHUNK
    nsub = uf_ref.shape[1] // c
    half = c // 2
    groups = []
    for d, u_ref in enumerate((uf_ref, ub_ref)):
        rev = d == 1
        incl, _ = _order_masks(c, rev)
        incl_b = incl.astype(BF16)
        pos = lambda p: c - 1 - p if rev else p
        row = lax.broadcasted_iota(jnp.int32, (c, 1), 0)
        first = (row >= half) if rev else (row < half)
        ti = lax.broadcasted_iota(jnp.int32, (c, c), 0)
        si = lax.broadcasted_iota(jnp.int32, (c, c), 1)
        if rev:
            straddle = jnp.logical_and(ti < half, si >= half)
        else:
            straddle = jnp.logical_and(ti >= half, si < half)
        for j in range(nsub):
            rows = pl.ds((nsub - 1 - j if rev else j) * c, c)
            for b in range(bsz):
                lr = _bf(u_ref[b, rows, 2 * GLA_KWP + 2 * GLA_VWP:UGLA_W])
                z = _dot(lr, wa_ref[d]) + ba_ref[d]
                la = -_softplus(-z) * (1.0 / GLA_TAU)
                g = _cumsum_dot(incl_b, la)
                g_end = jnp.sum(la, axis=0, keepdims=True)
                g_row = lambda p: g[pos(p):pos(p) + 1, :]
                piv = jnp.where(first, g_row(half // 2 - 1), g_row(half + half // 2 - 1))
                g_cut = g_row(half - 1)
                q = u_ref[b, rows, 0:GLA_KWP] * (GLA_DK ** -0.5)
                k = u_ref[b, rows, GLA_KWP:2 * GLA_KWP]
                groups.append(dict(
                    d=d, b=b, j=j, rows=rows, incl=incl, straddle=straddle, e_end=jnp.exp(g_end),
                    qt=_bf(q * jnp.exp(g - piv)), kt=_bf(k * jnp.exp(piv - g)),
                    qo=_bf(q * jnp.exp(jnp.minimum(g - g_cut, 0.0))),
                    ko=_bf(k * jnp.exp(jnp.minimum(g_cut - g, 0.0))),
                    qs=_bf(q * jnp.exp(g)), ks=_bf(k * jnp.exp(g_end - g)),
                    v=_bf(u_ref[b, rows, 2 * GLA_KWP:2 * GLA_KWP + GLA_VWP])))
    chains = [(p, h, slice(h * GLA_DKP, (h + 1) * GLA_DKP), slice(h * GLA_DVP, (h + 1) * GLA_DVP))
              for p in groups for h in range(GLA_HEADS)]
    same = [_dot_nt(p["qt"][:, sk], p["kt"][:, sk]) for p, _, sk, _ in chains]
    cross = [_dot_nt(p["qo"][:, sk], p["ko"][:, sk]) for p, _, sk, _ in chains]
    sc = [_bf(jnp.where(p["incl"], jnp.where(p["straddle"], x, s), 0.0))
          for s, x, (p, _, _, _) in zip(same, cross, chains)]
    intra = [_dot(sci, p["v"][:, sv]) for sci, (p, _, _, sv) in zip(sc, chains)]
    kv = [_dot_tn(p["v"][:, sv], p["ks"][:, sk]) for p, _, sk, sv in chains]
    state = {(d, b, h): st_ref[d, b, h] for d in range(2) for b in range(bsz) for h in range(GLA_HEADS)}
    outs = {}
    for j in range(nsub):
        step = [(i, ch) for i, ch in enumerate(chains) if ch[0]["j"] == j]
        s0 = [state[(p["d"], p["b"], h)] for _, (p, h, _, _) in step]
        for s, (i, (p, h, sk, _)) in zip(s0, step):
            outs[(p["d"], p["b"], j, h)] = intra[i] + _dot_nt(p["qs"][:, sk], _bf(s))
            state[(p["d"], p["b"], h)] = s * p["e_end"][:, sk] + kv[i]
    for key, s in state.items():
        st_ref[key] = s
    for p in groups:
        o_ref = ob_ref if p["d"] else of_ref
        o_ref[p["b"], p["rows"], :] = jnp.concatenate(
            [outs[(p["d"], p["b"], p["j"], h)] for h in range(GLA_HEADS)], axis=1)


def _gla_call(u_gla, wa, ba):
    bsz, ta, _ = u_gla.shape
    ns, _, _ = _chunk_maps(ta, SCAN_BLOCK)
    uf, ub = _chunk_specs(bsz, ta, SCAN_BLOCK, UGLA_W)
    of, ob = _chunk_specs(bsz, ta, SCAN_BLOCK, GLA_VWP)
    return pl.pallas_call(
        _gla_kernel,
        grid=(ns,),
        in_specs=[uf, ub, _const_spec(wa.shape), _const_spec(ba.shape)],
        out_specs=[of, ob],
        out_shape=[jax.ShapeDtypeStruct((bsz, ta, GLA_VWP), F32)] * 2,
        scratch_shapes=[pltpu.VMEM((2, bsz, GLA_HEADS, GLA_DVP, GLA_DKP), F32)],
        compiler_params=_params("arbitrary"),
        name="gla_scan",
    )(u_gla, u_gla, wa, ba)


RW_S1_W = 7 * RW_W
RW_S2_W = 2 * RW_W


def _rw_prep_kernel(u_ref, up_ref, un_ref, cw_ref, a0_ref, a2_ref, g2_ref, w0_ref, w2_ref,
                    kk_ref, ka_ref, rk_ref, bsum_ref, s1_ref, s2_ref, *, n_tiles):
    i = pl.program_id(1)
    tm = ROW_TILE
    u = u_ref[...]
    rows = lax.broadcasted_iota(jnp.int32, (tm, 1), 0)
    prev_row = jnp.where(i >= 2, up_ref[7:8, :], 0.0)
    next_row = jnp.where(jnp.logical_and(i >= 1, i < n_tiles - 1), un_ref[0:1, :], 0.0)
    u_prev = jnp.where(rows == 0, prev_row, pltpu.roll(u, 1, 0))
    u_next = jnp.where(rows == tm - 1, next_row, pltpu.roll(u, tm - 1, 0))
    uc = u_prev * cw_ref[0:1, :] + u * cw_ref[1:2, :] + u_next * cw_ref[2:3, :]

    w = RW_W
    r, k, v = uc[:, 0:w], uc[:, w:2 * w], uc[:, 2 * w:3 * w]
    lr_w = uc[:, 3 * w:3 * w + 128]
    lr_a = uc[:, 3 * w + 128:3 * w + 256]
    lr_g = uc[:, 3 * w + 256:3 * w + 384]
    a = _sigmoid(a0_ref[...] + _dot(_bf(lr_a), a2_ref[...]))
    gate = _dot(_bf(_sigmoid(lr_g)), g2_ref[...])
    kk = k * kk_ref[...]
    bsum = bsum_ref[...]
    kk = kk * lax.rsqrt(jnp.maximum(_split_dot(kk * kk, bsum), 1e-24))
    k2 = k * (1.0 + (a - 1.0) * ka_ref[...])
    ld = -float(np.exp(-0.5)) * _sigmoid(w0_ref[...] + _dot(_bf(jnp.tanh(lr_w)), w2_ref[...]))
    bonus = _split_dot(r * k2 * rk_ref[...], bsum) * v
    s1_ref[:, 0:w] = r
    s1_ref[:, w:2 * w] = k2
    s1_ref[:, 2 * w:3 * w] = v
    s1_ref[:, 3 * w:4 * w] = -kk
    s1_ref[:, 4 * w:5 * w] = kk * a
    s1_ref[:, 5 * w:7 * w] = ld
    s2_ref[:, 0:w] = gate
    s2_ref[:, w:2 * w] = bonus


def _rw_prep_call(u_rw, p):
    bsz, ta, _ = u_rw.shape
    nt = ta // ROW_TILE
    per = ROW_TILE // 8
    nb8 = ta // 8
    row = lambda w_: pl.BlockSpec((None, ROW_TILE, w_), lambda b, i: (b, i, 0))
    prev = pl.BlockSpec((None, 8, URW_W), lambda b, i: (b, jnp.maximum(i * per - 1, 0), 0))
    nxt = pl.BlockSpec((None, 8, URW_W), lambda b, i: (b, jnp.minimum((i + 1) * per, nb8 - 1), 0))
    consts = [p["conv"], p["a0"], p["a2"], p["g2"], p["w0"], p["w2"], p["k_k"], p["k_a"], p["r_k"], p["bsum"]]
    return pl.pallas_call(
        functools.partial(_rw_prep_kernel, n_tiles=nt),
        grid=(bsz, nt),
        in_specs=[row(URW_W), prev, nxt] + [_const_spec(a.shape) for a in consts],
        out_specs=[row(RW_S1_W), row(RW_S2_W)],
        out_shape=[jax.ShapeDtypeStruct((bsz, ta, RW_S1_W), F32),
                   jax.ShapeDtypeStruct((bsz, ta, RW_S2_W), F32)],
        compiler_params=_params("arbitrary", "arbitrary"),
        name="rw_prep",
    )(u_rw, u_rw, u_rw, *consts)


def _rw_kernel(sf_ref, sb_ref, of_ref, ob_ref, st_ref):
    @pl.when(pl.program_id(0) == 0)
    def _():
        st_ref[...] = jnp.zeros_like(st_ref)

    bsz = sf_ref.shape[0]
    c, w, dh = CHUNK, RW_W, RW_DH
    groups = []
    for d, s_ref in enumerate((sf_ref, sb_ref)):
        incl, _ = _order_masks(c, d == 1)
        incl_b = incl.astype(BF16)
        tt = lax.broadcasted_iota(jnp.int32, (2 * c, 2 * c), 0)
        ss = lax.broadcasted_iota(jnp.int32, (2 * c, 2 * c), 1) & (c - 1)
        t_in = tt & (c - 1)
        before = (ss > t_in) if d == 1 else (ss < t_in)
        pair_mask = jnp.logical_or(before, jnp.logical_and(tt >= c, ss == t_in))
        t64 = lax.broadcasted_iota(jnp.int32, (c, c), 0)
        s64 = lax.broadcasted_iota(jnp.int32, (c, c), 1)
        eye = (t64 == s64).astype(F32)
        late, early = (s64, t64) if d == 1 else (t64, s64)
        level_masks = [
            jnp.logical_and((t64 >> (j + 1)) == (s64 >> (j + 1)),
                            jnp.logical_and(((late >> j) & 1) == 1, ((early >> j) & 1) == 0))
            for j in range(6)]
        for b in range(bsz):
            la = s_ref[b, :, (5 + d) * w:(6 + d) * w]
            g = _cumsum_dot(incl_b, la)
            g_end = jnp.sum(la, axis=0, keepdims=True)
            e_g = jnp.exp(g)
            e_gp = jnp.exp(g - la)
            e_inv = jnp.exp(-g_end)
            e_k = jnp.exp(g_end - g)
            e_end = jnp.exp(g_end)
            r0 = s_ref[b, :, 0:w] * e_g
            a0 = s_ref[b, :, 3 * w:4 * w] * e_gp
            rt, at = r0 * e_inv, a0 * e_inv
            kt = s_ref[b, :, w:2 * w] * e_k
            bt = s_ref[b, :, 4 * w:5 * w] * e_k
            groups.append(dict(
                d=d, b=b, pair_mask=pair_mask, level_masks=level_masks, eye=eye, e_end=e_end, a0=a0, r0=_bf(r0),
                lhs=_bf(jnp.concatenate([at, rt], axis=0)),
                rhs=_bf(jnp.concatenate([bt, kt], axis=0)),
                v=_bf(s_ref[b, :, 2 * w:3 * w])))

    chains = [(q, h, slice(h * dh, (h + 1) * dh)) for q in groups for h in range(RW_HEADS)]
    pair = [jnp.where(q["pair_mask"], _dot_nt(q["lhs"][:, sl], q["rhs"][:, sl]), 0.0)
            for q, _, sl in chains]
    n = [p[0:c, 0:c] for p in pair]
    x = [jnp.concatenate([_dot(_bf(p[0:c, c:2 * c]), q["v"][:, sl]), q["a0"][:, sl]], axis=1)
         for p, (q, _, sl) in zip(pair, chains)]
    tm = [jnp.where(q["level_masks"][0], ni, q["eye"]) for ni, (q, _, _) in zip(n, chains)]
    for lvl in range(1, 6):
        tb = [_bf(t) for t in tm]
        left = [_bf(_dot(tbi, _bf(jnp.where(q["level_masks"][lvl], ni, 0.0))))
                for tbi, ni, (q, _, _) in zip(tb, n, chains)]
        tm = [t + _dot(li, tbi) for t, li, tbi in zip(tm, left, tb)]
    x = [_dot(_bf(t), _bf(xi)) for t, xi in zip(tm, x)]
    s0 = [st_ref[q["d"], q["b"], h] for q, h, _ in chains]
    s0b = [_bf(s) for s in s0]
    ms = [_dot_nt(jnp.concatenate([_bf(xi[:, dh:2 * dh]), q["r0"][:, sl]], axis=0), sb)
          for xi, sb, (q, _, sl) in zip(x, s0b, chains)]
    uv = [jnp.concatenate([_bf(xi[:, 0:dh] + m[0:c]), q["v"][:, sl]], axis=0)
          for xi, m, (q, _, sl) in zip(x, ms, chains)]
    outs = [m[c:2 * c] + _dot(_bf(p[c:2 * c, :]), w) for p, m, w in zip(pair, ms, uv)]
    for s, w, (q, h, sl) in zip(s0, uv, chains):
        st_ref[q["d"], q["b"], h] = s * q["e_end"][:, sl] + _dot_tn(w, q["rhs"][:, sl])
    for gi, q in enumerate(groups):
        o_ref = ob_ref if q["d"] else of_ref
        o_ref[q["b"]] = jnp.concatenate(outs[gi * RW_HEADS:(gi + 1) * RW_HEADS], axis=1)


def _rw_call(s1):
    bsz, ta, _ = s1.shape
    ns, _, _ = _chunk_maps(ta, CHUNK)
    sf, sb = _chunk_specs(bsz, ta, CHUNK, RW_S1_W)
    of, ob = _chunk_specs(bsz, ta, CHUNK, RW_W)
    return pl.pallas_call(
        _rw_kernel,
        grid=(ns,),
        in_specs=[sf, sb],
        out_specs=[of, ob],
        out_shape=[jax.ShapeDtypeStruct((bsz, ta, RW_W), F32)] * 2,
        scratch_shapes=[pltpu.VMEM((2, bsz, RW_HEADS, RW_DH, RW_DH), F32)],
        compiler_params=_params("arbitrary"),
        name="rw_scan",
    )(s1, s1)


def _post_kernel(x_ref, mod_ref, raf_ref, rab_ref, rg_ref, gaf_ref, gab_ref, gg_ref,
                 waf_ref, wab_ref, s2_ref, rn_ref, gn_ref, lw_ref, lb_ref,
                 avg_r_ref, sum_g_ref, avg_w_ref, wo_ref, pn_ref,
                 pre_ref, wg_ref, wu_ref, wd_ref, post_ref, o_ref):
    d = D_MODEL
    o = raf_ref[...] + rab_ref[...]
    avg = avg_r_ref[...]
    mu = _split_dot(o, avg)
    oc = o - mu
    var = _split_dot(oc * oc, avg)
    ya = _silu(rg_ref[...]) * (oc * lax.rsqrt(var + 1e-5) * rn_ref[...])
    o = gaf_ref[...] + gab_ref[...]
    ms = _split_dot(o * o, sum_g_ref[...]) * (1.0 / GLA_DV)
    yb = _silu(gg_ref[...]) * (o * lax.rsqrt(ms + 1e-5) * gn_ref[...])
    o = waf_ref[...] + wab_ref[...]
    avg = avg_w_ref[...]
    mu = _split_dot(o, avg)
    oc = o - mu
    var = _split_dot(oc * oc, avg)
    yc = (oc * lax.rsqrt(var + 64e-5) * lw_ref[...] + lb_ref[...] + s2_ref[:, RW_W:2 * RW_W]) * s2_ref[:, 0:RW_W]
    y = (_dot(_bf(ya), wo_ref[0:RET_W, :]) + _dot(_bf(yb), wo_ref[RET_W:RET_W + GLA_VWP, :])
         + _dot(_bf(yc), wo_ref[RET_W + GLA_VWP:MIXP_W, :]))
    x = x_ref[...] + mod_ref[:, 2 * d:3 * d] * _rms(y, pn_ref[...])
    h = _rms(x, pre_ref[...])
    h = _bf(h * (1.0 + mod_ref[:, 4 * d:5 * d]) + mod_ref[:, 3 * d:4 * d])
    act = _bf(_silu(_dot(h, wg_ref[...])) * _dot(h, wu_ref[...]))
    f = _dot(act, wd_ref[...])
    o_ref[...] = x + mod_ref[:, 5 * d:6 * d] * _rms(f, post_ref[...])


def _post_call(xa, modsel, ra, u_ret, ga, u_gla, wa, s2, p, tile_off):
    bsz, ta, d = xa.shape
    nt = ta // ROW_TILE - tile_off
    row = lambda w_, cb=0: pl.BlockSpec((None, ROW_TILE, w_), lambda b, i: (b, i + tile_off, cb))
    mod = pl.BlockSpec((None, None, 1, 6 * d), lambda b, i: (b, jnp.minimum(i + tile_off, 1), 0, 0))
    consts = [p["ret_norm"], p["gla_norm"], p["ln_w"], p["ln_b"], p["avg_r"], p["sum_g"], p["avg_w"],
              p["w_out"], p["norm_mix_post"],
              p["norm_ffn_pre"], p["w_gate"], p["w_up"], p["w_down"], p["norm_ffn_post"]]
    return pl.pallas_call(
        _post_kernel,
        grid=(bsz, nt),
        in_specs=[row(d), mod, row(RET_W), row(RET_W), row(RET_W, 3),
                  row(GLA_VWP), row(GLA_VWP), row(GLA_VWP, 2),
                  row(RW_W), row(RW_W), row(RW_S2_W)] + [_resident_spec(a.shape) for a in consts],
        out_specs=pl.BlockSpec((None, ROW_TILE, d), lambda b, i: (b, i, 0)),
        out_shape=jax.ShapeDtypeStruct((bsz, nt * ROW_TILE, d), F32),
        compiler_params=_params("arbitrary", "arbitrary"),
        name="post",
    )(xa, modsel, ra[0], ra[1], u_ret, ga[0], ga[1], u_gla, wa[0], wa[1], s2, *consts)


def _pad_last(a, n):
    return jnp.pad(a, [(0, 0)] * (a.ndim - 1) + [(0, n - a.shape[-1])])


def _pad_heads(a, heads, dh, dhp):
    lead = a.shape[:-1]
    a = a.reshape(lead + (heads, dh))
    return _pad_last(a, dhp).reshape(lead + (heads * dhp,))


def _arrange_rw(a):
    w = RW_W
    return jnp.concatenate([a[..., 0:3 * w + 128], _pad_last(a[..., 3 * w + 128:3 * w + 192], 128),
                            a[..., 3 * w + 192:3 * w + 320]], axis=-1)


def _arrange_in(a):
    g0 = URET_W
    kw, vw = GLA_HEADS * GLA_DK, GLA_HEADS * GLA_DV
    r0 = g0 + 2 * kw + 2 * vw + 2 * GLA_RANK
    return jnp.concatenate([
        a[..., 0:g0],
        _pad_heads(a[..., g0:g0 + kw], GLA_HEADS, GLA_DK, GLA_DKP),
        _pad_heads(a[..., g0 + kw:g0 + 2 * kw], GLA_HEADS, GLA_DK, GLA_DKP),
        _pad_heads(a[..., g0 + 2 * kw:g0 + 2 * kw + vw], GLA_HEADS, GLA_DV, GLA_DVP),
        _pad_heads(a[..., g0 + 2 * kw + vw:g0 + 2 * kw + 2 * vw], GLA_HEADS, GLA_DV, GLA_DVP),
        _pad_last(a[..., g0 + 2 * kw + 2 * vw:r0], 128),
        _arrange_rw(a[..., r0:]),
    ], axis=-1)


def _block_diag_ones(n, blk, val=1.0):
    i = np.arange(n)
    return jnp.asarray(np.where(i[:, None] // blk == i[None, :] // blk, val, 0.0), BF16)


def _rope_tables(t_lat):
    rows = t_lat // GRID_W
    row = jnp.repeat(jnp.arange(rows, dtype=F32), GRID_W)
    col = jnp.tile(jnp.arange(GRID_W, dtype=F32), rows)
    nf = RET_DH // 4
    inv = ROPE_THETA ** (-jnp.arange(nf, dtype=F32) / nf)
    ar, ac = row[:, None] * inv, col[:, None] * inv
    cos = jnp.concatenate([jnp.cos(ar), jnp.cos(ar), jnp.cos(ac), jnp.cos(ac)], axis=1)
    sin = jnp.concatenate([-jnp.sin(ar), jnp.sin(ar), -jnp.sin(ac), jnp.sin(ac)], axis=1)
    cos = jnp.concatenate([jnp.ones((CTX_LEN, RET_DH), F32), cos], axis=0)
    sin = jnp.concatenate([jnp.zeros((CTX_LEN, RET_DH), F32), sin], axis=0)
    return jnp.tile(cos, (1, RET_HEADS)), jnp.tile(sin, (1, RET_HEADS))


def kernel(x, c, ctx, c_ctx, w_mod, b_mod, norm_mix_pre, norm_mix_post, norm_ffn_pre, norm_ffn_post,
           w_in, ret_norm, gla_wa2_f, gla_ba_f, gla_wa2_b, gla_ba_b, gla_norm, rw_conv,
           rw_w0_f, rw_w2_f, rw_w0_b, rw_w2_b, rw_a0, rw_a2, rw_g2, rw_k_k, rw_k_a, rw_r_k,
           rw_ln_w, rw_ln_b, w_out, w_ffn_gate, w_ffn_up, w_ffn_down):
    bsz, t_lat, d = x.shape
    depth = w_mod.shape[0]
    assert d == D_MODEL and ctx.shape[1] == CTX_LEN and t_lat % ROW_TILE == 0 and bsz <= 7

    cc = jnp.zeros((8, d), F32).at[0:bsz].set(c).at[bsz].set(c_ctx)
    mods = _mod_call(cc, w_mod, b_mod)
    cos, sin = _rope_tables(t_lat)
    avg_r = _block_diag_ones(RET_W, RET_DH, 1.0 / RET_DH)
    avg_w = _block_diag_ones(RW_W, RW_DH, 1.0 / RW_DH)
    sum_w = _block_diag_ones(RW_W, RW_DH)
    sum_g = _block_diag_ones(GLA_VWP, GLA_DVP)

    xa = jnp.concatenate([ctx, x], axis=1)
    for l in range(depth):
        last = l == depth - 1
        modsel = jnp.stack([jnp.broadcast_to(mods[l, bsz], (bsz, 6 * d)), mods[l, 0:bsz]], axis=1)[:, :, None, :]
        row = lambda a: a[l].reshape(1, -1)

        u_ret, u_gla, u_rw = _proj_call(xa, modsel, row(norm_mix_pre), _bf(_arrange_in(w_in[l])))

        ra = _ret_call(u_ret, cos, sin)

        wa = jnp.stack([_pad_last(jnp.pad(_pad_heads(gla_wa2_f[l], GLA_HEADS, GLA_DK, GLA_DKP),
                                          [(0, 128 - GLA_RANK), (0, 0)]), GLA_KWP),
                        _pad_last(jnp.pad(_pad_heads(gla_wa2_b[l], GLA_HEADS, GLA_DK, GLA_DKP),
                                          [(GLA_RANK, 128 - 2 * GLA_RANK), (0, 0)]), GLA_KWP)])
        ba = jnp.stack([_pad_heads(gla_ba_f[l], GLA_HEADS, GLA_DK, GLA_DKP),
                        _pad_heads(gla_ba_b[l], GLA_HEADS, GLA_DK, GLA_DKP)])[:, None, :]
        ga = _gla_call(u_gla, _bf(wa), ba)

        w2 = jnp.zeros((128, 2 * RW_W), F32).at[0:64, 0:RW_W].set(rw_w2_f[l]).at[64:128, RW_W:].set(rw_w2_b[l])
        prep = {
            "conv": _arrange_rw(rw_conv[l]),
            "a0": row(rw_a0), "a2": _bf(jnp.pad(rw_a2[l], [(0, 64), (0, 0)])), "g2": _bf(rw_g2[l]),
            "w0": jnp.concatenate([rw_w0_f[l], rw_w0_b[l]]).reshape(1, -1), "w2": _bf(w2),
            "k_k": row(rw_k_k), "k_a": row(rw_k_a), "r_k": row(rw_r_k), "bsum": sum_w,
        }
        s1, s2 = _rw_prep_call(u_rw, prep)
        wo = _rw_call(s1)

        wout = w_out[l]
        outp = {
            "ret_norm": row(ret_norm),
            "gla_norm": jnp.tile(_pad_last(gla_norm[l], GLA_DVP), GLA_HEADS).reshape(1, -1),
            "ln_w": row(rw_ln_w), "ln_b": row(rw_ln_b),
            "avg_r": avg_r, "sum_g": sum_g, "avg_w": avg_w,
            "w_out": _bf(jnp.concatenate([
                wout[0:RET_W],
                jnp.pad(wout[RET_W:RET_W + GLA_HEADS * GLA_DV].reshape(GLA_HEADS, GLA_DV, d),
                        [(0, 0), (0, GLA_DVP - GLA_DV), (0, 0)]).reshape(GLA_VWP, d),
                wout[RET_W + GLA_HEADS * GLA_DV:]], axis=0)),
            "norm_mix_post": row(norm_mix_post),
            "norm_ffn_pre": row(norm_ffn_pre), "w_gate": _bf(w_ffn_gate[l]), "w_up": _bf(w_ffn_up[l]),
            "w_down": _bf(w_ffn_down[l]), "norm_ffn_post": row(norm_ffn_post),
        }
        tile_off = CTX_LEN // ROW_TILE if last else 0
        xa = _post_call(xa, modsel, ra, u_ret, ga, u_gla, wo, s2, outp, tile_off)
    return xa
```

```python
import functools

import numpy as np
import jax
import jax.numpy as jnp
from jax import lax
from jax.experimental import pallas as pl
from jax.experimental.pallas import tpu as pltpu

F32 = jnp.float32
BF16 = jnp.bfloat16

D_MODEL = 1024
CTX_LEN = 256
GRID_W = 64
ROPE_THETA = 10000.0

RET_HEADS, RET_DH = 4, 64
RET_W = RET_HEADS * RET_DH
GLA_HEADS, GLA_DK, GLA_DV = 4, 48, 96
GLA_DKP, GLA_DVP = 64, 128
GLA_KWP, GLA_VWP = GLA_HEADS * GLA_DKP, GLA_HEADS * GLA_DVP
GLA_RANK = 16
GLA_TAU = 16.0
RW_HEADS, RW_DH = 6, 64
RW_W = RW_HEADS * RW_DH
FFN_HIDDEN = 2816

URET_W = 4 * RET_W
UGLA_W = 2 * GLA_KWP + 2 * GLA_VWP + 128
URW_W = 3 * RW_W + 128 + 128 + 128
NP_IN = URET_W + UGLA_W + URW_W
MIXP_W = RET_W + GLA_VWP + RW_W

CHUNK = 64
RET_CHUNK = 256
SCAN_BLOCK = 256
ROW_TILE = 256
VMEM_LIMIT = 56 * 1024 * 1024


def _bf(x):
    return x.astype(BF16)


def _dot(a, b):
    return jnp.dot(a, b, preferred_element_type=F32)


def _dot_nt(a, b):
    return lax.dot_general(a, b, (((1,), (1,)), ((), ())), preferred_element_type=F32)


def _dot_tn(a, b):
    return lax.dot_general(a, b, (((0,), (0,)), ((), ())), preferred_element_type=F32)


def _split2(x):
    hi = _bf(x)
    return hi, _bf(x - hi.astype(F32))


def _group_dot(x, m):
    return _dot(_bf(x), m)


def _cumsum_dot(m, x):
    hi = _bf(x)
    r1 = x - hi.astype(F32)
    mid = _bf(r1)
    lo = _bf(r1 - mid.astype(F32))
    return _dot(m, hi) + (_dot(m, mid) + _dot(m, lo))


def _dot_3pass(a, b):
    ah, al = _split2(a)
    bh, bl = _split2(b)
    return _dot(ah, bh) + (_dot(ah, bl) + _dot(al, bh))


def _sigmoid(x):
    return 1.0 / (1.0 + jnp.exp(-x))


def _silu(x):
    return x * _sigmoid(x)


def _softplus(x):
    return jnp.maximum(x, 0.0) + jnp.log(1.0 + jnp.exp(-jnp.abs(x)))


def _rms(x, g, eps=1e-6):
    return x * lax.rsqrt(jnp.mean(x * x, axis=-1, keepdims=True) + eps) * g


def _order_masks(c, reverse):
    t = lax.broadcasted_iota(jnp.int32, (c, c), 0)
    s = lax.broadcasted_iota(jnp.int32, (c, c), 1)
    if reverse:
        return s >= t, s > t
    return s <= t, s < t


def _params(*sem):
    return pltpu.CompilerParams(dimension_semantics=sem, vmem_limit_bytes=VMEM_LIMIT)


def _const_spec(shape):
    nd = len(shape)
    return pl.BlockSpec(shape, lambda *_: (0,) * nd)


def _resident_spec(shape):
    nd = len(shape)
    return pl.BlockSpec(shape, lambda *_: (0,) * nd, pipeline_mode=pl.Buffered(1))


def _mod_kernel(c_ref, w_ref, b_ref, o_ref):
    c = c_ref[...]
    o_ref[...] = _dot_3pass(_silu(c), w_ref[...]) + b_ref[...]


def _mod_call(cc, w_mod, b_mod):
    depth, d, n = w_mod.shape
    tn = 1024
    return pl.pallas_call(
        _mod_kernel,
        grid=(depth, n // tn),
        in_specs=[
            pl.BlockSpec((8, d), lambda l, j: (0, 0)),
            pl.BlockSpec((None, d, tn), lambda l, j: (l, 0, j)),
            pl.BlockSpec((None, 1, tn), lambda l, j: (l, 0, j)),
        ],
        out_specs=pl.BlockSpec((None, 8, tn), lambda l, j: (l, 0, j)),
        out_shape=jax.ShapeDtypeStruct((depth, 8, n), F32),
        compiler_params=_params("arbitrary", "arbitrary"),
        name="mod",
    )(cc, w_mod, b_mod.reshape(depth, 1, n))


RW_S1_W = 7 * RW_W
RW_S2_W = 2 * RW_W
HALO = 16


def _tile_rows(i, c_ref, x_ref):
    return x_ref[...] if c_ref is None else jnp.where(i == 0, c_ref[...], x_ref[...])


def _proj_kernel(c_ref, x_ref, xp_ref, xn_ref, mod_ref, g_ref, w_ref, cw_ref, a0_ref, a2_ref, g2_ref, w0_ref,
                 w2_ref, kk_ref, ka_ref, rk_ref, bsum_ref, ur_ref, ug_ref, s1_ref, s2_ref, *, n_tiles):
    i = pl.program_id(1)
    d, tm, te = D_MODEL, ROW_TILE, ROW_TILE + 2 * HALO
    xe = jnp.concatenate([xp_ref[...], _tile_rows(i, c_ref, x_ref), xn_ref[...]], axis=0)
    h = _rms(xe, g_ref[...])
    h = h * (1.0 + mod_ref[:, d:2 * d]) + mod_ref[:, 0:d]
    rows = lax.broadcasted_iota(jnp.int32, (te, 1), 0)
    keep = jnp.logical_and(jnp.logical_or(rows >= HALO, i >= 2),
                           jnp.logical_or(rows < HALO + tm, jnp.logical_and(i >= 1, i < n_tiles - 1)))
    h = _bf(jnp.where(keep, h, 0.0))
    ue = _dot(h, w_ref[:, URET_W + UGLA_W:NP_IN])
    hm = h[HALO:HALO + tm]
    ur_ref[...] = _dot(hm, w_ref[:, 0:URET_W])
    ug_ref[...] = _dot(hm, w_ref[:, URET_W:URET_W + UGLA_W])

    mid = lambda a: a[HALO:HALO + tm]
    uc = (mid(pltpu.roll(ue, 1, 0)) * cw_ref[0:1, :] + mid(ue) * cw_ref[1:2, :]
          + mid(pltpu.roll(ue, te - 1, 0)) * cw_ref[2:3, :])
    w = RW_W
    r, k, v = uc[:, 0:w], uc[:, w:2 * w], uc[:, 2 * w:3 * w]
    lr_w = uc[:, 3 * w:3 * w + 128]
    lr_a = uc[:, 3 * w + 128:3 * w + 256]
    lr_g = uc[:, 3 * w + 256:3 * w + 384]
    a = _sigmoid(a0_ref[...] + _dot(_bf(lr_a), a2_ref[...]))
    gate = _dot(_bf(_sigmoid(lr_g)), g2_ref[...])
    kk = k * kk_ref[...]
    bsum = bsum_ref[...]
    kk = kk * lax.rsqrt(jnp.maximum(_group_dot(kk * kk, bsum), 1e-24))
    k2 = k * (1.0 + (a - 1.0) * ka_ref[...])
    ld = -float(np.exp(-0.5)) * _sigmoid(w0_ref[...] + _dot(_bf(jnp.tanh(lr_w)), w2_ref[...]))
    bonus = _group_dot(r * k2 * rk_ref[...], bsum) * v
    s1_ref[:, 0:w] = r
    s1_ref[:, w:2 * w] = k2
    s1_ref[:, 2 * w:3 * w] = v
    s1_ref[:, 3 * w:4 * w] = -kk
    s1_ref[:, 4 * w:5 * w] = kk * a
    s1_ref[:, 5 * w:7 * w] = ld
    s2_ref[:, 0:w] = gate
    s2_ref[:, w:2 * w] = bonus


def _stream_inputs(ctx, x):
    d = x.shape[2]
    off = 0 if ctx is None else CTX_LEN // ROW_TILE

    def specs(tile):
        x_spec = pl.BlockSpec((None, ROW_TILE, d), lambda b, i: (b, jnp.maximum(tile(i) - off, 0), 0))
        if ctx is None:
            return [x_spec]
        return [pl.BlockSpec((None, ROW_TILE, d), lambda b, i: (b, 0, 0)), x_spec]

    return ([x] if ctx is None else [ctx, x]), specs, off


def _proj_call(ctx, x, modsel, g, w, p):
    bsz, tx, d = x.shape
    arrays, tile_specs, off = _stream_inputs(ctx, x)
    nt = tx // ROW_TILE + off
    per = ROW_TILE // HALO
    nh = tx // HALO
    row = lambda w_: pl.BlockSpec((None, ROW_TILE, w_), lambda b, i: (b, i, 0))
    prev = pl.BlockSpec((None, HALO, d), lambda b, i: (b, jnp.clip((i - off) * per - 1, 0, nh - 1), 0))
    nxt = pl.BlockSpec((None, HALO, d), lambda b, i: (b, jnp.clip((i - off + 1) * per, 0, nh - 1), 0))
    mod = pl.BlockSpec((None, None, 1, 6 * d), lambda b, i: (b, jnp.minimum(i, 1), 0, 0))
    consts = [g, w, p["conv"], p["a0"], p["a2"], p["g2"], p["w0"], p["w2"], p["k_k"], p["k_a"], p["r_k"], p["bsum"]]
    widths = (URET_W, UGLA_W, RW_S1_W, RW_S2_W)
    kern = functools.partial(_proj_kernel, n_tiles=nt)
    return pl.pallas_call(
        kern if ctx is not None else functools.partial(kern, None),
        grid=(bsz, nt),
        in_specs=tile_specs(lambda i: i) + [prev, nxt, mod] + [_resident_spec(a.shape) for a in consts],
        out_specs=[row(w_) for w_ in widths],
        out_shape=[jax.ShapeDtypeStruct((bsz, nt * ROW_TILE, w_), F32) for w_ in widths],
        compiler_params=_params("arbitrary", "arbitrary"),
        name="proj",
    )(*arrays, x, x, modsel, *consts)


def _chunk_maps(ta, chunk):
    ns = ta // chunk
    nc = CTX_LEN // chunk

    def fwd(s):
        return s

    def bwd(s):
        return jnp.where(s < nc, nc - 1 - s, ns + nc - 1 - s)

    return ns, fwd, bwd


def _chunk_specs(bsz, ta, chunk, width):
    _, fwd, bwd = _chunk_maps(ta, chunk)
    return tuple(pl.BlockSpec((bsz, chunk, width), lambda s, m=m: (0, m(s), 0)) for m in (fwd, bwd))


def _rope_swap(x):
    lane = lax.broadcasted_iota(jnp.int32, x.shape, 1)
    w = x.shape[1]
    return jnp.where(lane % 32 < 16, pltpu.roll(x, w - 16, 1), pltpu.roll(x, 16, 1))


def _ret_kernel(uf_ref, ub_ref, cf_ref, sf_ref, cb_ref, sb_ref, dm_ref, qd_ref, kd_ref, cd_ref,
                of_ref, ob_ref, st_ref):
    @pl.when(pl.program_id(0) == 0)
    def _():
        st_ref[...] = jnp.zeros_like(st_ref)

    bsz = uf_ref.shape[0]
    groups = []
    for d, (u_ref, cos_ref, sin_ref) in enumerate(((uf_ref, cf_ref, sf_ref), (ub_ref, cb_ref, sb_ref))):
        cos, sin = cos_ref[...], sin_ref[...]
        for b in range(bsz):
            q = u_ref[b, :, 0:RET_W]
            k = u_ref[b, :, RET_W:2 * RET_W]
            q = q * cos + _rope_swap(q) * sin
            k = (k * cos + _rope_swap(k) * sin) * (RET_DH ** -0.5)
            groups.append(dict(d=d, b=b, q=_bf(q), k=_bf(k), qd=_bf(q * qd_ref[d]), kd=_bf(k * kd_ref[d]),
                               v=_bf(u_ref[b, :, 2 * RET_W:3 * RET_W])))
    chains = [(p, h, slice(h * RET_DH, (h + 1) * RET_DH)) for p in groups for h in range(RET_HEADS)]
    sc = [_bf(_dot_nt(p["q"][:, sl], p["k"][:, sl]) * dm_ref[p["d"], h]) for p, h, sl in chains]
    s0 = [st_ref[p["d"], p["b"], h] for p, h, _ in chains]
    outs = [_dot(sci, p["v"][:, sl]) + _dot_nt(p["qd"][:, sl], _bf(s)) for sci, s, (p, _, sl) in zip(sc, s0, chains)]
    for s, (p, h, sl) in zip(s0, chains):
        st_ref[p["d"], p["b"], h] = s * cd_ref[p["d"], h] + _dot_tn(p["v"][:, sl], p["kd"][:, sl])
    for gi, p in enumerate(groups):
        o_ref = ob_ref if p["d"] else of_ref
        o_ref[p["b"]] = jnp.concatenate(outs[gi * RET_HEADS:(gi + 1) * RET_HEADS], axis=1)


def _ret_consts():
    c = RET_CHUNK
    lg = np.log1p(-np.exp2(-5.0 - np.arange(RET_HEADS, dtype=np.float64)))
    pos = np.arange(c, dtype=np.float64)
    rel = pos[:, None] - pos[None, :]
    dm = np.zeros((2, RET_HEADS, c, c))
    qd = np.zeros((2, c, RET_W))
    kd = np.zeros((2, c, RET_W))
    cd = np.zeros((2, RET_HEADS, 1, RET_DH))
    for d in range(2):
        lgd = lg if d == 0 else lg[::-1]
        p = pos if d == 0 else c - 1.0 - pos
        for h in range(RET_HEADS):
            intra = np.where(rel >= 0, np.exp(lgd[h] * np.maximum(rel, 0.0)), 0.0)
            dm[d, h] = intra if d == 0 else intra.T
            qd[d, :, h * RET_DH:(h + 1) * RET_DH] = np.exp(lgd[h] * (p + 1.0))[:, None]
            kd[d, :, h * RET_DH:(h + 1) * RET_DH] = np.exp(lgd[h] * (c - 1.0 - p))[:, None]
            cd[d, h] = np.exp(lgd[h] * c)
    return tuple(jnp.asarray(a, F32) for a in (dm, qd, kd, cd))


def _ret_call(u_ret, cos, sin):
    bsz, ta, _ = u_ret.shape
    c = RET_CHUNK
    ns, fwd, bwd = _chunk_maps(ta, c)
    uf, ub = _chunk_specs(bsz, ta, c, URET_W)
    tf, tb = (pl.BlockSpec((c, RET_W), lambda s, m=m: (m(s), 0)) for m in (fwd, bwd))
    of, ob = _chunk_specs(bsz, ta, c, RET_W)
    dm, qd, kd, cd = _ret_consts()
    return pl.pallas_call(
        _ret_kernel,
        grid=(ns,),
        in_specs=[uf, ub, tf, tf, tb, tb, _const_spec(dm.shape), _const_spec(qd.shape),
                  _const_spec(kd.shape), _const_spec(cd.shape)],
        out_specs=[of, ob],
        out_shape=[jax.ShapeDtypeStruct((bsz, ta, RET_W), F32)] * 2,
        scratch_shapes=[pltpu.VMEM((2, bsz, RET_HEADS, RET_DH, RET_DH), F32)],
        compiler_params=_params("arbitrary"),
        name="ret_scan",
    )(u_ret, u_ret, cos, sin, cos, sin, dm, qd, kd, cd)


def _gla_kernel(uf_ref, ub_ref, wa_ref, ba_ref, of_ref, ob_ref, st_ref):
    @pl.when(pl.program_id(0) == 0)
    def _():
        st_ref[...] = jnp.zeros_like(st_ref)

    bsz = uf_ref.shape[0]
    c = CHUNK
    nsub = uf_ref.shape[1] // c
    half = c // 2
    groups = []
    for d, u_ref in enumerate((uf_ref, ub_ref)):
        rev = d == 1
        incl, _ = _order_masks(c, rev)
        incl_b = incl.astype(BF16)
        pos = lambda p: c - 1 - p if rev else p
        row = lax.broadcasted_iota(jnp.int32, (c, 1), 0)
        first = (row >= half) if rev else (row < half)
        ti = lax.broadcasted_iota(jnp.int32, (c, c), 0)
        si = lax.broadcasted_iota(jnp.int32, (c, c), 1)
        if rev:
            straddle = jnp.logical_and(ti < half, si >= half)
        else:
            straddle = jnp.logical_and(ti >= half, si < half)
        for j in range(nsub):
            rows = pl.ds((nsub - 1 - j if rev else j) * c, c)
            for b in range(bsz):
                lr = _bf(u_ref[b, rows, 2 * GLA_KWP + 2 * GLA_VWP:UGLA_W])
                z = _dot(lr, wa_ref[d]) + ba_ref[d]
                la = -_softplus(-z) * (1.0 / GLA_TAU)
                g = _cumsum_dot(incl_b, la)
                g_end = jnp.sum(la, axis=0, keepdims=True)
                g_row = lambda p: g[pos(p):pos(p) + 1, :]
                piv = jnp.where(first, g_row(half // 2 - 1), g_row(half + half // 2 - 1))
                g_cut = g_row(half - 1)
                q = u_ref[b, rows, 0:GLA_KWP] * (GLA_DK ** -0.5)
                k = u_ref[b, rows, GLA_KWP:2 * GLA_KWP]
                groups.append(dict(
                    d=d, b=b, j=j, rows=rows, incl=incl, straddle=straddle, e_end=jnp.exp(g_end),
                    qt=_bf(q * jnp.exp(g - piv)), kt=_bf(k * jnp.exp(piv - g)),
                    qo=_bf(q * jnp.exp(jnp.minimum(g - g_cut, 0.0))),
                    ko=_bf(k * jnp.exp(jnp.minimum(g_cut - g, 0.0))),
                    qs=_bf(q * jnp.exp(g)), ks=_bf(k * jnp.exp(g_end - g)),
                    v=_bf(u_ref[b, rows, 2 * GLA_KWP:2 * GLA_KWP + GLA_VWP])))
    chains = [(p, h, slice(h * GLA_DKP, (h + 1) * GLA_DKP), slice(h * GLA_DVP, (h + 1) * GLA_DVP))
              for p in groups for h in range(GLA_HEADS)]
    same = [_dot_nt(p["qt"][:, sk], p["kt"][:, sk]) for p, _, sk, _ in chains]
    cross = [_dot_nt(p["qo"][:, sk], p["ko"][:, sk]) for p, _, sk, _ in chains]
    sc = [_bf(jnp.where(p["incl"], jnp.where(p["straddle"], x, s), 0.0))
          for s, x, (p, _, _, _) in zip(same, cross, chains)]
    intra = [_dot(sci, p["v"][:, sv]) for sci, (p, _, _, sv) in zip(sc, chains)]
    kv = [_dot_tn(p["v"][:, sv], p["ks"][:, sk]) for p, _, sk, sv in chains]
    state = {(d, b, h): st_ref[d, b, h] for d in range(2) for b in range(bsz) for h in range(GLA_HEADS)}
    outs = {}
    for j in range(nsub):
        step = [(i, ch) for i, ch in enumerate(chains) if ch[0]["j"] == j]
        s0 = [state[(p["d"], p["b"], h)] for _, (p, h, _, _) in step]
        for s, (i, (p, h, sk, _)) in zip(s0, step):
            outs[(p["d"], p["b"], j, h)] = intra[i] + _dot_nt(p["qs"][:, sk], _bf(s))
            state[(p["d"], p["b"], h)] = s * p["e_end"][:, sk] + kv[i]
    for key, s in state.items():
        st_ref[key] = s
    for p in groups:
        o_ref = ob_ref if p["d"] else of_ref
        o_ref[p["b"], p["rows"], :] = jnp.concatenate(
            [outs[(p["d"], p["b"], p["j"], h)] for h in range(GLA_HEADS)], axis=1)


def _gla_call(u_gla, wa, ba):
    bsz, ta, _ = u_gla.shape
    ns, _, _ = _chunk_maps(ta, SCAN_BLOCK)
    uf, ub = _chunk_specs(bsz, ta, SCAN_BLOCK, UGLA_W)
    of, ob = _chunk_specs(bsz, ta, SCAN_BLOCK, GLA_VWP)
    return pl.pallas_call(
        _gla_kernel,
        grid=(ns,),
        in_specs=[uf, ub, _const_spec(wa.shape), _const_spec(ba.shape)],
        out_specs=[of, ob],
        out_shape=[jax.ShapeDtypeStruct((bsz, ta, GLA_VWP), F32)] * 2,
        scratch_shapes=[pltpu.VMEM((2, bsz, GLA_HEADS, GLA_DVP, GLA_DKP), F32)],
        compiler_params=_params("arbitrary"),
        name="gla_scan",
    )(u_gla, u_gla, wa, ba)


def _rw_kernel(sf_ref, sb_ref, of_ref, ob_ref, st_ref):
    @pl.when(pl.program_id(0) == 0)
    def _():
        st_ref[...] = jnp.zeros_like(st_ref)

    bsz = sf_ref.shape[0]
    c, w, dh = CHUNK, RW_W, RW_DH
    groups = []
    for d, s_ref in enumerate((sf_ref, sb_ref)):
        incl, _ = _order_masks(c, d == 1)
        incl_b = incl.astype(BF16)
        tt = lax.broadcasted_iota(jnp.int32, (2 * c, 2 * c), 0)
        ss = lax.broadcasted_iota(jnp.int32, (2 * c, 2 * c), 1) & (c - 1)
        t_in = tt & (c - 1)
        before = (ss > t_in) if d == 1 else (ss < t_in)
        pair_mask = jnp.logical_or(before, jnp.logical_and(tt >= c, ss == t_in))
        t64 = lax.broadcasted_iota(jnp.int32, (c, c), 0)
        s64 = lax.broadcasted_iota(jnp.int32, (c, c), 1)
        eye = (t64 == s64).astype(F32)
        late, early = (s64, t64) if d == 1 else (t64, s64)
        level_masks = [
            jnp.logical_and((t64 >> (j + 1)) == (s64 >> (j + 1)),
                            jnp.logical_and(((late >> j) & 1) == 1, ((early >> j) & 1) == 0))
            for j in range(6)]
        for b in range(bsz):
            la = s_ref[b, :, (5 + d) * w:(6 + d) * w]
            g = _cumsum_dot(incl_b, la)
            g_end = jnp.sum(la, axis=0, keepdims=True)
            e_g = jnp.exp(g)
            e_gp = jnp.exp(g - la)
            e_inv = jnp.exp(-g_end)
            e_k = jnp.exp(g_end - g)
            e_end = jnp.exp(g_end)
            r0 = s_ref[b, :, 0:w] * e_g
            a0 = s_ref[b, :, 3 * w:4 * w] * e_gp
            rt, at = r0 * e_inv, a0 * e_inv
            kt = s_ref[b, :, w:2 * w] * e_k
            bt = s_ref[b, :, 4 * w:5 * w] * e_k
            groups.append(dict(
                d=d, b=b, pair_mask=pair_mask, level_masks=level_masks, eye=eye, e_end=e_end, a0=a0, r0=_bf(r0),
                lhs=_bf(jnp.concatenate([at, rt], axis=0)),
                rhs=_bf(jnp.concatenate([bt, kt], axis=0)),
                v=_bf(s_ref[b, :, 2 * w:3 * w])))

    chains = [(q, h, slice(h * dh, (h + 1) * dh)) for q in groups for h in range(RW_HEADS)]
    pair = [jnp.where(q["pair_mask"], _dot_nt(q["lhs"][:, sl], q["rhs"][:, sl]), 0.0)
            for q, _, sl in chains]
    n = [p[0:c, 0:c] for p in pair]
    x = [jnp.concatenate([_dot(_bf(p[0:c, c:2 * c]), q["v"][:, sl]), q["a0"][:, sl]], axis=1)
         for p, (q, _, sl) in zip(pair, chains)]
    tm = [jnp.where(q["level_masks"][0], ni, q["eye"]) for ni, (q, _, _) in zip(n, chains)]
    for lvl in range(1, 6):
        tb = [_bf(t) for t in tm]
        left = [_bf(_dot(tbi, _bf(jnp.where(q["level_masks"][lvl], ni, 0.0))))
                for tbi, ni, (q, _, _) in zip(tb, n, chains)]
        tm = [t + _dot(li, tbi) for t, li, tbi in zip(tm, left, tb)]
    x = [_dot(_bf(t), _bf(xi)) for t, xi in zip(tm, x)]
    s0 = [st_ref[q["d"], q["b"], h] for q, h, _ in chains]
    s0b = [_bf(s) for s in s0]
    ms = [_dot_nt(jnp.concatenate([_bf(xi[:, dh:2 * dh]), q["r0"][:, sl]], axis=0), sb)
          for xi, sb, (q, _, sl) in zip(x, s0b, chains)]
    uv = [jnp.concatenate([_bf(xi[:, 0:dh] + m[0:c]), q["v"][:, sl]], axis=0)
          for xi, m, (q, _, sl) in zip(x, ms, chains)]
    outs = [m[c:2 * c] + _dot(_bf(p[c:2 * c, :]), w) for p, m, w in zip(pair, ms, uv)]
    for s, w, (q, h, sl) in zip(s0, uv, chains):
        st_ref[q["d"], q["b"], h] = s * q["e_end"][:, sl] + _dot_tn(w, q["rhs"][:, sl])
    for gi, q in enumerate(groups):
        o_ref = ob_ref if q["d"] else of_ref
        o_ref[q["b"]] = jnp.concatenate(outs[gi * RW_HEADS:(gi + 1) * RW_HEADS], axis=1)


def _rw_call(s1):
    bsz, ta, _ = s1.shape
    ns, _, _ = _chunk_maps(ta, CHUNK)
    sf, sb = _chunk_specs(bsz, ta, CHUNK, RW_S1_W)
    of, ob = _chunk_specs(bsz, ta, CHUNK, RW_W)
    return pl.pallas_call(
        _rw_kernel,
        grid=(ns,),
        in_specs=[sf, sb],
        out_specs=[of, ob],
        out_shape=[jax.ShapeDtypeStruct((bsz, ta, RW_W), F32)] * 2,
        scratch_shapes=[pltpu.VMEM((2, bsz, RW_HEADS, RW_DH, RW_DH), F32)],
        compiler_params=_params("arbitrary"),
        name="rw_scan",
    )(s1, s1)


def _post_kernel(c_ref, x_ref, mod_ref, modf_ref, raf_ref, rab_ref, rg_ref, gaf_ref, gab_ref, gg_ref,
                 waf_ref, wab_ref, s2_ref, rn_ref, gn_ref, lw_ref, lb_ref,
                 avg_r_ref, sum_g_ref, avg_w_ref, wo_ref, pn_ref,
                 pre_ref, wg_ref, wu_ref, wd_ref, post_ref, o_ref, x1_ref, *, cur_tile):
    d = D_MODEL

    @pl.when(jnp.logical_and(pl.program_id(0) == 0, pl.program_id(1) == 0))
    def _():
        x1_ref[...] = jnp.zeros_like(x1_ref)

    x_prev = x1_ref[...]
    o = raf_ref[...] + rab_ref[...]
    avg = avg_r_ref[...]
    mu = _group_dot(o, avg)
    oc = o - mu
    var = _group_dot(oc * oc, avg)
    ya = _silu(rg_ref[...]) * (oc * lax.rsqrt(var + 1e-5) * rn_ref[...])
    o = gaf_ref[...] + gab_ref[...]
    ms = _group_dot(o * o, sum_g_ref[...]) * (1.0 / GLA_DV)
    yb = _silu(gg_ref[...]) * (o * lax.rsqrt(ms + 1e-5) * gn_ref[...])
    o = waf_ref[...] + wab_ref[...]
    avg = avg_w_ref[...]
    mu = _group_dot(o, avg)
    oc = o - mu
    var = _group_dot(oc * oc, avg)
    yc = (oc * lax.rsqrt(var + 64e-5) * lw_ref[...] + lb_ref[...] + s2_ref[:, RW_W:2 * RW_W]) * s2_ref[:, 0:RW_W]
    y = (_dot(_bf(ya), wo_ref[0:RET_W, :]) + _dot(_bf(yb), wo_ref[RET_W:RET_W + GLA_VWP, :])
         + _dot(_bf(yc), wo_ref[RET_W + GLA_VWP:MIXP_W, :]))
    x_res = _tile_rows(cur_tile(pl.program_id(1)), c_ref, x_ref)
    x1_ref[...] = x_res + mod_ref[:, 2 * d:3 * d] * _rms(y, pn_ref[...])
    h = _rms(x_prev, pre_ref[...])
    h = _bf(h * (1.0 + modf_ref[:, 4 * d:5 * d]) + modf_ref[:, 3 * d:4 * d])
    act = _bf(_silu(_dot(h, wg_ref[...])) * _dot(h, wu_ref[...]))
    f = _dot(act, wd_ref[...])
    o_ref[...] = x_prev + modf_ref[:, 5 * d:6 * d] * _rms(f, post_ref[...])


def _post_call(ctx, x, modsel, ra, u_ret, ga, u_gla, wa, s2, p, tile_off):
    bsz, tx, d = x.shape
    arrays, tile_specs, off = _stream_inputs(ctx, x)
    nt = tx // ROW_TILE + off - tile_off
    cur = lambda i: jnp.minimum(i, nt - 1) + tile_off
    prv = lambda i: jnp.maximum(i - 1, 0)
    row = lambda w_, cb=0: pl.BlockSpec((None, ROW_TILE, w_), lambda b, i: (b, cur(i), cb))
    kern = functools.partial(_post_kernel, cur_tile=cur)
    mod = pl.BlockSpec((None, None, 1, 6 * d), lambda b, i: (b, jnp.minimum(cur(i), 1), 0, 0))
    modf = pl.BlockSpec((None, None, 1, 6 * d), lambda b, i: (b, jnp.minimum(prv(i) + tile_off, 1), 0, 0))
    consts = [p["ret_norm"], p["gla_norm"], p["ln_w"], p["ln_b"], p["avg_r"], p["sum_g"], p["avg_w"],
              p["w_out"], p["norm_mix_post"],
              p["norm_ffn_pre"], p["w_gate"], p["w_up"], p["w_down"], p["norm_ffn_post"]]
    return pl.pallas_call(
        kern if ctx is not None else functools.partial(kern, None),
        grid=(bsz, nt + 1),
        in_specs=tile_specs(cur) + [mod, modf, row(RET_W), row(RET_W), row(RET_W, 3),
                  row(GLA_VWP), row(GLA_VWP), row(GLA_VWP, 2),
                  row(RW_W), row(RW_W), row(RW_S2_W)] + [_resident_spec(a.shape) for a in consts],
        out_specs=pl.BlockSpec((None, ROW_TILE, d), lambda b, i: (b, prv(i), 0)),
        out_shape=jax.ShapeDtypeStruct((bsz, nt * ROW_TILE, d), F32),
        scratch_shapes=[pltpu.VMEM((ROW_TILE, d), F32)],
        compiler_params=_params("arbitrary", "arbitrary"),
        name="post",
    )(*arrays, modsel, modsel, ra[0], ra[1], u_ret, ga[0], ga[1], u_gla, wa[0], wa[1], s2, *consts)


def _pad_last(a, n):
    return jnp.pad(a, [(0, 0)] * (a.ndim - 1) + [(0, n - a.shape[-1])])


def _pad_heads(a, heads, dh, dhp):
    lead = a.shape[:-1]
    a = a.reshape(lead + (heads, dh))
    return _pad_last(a, dhp).reshape(lead + (heads * dhp,))


def _arrange_rw(a):
    w = RW_W
    return jnp.concatenate([a[..., 0:3 * w + 128], _pad_last(a[..., 3 * w + 128:3 * w + 192], 128),
                            a[..., 3 * w + 192:3 * w + 320]], axis=-1)


def _arrange_in(a):
    g0 = URET_W
    kw, vw = GLA_HEADS * GLA_DK, GLA_HEADS * GLA_DV
    r0 = g0 + 2 * kw + 2 * vw + 2 * GLA_RANK
    return jnp.concatenate([
        a[..., 0:g0],
        _pad_heads(a[..., g0:g0 + kw], GLA_HEADS, GLA_DK, GLA_DKP),
        _pad_heads(a[..., g0 + kw:g0 + 2 * kw], GLA_HEADS, GLA_DK, GLA_DKP),
        _pad_heads(a[..., g0 + 2 * kw:g0 + 2 * kw + vw], GLA_HEADS, GLA_DV, GLA_DVP),
        _pad_heads(a[..., g0 + 2 * kw + vw:g0 + 2 * kw + 2 * vw], GLA_HEADS, GLA_DV, GLA_DVP),
        _pad_last(a[..., g0 + 2 * kw + 2 * vw:r0], 128),
        _arrange_rw(a[..., r0:]),
    ], axis=-1)


def _block_diag_ones(n, blk, val=1.0):
    i = np.arange(n)
    return jnp.asarray(np.where(i[:, None] // blk == i[None, :] // blk, val, 0.0), BF16)


def _rope_tables(t_lat):
    rows = t_lat // GRID_W
    row = jnp.repeat(jnp.arange(rows, dtype=F32), GRID_W)
    col = jnp.tile(jnp.arange(GRID_W, dtype=F32), rows)
    nf = RET_DH // 4
    inv = ROPE_THETA ** (-jnp.arange(nf, dtype=F32) / nf)
    ar, ac = row[:, None] * inv, col[:, None] * inv
    cos = jnp.concatenate([jnp.cos(ar), jnp.cos(ar), jnp.cos(ac), jnp.cos(ac)], axis=1)
    sin = jnp.concatenate([-jnp.sin(ar), jnp.sin(ar), -jnp.sin(ac), jnp.sin(ac)], axis=1)
    cos = jnp.concatenate([jnp.ones((CTX_LEN, RET_DH), F32), cos], axis=0)
    sin = jnp.concatenate([jnp.zeros((CTX_LEN, RET_DH), F32), sin], axis=0)
    return jnp.tile(cos, (1, RET_HEADS)), jnp.tile(sin, (1, RET_HEADS))


def kernel(x, c, ctx, c_ctx, w_mod, b_mod, norm_mix_pre, norm_mix_post, norm_ffn_pre, norm_ffn_post,
           w_in, ret_norm, gla_wa2_f, gla_ba_f, gla_wa2_b, gla_ba_b, gla_norm, rw_conv,
           rw_w0_f, rw_w2_f, rw_w0_b, rw_w2_b, rw_a0, rw_a2, rw_g2, rw_k_k, rw_k_a, rw_r_k,
           rw_ln_w, rw_ln_b, w_out, w_ffn_gate, w_ffn_up, w_ffn_down):
    bsz, t_lat, d = x.shape
    depth = w_mod.shape[0]
    assert d == D_MODEL and ctx.shape[1] == CTX_LEN and t_lat % ROW_TILE == 0 and bsz <= 7

    cc = jnp.zeros((8, d), F32).at[0:bsz].set(c).at[bsz].set(c_ctx)
    mods = _mod_call(cc, w_mod, b_mod)
    cos, sin = _rope_tables(t_lat)
    avg_r = _block_diag_ones(RET_W, RET_DH, 1.0 / RET_DH)
    avg_w = _block_diag_ones(RW_W, RW_DH, 1.0 / RW_DH)
    sum_w = _block_diag_ones(RW_W, RW_DH)
    sum_g = _block_diag_ones(GLA_VWP, GLA_DVP)

    ctx_in, xa = ctx, x
    for l in range(depth):
        last = l == depth - 1
        modsel = jnp.stack([jnp.broadcast_to(mods[l, bsz], (bsz, 6 * d)), mods[l, 0:bsz]], axis=1)[:, :, None, :]
        row = lambda a: a[l].reshape(1, -1)

        w2 = jnp.zeros((128, 2 * RW_W), F32).at[0:64, 0:RW_W].set(rw_w2_f[l]).at[64:128, RW_W:].set(rw_w2_b[l])
        prep = {
            "conv": _arrange_rw(rw_conv[l]),
            "a0": row(rw_a0), "a2": _bf(jnp.pad(rw_a2[l], [(0, 64), (0, 0)])), "g2": _bf(rw_g2[l]),
            "w0": jnp.concatenate([rw_w0_f[l], rw_w0_b[l]]).reshape(1, -1), "w2": _bf(w2),
            "k_k": row(rw_k_k), "k_a": row(rw_k_a), "r_k": row(rw_r_k), "bsum": sum_w,
        }
        u_ret, u_gla, s1, s2 = _proj_call(ctx_in, xa, modsel, row(norm_mix_pre), _bf(_arrange_in(w_in[l])), prep)

        ra = _ret_call(u_ret, cos, sin)

        wa = jnp.stack([_pad_last(jnp.pad(_pad_heads(gla_wa2_f[l], GLA_HEADS, GLA_DK, GLA_DKP),
                                          [(0, 128 - GLA_RANK), (0, 0)]), GLA_KWP),
                        _pad_last(jnp.pad(_pad_heads(gla_wa2_b[l], GLA_HEADS, GLA_DK, GLA_DKP),
                                          [(GLA_RANK, 128 - 2 * GLA_RANK), (0, 0)]), GLA_KWP)])
        ba = jnp.stack([_pad_heads(gla_ba_f[l], GLA_HEADS, GLA_DK, GLA_DKP),
                        _pad_heads(gla_ba_b[l], GLA_HEADS, GLA_DK, GLA_DKP)])[:, None, :]
        ga = _gla_call(u_gla, _bf(wa), ba)

        wo = _rw_call(s1)

        wout = w_out[l]
        outp = {
            "ret_norm": row(ret_norm),
            "gla_norm": jnp.tile(_pad_last(gla_norm[l], GLA_DVP), GLA_HEADS).reshape(1, -1),
            "ln_w": row(rw_ln_w), "ln_b": row(rw_ln_b),
            "avg_r": avg_r, "sum_g": sum_g, "avg_w": avg_w,
            "w_out": _bf(jnp.concatenate([
                wout[0:RET_W],
                jnp.pad(wout[RET_W:RET_W + GLA_HEADS * GLA_DV].reshape(GLA_HEADS, GLA_DV, d),
                        [(0, 0), (0, GLA_DVP - GLA_DV), (0, 0)]).reshape(GLA_VWP, d),
                wout[RET_W + GLA_HEADS * GLA_DV:]], axis=0)),
            "norm_mix_post": row(norm_mix_post),
            "norm_ffn_pre": row(norm_ffn_pre), "w_gate": _bf(w_ffn_gate[l]), "w_up": _bf(w_ffn_up[l]),
            "w_down": _bf(w_ffn_down[l]), "norm_ffn_post": row(norm_ffn_post),
        }
        tile_off = CTX_LEN // ROW_TILE if last else 0
        xa = _post_call(ctx_in, xa, modsel, ra, u_ret, ga, u_gla, wo, s2, outp, tile_off)
        ctx_in = None
    return xa
```

```python
import functools

import numpy as np
import jax
import jax.numpy as jnp
from jax import lax
from jax.experimental import pallas as pl
from jax.experimental.pallas import tpu as pltpu

F32 = jnp.float32
BF16 = jnp.bfloat16

D_MODEL = 1024
CTX_LEN = 256
GRID_W = 64
ROPE_THETA = 10000.0

RET_HEADS, RET_DH = 4, 64
RET_W = RET_HEADS * RET_DH
GLA_HEADS, GLA_DK, GLA_DV = 4, 48, 96
GLA_DKP, GLA_DVP = 64, 128
GLA_KWP, GLA_VWP = GLA_HEADS * GLA_DKP, GLA_HEADS * GLA_DVP
GLA_RANK = 16
GLA_TAU = 16.0
RW_HEADS, RW_DH = 6, 64
RW_W = RW_HEADS * RW_DH
FFN_HIDDEN = 2816

URET_W = 4 * RET_W
UGLA_W = 2 * GLA_KWP + 2 * GLA_VWP + 128
URW_W = 3 * RW_W + 128 + 128 + 128
NP_IN = URET_W + UGLA_W + URW_W
MIXP_W = RET_W + GLA_VWP + RW_W

CHUNK = 64
RET_CHUNK = 256
SCAN_BLOCK = 256
RW_BLOCK = 128
ROW_TILE = 256
VMEM_LIMIT = 56 * 1024 * 1024


def _bf(x):
    return x.astype(BF16)


def _dot(a, b):
    return jnp.dot(a, b, preferred_element_type=F32)


def _dot_nt(a, b):
    return lax.dot_general(a, b, (((1,), (1,)), ((), ())), preferred_element_type=F32)


def _dot_tn(a, b):
    return lax.dot_general(a, b, (((0,), (0,)), ((), ())), preferred_element_type=F32)


def _split2(x):
    hi = _bf(x)
    return hi, _bf(x - hi.astype(F32))


def _group_dot(x, m):
    return _dot(_bf(x), m)


def _cumsum_dot(m, x):
    hi = _bf(x)
    r1 = x - hi.astype(F32)
    mid = _bf(r1)
    lo = _bf(r1 - mid.astype(F32))
    return _dot(m, hi) + (_dot(m, mid) + _dot(m, lo))


def _dot_3pass(a, b):
    ah, al = _split2(a)
    bh, bl = _split2(b)
    return _dot(ah, bh) + (_dot(ah, bl) + _dot(al, bh))


def _sigmoid(x):
    return 1.0 / (1.0 + jnp.exp(-x))


def _silu(x):
    return x * _sigmoid(x)


def _softplus(x):
    return jnp.maximum(x, 0.0) + jnp.log(1.0 + jnp.exp(-jnp.abs(x)))


def _rms(x, g, eps=1e-6):
    return x * lax.rsqrt(jnp.mean(x * x, axis=-1, keepdims=True) + eps) * g


def _order_masks(c, reverse):
    t = lax.broadcasted_iota(jnp.int32, (c, c), 0)
    s = lax.broadcasted_iota(jnp.int32, (c, c), 1)
    if reverse:
        return s >= t, s > t
    return s <= t, s < t


def _params(*sem):
    return pltpu.CompilerParams(dimension_semantics=sem, vmem_limit_bytes=VMEM_LIMIT)


def _const_spec(shape):
    nd = len(shape)
    return pl.BlockSpec(shape, lambda *_: (0,) * nd)


def _resident_spec(shape):
    nd = len(shape)
    return pl.BlockSpec(shape, lambda *_: (0,) * nd, pipeline_mode=pl.Buffered(1))


def _layer_spec(a, l):
    nd = a.ndim - 1
    return pl.BlockSpec((None,) + a.shape[1:], lambda *_: (l,) + (0,) * nd, pipeline_mode=pl.Buffered(1))


def _mod_kernel(c_ref, w_ref, b_ref, o_ref):
    c = c_ref[...]
    o_ref[...] = _dot_3pass(_silu(c), w_ref[...]) + b_ref[...]


def _mod_call(cc, w_mod, b_mod):
    depth, d, n = w_mod.shape
    tn = 1024
    return pl.pallas_call(
        _mod_kernel,
        grid=(depth, n // tn),
        in_specs=[
            pl.BlockSpec((8, d), lambda l, j: (0, 0)),
            pl.BlockSpec((None, d, tn), lambda l, j: (l, 0, j)),
            pl.BlockSpec((None, 1, tn), lambda l, j: (l, 0, j)),
        ],
        out_specs=pl.BlockSpec((None, 8, tn), lambda l, j: (l, 0, j)),
        out_shape=jax.ShapeDtypeStruct((depth, 8, n), F32),
        compiler_params=_params("arbitrary", "arbitrary"),
        name="mod",
    )(cc, w_mod, b_mod.reshape(depth, 1, n))


RW_S1_W = 7 * RW_W
RW_S2_W = 2 * RW_W
HALO = 16


def _tile_rows(i, c_ref, x_ref):
    return x_ref[...] if c_ref is None else jnp.where(i == 0, c_ref[...], x_ref[...])


def _proj_kernel(c_ref, x_ref, xp_ref, xn_ref, mod_ref, g_ref, w_ref, cw_ref, a0_ref, a2_ref, g2_ref, w0_ref,
                 w2_ref, kk_ref, ka_ref, rk_ref, bsum_ref, ur_ref, ug_ref, s1_ref, s2_ref, *, n_tiles):
    i = pl.program_id(1)
    d, tm, te = D_MODEL, ROW_TILE, ROW_TILE + 2 * HALO
    xe = jnp.concatenate([xp_ref[...], _tile_rows(i, c_ref, x_ref), xn_ref[...]], axis=0)
    h = _rms(xe, g_ref[...])
    h = h * (1.0 + mod_ref[:, d:2 * d]) + mod_ref[:, 0:d]
    rows = lax.broadcasted_iota(jnp.int32, (te, 1), 0)
    keep = jnp.logical_and(jnp.logical_or(rows >= HALO, i >= 2),
                           jnp.logical_or(rows < HALO + tm, jnp.logical_and(i >= 1, i < n_tiles - 1)))
    h = _bf(jnp.where(keep, h, 0.0))
    ue = _dot(h, w_ref[:, URET_W + UGLA_W:NP_IN])
    hm = h[HALO:HALO + tm]
    ur_ref[...] = _dot(hm, w_ref[:, 0:URET_W])
    ug_ref[...] = _dot(hm, w_ref[:, URET_W:URET_W + UGLA_W])

    mid = lambda a: a[HALO:HALO + tm]
    uc = (mid(pltpu.roll(ue, 1, 0)) * cw_ref[0:1, :] + mid(ue) * cw_ref[1:2, :]
          + mid(pltpu.roll(ue, te - 1, 0)) * cw_ref[2:3, :])
    w = RW_W
    r, k, v = uc[:, 0:w], uc[:, w:2 * w], uc[:, 2 * w:3 * w]
    lr_w = uc[:, 3 * w:3 * w + 128]
    lr_a = uc[:, 3 * w + 128:3 * w + 256]
    lr_g = uc[:, 3 * w + 256:3 * w + 384]
    a = _sigmoid(a0_ref[...] + _dot(_bf(lr_a), a2_ref[...]))
    gate = _dot(_bf(_sigmoid(lr_g)), g2_ref[...])
    kk = k * kk_ref[...]
    bsum = bsum_ref[...]
    kk = kk * lax.rsqrt(jnp.maximum(_group_dot(kk * kk, bsum), 1e-24))
    k2 = k * (1.0 + (a - 1.0) * ka_ref[...])
    ld = -float(np.exp(-0.5)) * _sigmoid(w0_ref[...] + _dot(_bf(jnp.tanh(lr_w)), w2_ref[...]))
    bonus = _group_dot(r * k2 * rk_ref[...], bsum) * v
    s1_ref[:, 0:w] = r
    s1_ref[:, w:2 * w] = k2
    s1_ref[:, 2 * w:3 * w] = v
    s1_ref[:, 3 * w:4 * w] = -kk
    s1_ref[:, 4 * w:5 * w] = kk * a
    s1_ref[:, 5 * w:7 * w] = ld
    s2_ref[:, 0:w] = gate
    s2_ref[:, w:2 * w] = bonus


def _stream_inputs(ctx, x):
    d = x.shape[2]
    off = 0 if ctx is None else CTX_LEN // ROW_TILE

    def specs(tile):
        x_spec = pl.BlockSpec((None, ROW_TILE, d), lambda b, i: (b, jnp.maximum(tile(i) - off, 0), 0))
        if ctx is None:
            return [x_spec]
        return [pl.BlockSpec((None, ROW_TILE, d), lambda b, i: (b, 0, 0)), x_spec]

    return ([x] if ctx is None else [ctx, x]), specs, off


def _proj_call(ctx, x, modsel, p, sum_w, l):
    bsz, tx, d = x.shape
    arrays, tile_specs, off = _stream_inputs(ctx, x)
    nt = tx // ROW_TILE + off
    per = ROW_TILE // HALO
    nh = tx // HALO
    row = lambda w_: pl.BlockSpec((None, ROW_TILE, w_), lambda b, i: (b, i, 0))
    prev = pl.BlockSpec((None, HALO, d), lambda b, i: (b, jnp.clip((i - off) * per - 1, 0, nh - 1), 0))
    nxt = pl.BlockSpec((None, HALO, d), lambda b, i: (b, jnp.clip((i - off + 1) * per, 0, nh - 1), 0))
    mod = pl.BlockSpec((None, None, None, 1, 6 * d), lambda b, i: (l, b, jnp.minimum(i, 1), 0, 0))
    consts = [p[k] for k in ("norm_mix_pre", "w_in", "conv", "a0", "a2", "g2", "w0", "w2", "k_k", "k_a", "r_k")]
    widths = (URET_W, UGLA_W, RW_S1_W, RW_S2_W)
    kern = functools.partial(_proj_kernel, n_tiles=nt)
    return pl.pallas_call(
        kern if ctx is not None else functools.partial(kern, None),
        grid=(bsz, nt),
        in_specs=(tile_specs(lambda i: i) + [prev, nxt, mod] + [_layer_spec(a, l) for a in consts]
                  + [_resident_spec(sum_w.shape)]),
        out_specs=[row(w_) for w_ in widths],
        out_shape=[jax.ShapeDtypeStruct((bsz, nt * ROW_TILE, w_), F32) for w_ in widths],
        compiler_params=_params("arbitrary", "arbitrary"),
        name="proj",
    )(*arrays, x, x, modsel, *consts, sum_w)


def _chunk_maps(ta, chunk):
    ns = ta // chunk
    nc = CTX_LEN // chunk

    def fwd(s):
        return s

    def bwd(s):
        return jnp.where(s < nc, nc - 1 - s, ns + nc - 1 - s)

    return ns, fwd, bwd


def _chunk_specs(bsz, ta, chunk, width):
    _, fwd, bwd = _chunk_maps(ta, chunk)
    return tuple(pl.BlockSpec((bsz, chunk, width), lambda s, m=m: (0, m(s), 0)) for m in (fwd, bwd))


def _rope_swap(x):
    lane = lax.broadcasted_iota(jnp.int32, x.shape, 1)
    w = x.shape[1]
    return jnp.where(lane % 32 < 16, pltpu.roll(x, w - 16, 1), pltpu.roll(x, 16, 1))


def _ret_kernel(uf_ref, ub_ref, cf_ref, sf_ref, cb_ref, sb_ref, dm_ref, qd_ref, kd_ref, cd_ref,
                of_ref, ob_ref, st_ref):
    @pl.when(pl.program_id(0) == 0)
    def _():
        st_ref[...] = jnp.zeros_like(st_ref)

    bsz = uf_ref.shape[0]
    groups = []
    for d, (u_ref, cos_ref, sin_ref) in enumerate(((uf_ref, cf_ref, sf_ref), (ub_ref, cb_ref, sb_ref))):
        cos, sin = cos_ref[...], sin_ref[...]
        for b in range(bsz):
            q = u_ref[b, :, 0:RET_W]
            k = u_ref[b, :, RET_W:2 * RET_W]
            q = q * cos + _rope_swap(q) * sin
            k = (k * cos + _rope_swap(k) * sin) * (RET_DH ** -0.5)
            groups.append(dict(d=d, b=b, q=_bf(q), k=_bf(k), qd=_bf(q * qd_ref[d]), kd=_bf(k * kd_ref[d]),
                               v=_bf(u_ref[b, :, 2 * RET_W:3 * RET_W])))
    chains = [(p, h, slice(h * RET_DH, (h + 1) * RET_DH)) for p in groups for h in range(RET_HEADS)]
    sc = [_bf(_dot_nt(p["q"][:, sl], p["k"][:, sl]) * dm_ref[p["d"], h]) for p, h, sl in chains]
    s0 = [st_ref[p["d"], p["b"], h] for p, h, _ in chains]
    outs = [_dot(sci, p["v"][:, sl]) + _dot_nt(p["qd"][:, sl], _bf(s)) for sci, s, (p, _, sl) in zip(sc, s0, chains)]
    for s, (p, h, sl) in zip(s0, chains):
        st_ref[p["d"], p["b"], h] = s * cd_ref[p["d"], h] + _dot_tn(p["v"][:, sl], p["kd"][:, sl])
    for gi, p in enumerate(groups):
        o_ref = ob_ref if p["d"] else of_ref
        o_ref[p["b"]] = jnp.concatenate(outs[gi * RET_HEADS:(gi + 1) * RET_HEADS], axis=1)


def _ret_consts():
    c = RET_CHUNK
    lg = np.log1p(-np.exp2(-5.0 - np.arange(RET_HEADS, dtype=np.float64)))
    pos = np.arange(c, dtype=np.float64)
    rel = pos[:, None] - pos[None, :]
    dm = np.zeros((2, RET_HEADS, c, c))
    qd = np.zeros((2, c, RET_W))
    kd = np.zeros((2, c, RET_W))
    cd = np.zeros((2, RET_HEADS, 1, RET_DH))
    for d in range(2):
        lgd = lg if d == 0 else lg[::-1]
        p = pos if d == 0 else c - 1.0 - pos
        for h in range(RET_HEADS):
            intra = np.where(rel >= 0, np.exp(lgd[h] * np.maximum(rel, 0.0)), 0.0)
            dm[d, h] = intra if d == 0 else intra.T
            qd[d, :, h * RET_DH:(h + 1) * RET_DH] = np.exp(lgd[h] * (p + 1.0))[:, None]
            kd[d, :, h * RET_DH:(h + 1) * RET_DH] = np.exp(lgd[h] * (c - 1.0 - p))[:, None]
            cd[d, h] = np.exp(lgd[h] * c)
    return tuple(jnp.asarray(a, F32) for a in (dm, qd, kd, cd))


def _ret_call(u_ret, cos, sin):
    bsz, ta, _ = u_ret.shape
    c = RET_CHUNK
    ns, fwd, bwd = _chunk_maps(ta, c)
    uf, ub = _chunk_specs(bsz, ta, c, URET_W)
    tf, tb = (pl.BlockSpec((c, RET_W), lambda s, m=m: (m(s), 0)) for m in (fwd, bwd))
    of, ob = _chunk_specs(bsz, ta, c, RET_W)
    dm, qd, kd, cd = _ret_consts()
    return pl.pallas_call(
        _ret_kernel,
        grid=(ns,),
        in_specs=[uf, ub, tf, tf, tb, tb, _const_spec(dm.shape), _const_spec(qd.shape),
                  _const_spec(kd.shape), _const_spec(cd.shape)],
        out_specs=[of, ob],
        out_shape=[jax.ShapeDtypeStruct((bsz, ta, RET_W), F32)] * 2,
        scratch_shapes=[pltpu.VMEM((2, bsz, RET_HEADS, RET_DH, RET_DH), F32)],
        compiler_params=_params("arbitrary"),
        name="ret_scan",
    )(u_ret, u_ret, cos, sin, cos, sin, dm, qd, kd, cd)


def _gla_kernel(uf_ref, ub_ref, wa_ref, ba_ref, of_ref, ob_ref, st_ref):
    @pl.when(pl.program_id(0) == 0)
    def _():
        st_ref[...] = jnp.zeros_like(st_ref)

    bsz = uf_ref.shape[0]
    c = CHUNK
    nsub = uf_ref.shape[1] // c
    half = c // 2
    groups = []
    for d, u_ref in enumerate((uf_ref, ub_ref)):
        rev = d == 1
        incl, _ = _order_masks(c, rev)
        incl_b = incl.astype(BF16)
        pos = lambda p: c - 1 - p if rev else p
        row = lax.broadcasted_iota(jnp.int32, (c, 1), 0)
        first = (row >= half) if rev else (row < half)
        ti = lax.broadcasted_iota(jnp.int32, (c, c), 0)
        si = lax.broadcasted_iota(jnp.int32, (c, c), 1)
        if rev:
            straddle = jnp.logical_and(ti < half, si >= half)
        else:
            straddle = jnp.logical_and(ti >= half, si < half)
        for j in range(nsub):
            rows = pl.ds((nsub - 1 - j if rev else j) * c, c)
            for b in range(bsz):
                lr = _bf(u_ref[b, rows, 2 * GLA_KWP + 2 * GLA_VWP:UGLA_W])
                z = _dot(lr, wa_ref[d]) + ba_ref[d]
                la = -_softplus(-z) * (1.0 / GLA_TAU)
                g = _cumsum_dot(incl_b, la)
                g_end = jnp.sum(la, axis=0, keepdims=True)
                g_row = lambda p: g[pos(p):pos(p) + 1, :]
                piv = jnp.where(first, g_row(half // 2 - 1), g_row(half + half // 2 - 1))
                g_cut = g_row(half - 1)
                q = u_ref[b, rows, 0:GLA_KWP] * (GLA_DK ** -0.5)
                k = u_ref[b, rows, GLA_KWP:2 * GLA_KWP]
                groups.append(dict(
                    d=d, b=b, j=j, rows=rows, incl=incl, straddle=straddle, e_end=jnp.exp(g_end),
                    qt=_bf(q * jnp.exp(g - piv)), kt=_bf(k * jnp.exp(piv - g)),
                    qo=_bf(q * jnp.exp(jnp.minimum(g - g_cut, 0.0))),
                    ko=_bf(k * jnp.exp(jnp.minimum(g_cut - g, 0.0))),
                    qs=_bf(q * jnp.exp(g)), ks=_bf(k * jnp.exp(g_end - g)),
                    v=_bf(u_ref[b, rows, 2 * GLA_KWP:2 * GLA_KWP + GLA_VWP])))
    chains = [(p, h, slice(h * GLA_DKP, (h + 1) * GLA_DKP), slice(h * GLA_DVP, (h + 1) * GLA_DVP))
              for p in groups for h in range(GLA_HEADS)]
    same = [_dot_nt(p["qt"][:, sk], p["kt"][:, sk]) for p, _, sk, _ in chains]
    cross = [_dot_nt(p["qo"][:, sk], p["ko"][:, sk]) for p, _, sk, _ in chains]
    sc = [_bf(jnp.where(p["incl"], jnp.where(p["straddle"], x, s), 0.0))
          for s, x, (p, _, _, _) in zip(same, cross, chains)]
    intra = [_dot(sci, p["v"][:, sv]) for sci, (p, _, _, sv) in zip(sc, chains)]
    kv = [_dot_tn(p["v"][:, sv], p["ks"][:, sk]) for p, _, sk, sv in chains]
    state = {(d, b, h): st_ref[d, b, h] for d in range(2) for b in range(bsz) for h in range(GLA_HEADS)}
    outs = {}
    for j in range(nsub):
        step = [(i, ch) for i, ch in enumerate(chains) if ch[0]["j"] == j]
        s0 = [state[(p["d"], p["b"], h)] for _, (p, h, _, _) in step]
        for s, (i, (p, h, sk, _)) in zip(s0, step):
            outs[(p["d"], p["b"], j, h)] = intra[i] + _dot_nt(p["qs"][:, sk], _bf(s))
            state[(p["d"], p["b"], h)] = s * p["e_end"][:, sk] + kv[i]
    for key, s in state.items():
        st_ref[key] = s
    for p in groups:
        o_ref = ob_ref if p["d"] else of_ref
        o_ref[p["b"], p["rows"], :] = jnp.concatenate(
            [outs[(p["d"], p["b"], p["j"], h)] for h in range(GLA_HEADS)], axis=1)


def _gla_call(u_gla, p, l):
    bsz, ta, _ = u_gla.shape
    wa, ba = p["gla_wa"], p["gla_ba"]
    ns, _, _ = _chunk_maps(ta, SCAN_BLOCK)
    uf, ub = _chunk_specs(bsz, ta, SCAN_BLOCK, UGLA_W)
    of, ob = _chunk_specs(bsz, ta, SCAN_BLOCK, GLA_VWP)
    return pl.pallas_call(
        _gla_kernel,
        grid=(ns,),
        in_specs=[uf, ub, _layer_spec(wa, l), _layer_spec(ba, l)],
        out_specs=[of, ob],
        out_shape=[jax.ShapeDtypeStruct((bsz, ta, GLA_VWP), F32)] * 2,
        scratch_shapes=[pltpu.VMEM((2, bsz, GLA_HEADS, GLA_DVP, GLA_DKP), F32)],
        compiler_params=_params("arbitrary"),
        name="gla_scan",
    )(u_gla, u_gla, wa, ba)


def _rw_kernel(sf_ref, sb_ref, of_ref, ob_ref, st_ref):
    @pl.when(pl.program_id(0) == 0)
    def _():
        st_ref[...] = jnp.zeros_like(st_ref)

    bsz = sf_ref.shape[0]
    c, w, dh = CHUNK, RW_W, RW_DH
    nsub = sf_ref.shape[1] // c
    groups = []
    for d, s_ref in enumerate((sf_ref, sb_ref)):
        incl, _ = _order_masks(c, d == 1)
        incl_b = incl.astype(BF16)
        tt = lax.broadcasted_iota(jnp.int32, (2 * c, 2 * c), 0)
        ss = lax.broadcasted_iota(jnp.int32, (2 * c, 2 * c), 1) & (c - 1)
        t_in = tt & (c - 1)
        before = (ss > t_in) if d == 1 else (ss < t_in)
        pair_mask = jnp.logical_or(before, jnp.logical_and(tt >= c, ss == t_in))
        t64 = lax.broadcasted_iota(jnp.int32, (c, c), 0)
        s64 = lax.broadcasted_iota(jnp.int32, (c, c), 1)
        eye = (t64 == s64).astype(F32)
        late, early = (s64, t64) if d == 1 else (t64, s64)
        level_masks = [
            jnp.logical_and((t64 >> (j + 1)) == (s64 >> (j + 1)),
                            jnp.logical_and(((late >> j) & 1) == 1, ((early >> j) & 1) == 0))
            for j in range(6)]
        for j in range(nsub):
            rows = pl.ds((nsub - 1 - j if d == 1 else j) * c, c)
            for b in range(bsz):
                la = s_ref[b, rows, (5 + d) * w:(6 + d) * w]
                g = _cumsum_dot(incl_b, la)
                g_end = jnp.sum(la, axis=0, keepdims=True)
                e_g = jnp.exp(g)
                e_gp = jnp.exp(g - la)
                e_inv = jnp.exp(-g_end)
                e_k = jnp.exp(g_end - g)
                e_end = jnp.exp(g_end)
                r0 = s_ref[b, rows, 0:w] * e_g
                a0 = s_ref[b, rows, 3 * w:4 * w] * e_gp
                rt, at = r0 * e_inv, a0 * e_inv
                kt = s_ref[b, rows, w:2 * w] * e_k
                bt = s_ref[b, rows, 4 * w:5 * w] * e_k
                groups.append(dict(
                    d=d, b=b, j=j, rows=rows, pair_mask=pair_mask, level_masks=level_masks, eye=eye,
                    e_end=e_end, a0=a0, r0=_bf(r0),
                    lhs=_bf(jnp.concatenate([at, rt], axis=0)),
                    rhs=_bf(jnp.concatenate([bt, kt], axis=0)),
                    v=_bf(s_ref[b, rows, 2 * w:3 * w])))

    chains = [(q, h, slice(h * dh, (h + 1) * dh)) for q in groups for h in range(RW_HEADS)]
    pair = [jnp.where(q["pair_mask"], _dot_nt(q["lhs"][:, sl], q["rhs"][:, sl]), 0.0)
            for q, _, sl in chains]
    n = [p[0:c, 0:c] for p in pair]
    x = [jnp.concatenate([_dot(_bf(p[0:c, c:2 * c]), q["v"][:, sl]), q["a0"][:, sl]], axis=1)
         for p, (q, _, sl) in zip(pair, chains)]
    tm = [jnp.where(q["level_masks"][0], ni, q["eye"]) for ni, (q, _, _) in zip(n, chains)]
    for lvl in range(1, 6):
        tb = [_bf(t) for t in tm]
        left = [_bf(_dot(tbi, _bf(jnp.where(q["level_masks"][lvl], ni, 0.0))))
                for tbi, ni, (q, _, _) in zip(tb, n, chains)]
        tm = [t + _dot(li, tbi) for t, li, tbi in zip(tm, left, tb)]
    x = [_dot(_bf(t), _bf(xi)) for t, xi in zip(tm, x)]
    mr = [jnp.concatenate([_bf(xi[:, dh:2 * dh]), q["r0"][:, sl]], axis=0) for xi, (q, _, sl) in zip(x, chains)]
    arbk = [_bf(p[c:2 * c, :]) for p in pair]
    state = {(d, b, h): st_ref[d, b, h] for d in range(2) for b in range(bsz) for h in range(RW_HEADS)}
    outs = {}
    for j in range(nsub):
        step = [(i, ch) for i, ch in enumerate(chains) if ch[0]["j"] == j]
        s0 = [state[(q["d"], q["b"], h)] for _, (q, h, _) in step]
        ms = [_dot_nt(mr[i], _bf(s)) for s, (i, _) in zip(s0, step)]
        uv = [jnp.concatenate([_bf(x[i][:, 0:dh] + m[0:c]), q["v"][:, sl]], axis=0)
              for m, (i, (q, _, sl)) in zip(ms, step)]
        for s, m, w, (i, (q, h, sl)) in zip(s0, ms, uv, step):
            outs[(q["d"], q["b"], j, h)] = m[c:2 * c] + _dot(arbk[i], w)
            state[(q["d"], q["b"], h)] = s * q["e_end"][:, sl] + _dot_tn(w, q["rhs"][:, sl])
    for key, s in state.items():
        st_ref[key] = s
    for q in groups:
        o_ref = ob_ref if q["d"] else of_ref
        o_ref[q["b"], q["rows"], :] = jnp.concatenate(
            [outs[(q["d"], q["b"], q["j"], h)] for h in range(RW_HEADS)], axis=1)


def _rw_call(s1):
    bsz, ta, _ = s1.shape
    ns, _, _ = _chunk_maps(ta, RW_BLOCK)
    sf, sb = _chunk_specs(bsz, ta, RW_BLOCK, RW_S1_W)
    of, ob = _chunk_specs(bsz, ta, RW_BLOCK, RW_W)
    return pl.pallas_call(
        _rw_kernel,
        grid=(ns,),
        in_specs=[sf, sb],
        out_specs=[of, ob],
        out_shape=[jax.ShapeDtypeStruct((bsz, ta, RW_W), F32)] * 2,
        scratch_shapes=[pltpu.VMEM((2, bsz, RW_HEADS, RW_DH, RW_DH), F32)],
        compiler_params=_params("arbitrary"),
        name="rw_scan",
    )(s1, s1)


def _post_kernel(c_ref, x_ref, mod_ref, modf_ref, raf_ref, rab_ref, rg_ref, gaf_ref, gab_ref, gg_ref,
                 waf_ref, wab_ref, s2_ref, rn_ref, gn_ref, lw_ref, lb_ref,
                 avg_r_ref, sum_g_ref, avg_w_ref, wo_ref, pn_ref,
                 pre_ref, wg_ref, wu_ref, wd_ref, post_ref, o_ref, x1_ref, *, cur_tile):
    d = D_MODEL

    @pl.when(jnp.logical_and(pl.program_id(0) == 0, pl.program_id(1) == 0))
    def _():
        x1_ref[...] = jnp.zeros_like(x1_ref)

    x_prev = x1_ref[...]
    o = raf_ref[...] + rab_ref[...]
    avg = avg_r_ref[...]
    mu = _group_dot(o, avg)
    oc = o - mu
    var = _group_dot(oc * oc, avg)
    ya = _silu(rg_ref[...]) * (oc * lax.rsqrt(var + 1e-5) * rn_ref[...])
    o = gaf_ref[...] + gab_ref[...]
    ms = _group_dot(o * o, sum_g_ref[...]) * (1.0 / GLA_DV)
    yb = _silu(gg_ref[...]) * (o * lax.rsqrt(ms + 1e-5) * gn_ref[...])
    o = waf_ref[...] + wab_ref[...]
    avg = avg_w_ref[...]
    mu = _group_dot(o, avg)
    oc = o - mu
    var = _group_dot(oc * oc, avg)
    yc = (oc * lax.rsqrt(var + 64e-5) * lw_ref[...] + lb_ref[...] + s2_ref[:, RW_W:2 * RW_W]) * s2_ref[:, 0:RW_W]
    y = (_dot(_bf(ya), wo_ref[0:RET_W, :]) + _dot(_bf(yb), wo_ref[RET_W:RET_W + GLA_VWP, :])
         + _dot(_bf(yc), wo_ref[RET_W + GLA_VWP:MIXP_W, :]))
    x_res = _tile_rows(cur_tile(pl.program_id(1)), c_ref, x_ref)
    x1_ref[...] = x_res + mod_ref[:, 2 * d:3 * d] * _rms(y, pn_ref[...])
    h = _rms(x_prev, pre_ref[...])
    h = _bf(h * (1.0 + modf_ref[:, 4 * d:5 * d]) + modf_ref[:, 3 * d:4 * d])
    act = _bf(_silu(_dot(h, wg_ref[...])) * _dot(h, wu_ref[...]))
    f = _dot(act, wd_ref[...])
    o_ref[...] = x_prev + modf_ref[:, 5 * d:6 * d] * _rms(f, post_ref[...])


def _post_call(ctx, x, modsel, ra, u_ret, ga, u_gla, wa, s2, p, shared, l, tile_off):
    bsz, tx, d = x.shape
    arrays, tile_specs, off = _stream_inputs(ctx, x)
    nt = tx // ROW_TILE + off - tile_off
    cur = lambda i: jnp.minimum(i, nt - 1) + tile_off
    prv = lambda i: jnp.maximum(i - 1, 0)
    row = lambda w_, cb=0: pl.BlockSpec((None, ROW_TILE, w_), lambda b, i: (b, cur(i), cb))
    kern = functools.partial(_post_kernel, cur_tile=cur)
    mod = pl.BlockSpec((None, None, None, 1, 6 * d), lambda b, i: (l, b, jnp.minimum(cur(i), 1), 0, 0))
    modf = pl.BlockSpec((None, None, None, 1, 6 * d),
                        lambda b, i: (l, b, jnp.minimum(prv(i) + tile_off, 1), 0, 0))
    spec = lambda k: _layer_spec(p[k], l) if k in p else _resident_spec(shared[k].shape)
    names = ["ret_norm", "gla_norm", "ln_w", "ln_b", "avg_r", "sum_g", "avg_w", "w_out", "norm_mix_post",
             "norm_ffn_pre", "w_gate", "w_up", "w_down", "norm_ffn_post"]
    consts = [p[k] if k in p else shared[k] for k in names]
    return pl.pallas_call(
        kern if ctx is not None else functools.partial(kern, None),
        grid=(bsz, nt + 1),
        in_specs=tile_specs(cur) + [mod, modf, row(RET_W), row(RET_W), row(RET_W, 3),
                  row(GLA_VWP), row(GLA_VWP), row(GLA_VWP, 2),
                  row(RW_W), row(RW_W), row(RW_S2_W)] + [spec(k) for k in names],
        out_specs=pl.BlockSpec((None, ROW_TILE, d), lambda b, i: (b, prv(i), 0)),
        out_shape=jax.ShapeDtypeStruct((bsz, nt * ROW_TILE, d), F32),
        scratch_shapes=[pltpu.VMEM((ROW_TILE, d), F32)],
        compiler_params=_params("arbitrary", "arbitrary"),
        name="post",
    )(*arrays, modsel, modsel, ra[0], ra[1], u_ret, ga[0], ga[1], u_gla, wa[0], wa[1], s2, *consts)


def _pad_last(a, n):
    return jnp.pad(a, [(0, 0)] * (a.ndim - 1) + [(0, n - a.shape[-1])])


def _pad_heads(a, heads, dh, dhp):
    lead = a.shape[:-1]
    a = a.reshape(lead + (heads, dh))
    return _pad_last(a, dhp).reshape(lead + (heads * dhp,))


def _arrange_rw(a):
    w = RW_W
    return jnp.concatenate([a[..., 0:3 * w + 128], _pad_last(a[..., 3 * w + 128:3 * w + 192], 128),
                            a[..., 3 * w + 192:3 * w + 320]], axis=-1)


def _arrange_in(a):
    g0 = URET_W
    kw, vw = GLA_HEADS * GLA_DK, GLA_HEADS * GLA_DV
    r0 = g0 + 2 * kw + 2 * vw + 2 * GLA_RANK
    return jnp.concatenate([
        a[..., 0:g0],
        _pad_heads(a[..., g0:g0 + kw], GLA_HEADS, GLA_DK, GLA_DKP),
        _pad_heads(a[..., g0 + kw:g0 + 2 * kw], GLA_HEADS, GLA_DK, GLA_DKP),
        _pad_heads(a[..., g0 + 2 * kw:g0 + 2 * kw + vw], GLA_HEADS, GLA_DV, GLA_DVP),
        _pad_heads(a[..., g0 + 2 * kw + vw:g0 + 2 * kw + 2 * vw], GLA_HEADS, GLA_DV, GLA_DVP),
        _pad_last(a[..., g0 + 2 * kw + 2 * vw:r0], 128),
        _arrange_rw(a[..., r0:]),
    ], axis=-1)


def _block_diag_ones(n, blk, val=1.0):
    i = np.arange(n)
    return jnp.asarray(np.where(i[:, None] // blk == i[None, :] // blk, val, 0.0), BF16)


def _rope_tables(t_lat):
    rows = t_lat // GRID_W
    row = jnp.repeat(jnp.arange(rows, dtype=F32), GRID_W)
    col = jnp.tile(jnp.arange(GRID_W, dtype=F32), rows)
    nf = RET_DH // 4
    inv = ROPE_THETA ** (-jnp.arange(nf, dtype=F32) / nf)
    ar, ac = row[:, None] * inv, col[:, None] * inv
    cos = jnp.concatenate([jnp.cos(ar), jnp.cos(ar), jnp.cos(ac), jnp.cos(ac)], axis=1)
    sin = jnp.concatenate([-jnp.sin(ar), jnp.sin(ar), -jnp.sin(ac), jnp.sin(ac)], axis=1)
    cos = jnp.concatenate([jnp.ones((CTX_LEN, RET_DH), F32), cos], axis=0)
    sin = jnp.concatenate([jnp.zeros((CTX_LEN, RET_DH), F32), sin], axis=0)
    return jnp.tile(cos, (1, RET_HEADS)), jnp.tile(sin, (1, RET_HEADS))


def kernel(x, c, ctx, c_ctx, w_mod, b_mod, norm_mix_pre, norm_mix_post, norm_ffn_pre, norm_ffn_post,
           w_in, ret_norm, gla_wa2_f, gla_ba_f, gla_wa2_b, gla_ba_b, gla_norm, rw_conv,
           rw_w0_f, rw_w2_f, rw_w0_b, rw_w2_b, rw_a0, rw_a2, rw_g2, rw_k_k, rw_k_a, rw_r_k,
           rw_ln_w, rw_ln_b, w_out, w_ffn_gate, w_ffn_up, w_ffn_down):
    bsz, t_lat, d = x.shape
    depth = w_mod.shape[0]
    assert d == D_MODEL and ctx.shape[1] == CTX_LEN and t_lat % ROW_TILE == 0 and bsz <= 7

    cc = jnp.zeros((8, d), F32).at[0:bsz].set(c).at[bsz].set(c_ctx)
    mods = _mod_call(cc, w_mod, b_mod)
    cos, sin = _rope_tables(t_lat)
    avg_r = _block_diag_ones(RET_W, RET_DH, 1.0 / RET_DH)
    avg_w = _block_diag_ones(RW_W, RW_DH, 1.0 / RW_DH)
    sum_w = _block_diag_ones(RW_W, RW_DH)
    sum_g = _block_diag_ones(GLA_VWP, GLA_DVP)

    shared = {"avg_r": avg_r, "sum_g": sum_g, "avg_w": avg_w}
    modsel = jnp.stack([jnp.broadcast_to(mods[:, bsz:bsz + 1], (depth, bsz, 6 * d)), mods[:, 0:bsz]],
                       axis=2)[:, :, :, None, :]

    rows = lambda a: a.reshape(depth, 1, -1)
    gla_heads = lambda a: _pad_heads(a, GLA_HEADS, GLA_DK, GLA_DKP)
    n_gv = GLA_HEADS * GLA_DV
    p = {
        "norm_mix_pre": rows(norm_mix_pre), "w_in": _bf(_arrange_in(w_in)),
        "conv": _arrange_rw(rw_conv), "a0": rows(rw_a0),
        "a2": _bf(jnp.pad(rw_a2, [(0, 0), (0, 64), (0, 0)])), "g2": _bf(rw_g2),
        "w0": jnp.concatenate([rw_w0_f, rw_w0_b], axis=-1)[:, None, :],
        "w2": _bf(jnp.concatenate([jnp.pad(rw_w2_f, [(0, 0), (0, 0), (0, RW_W)]),
                                   jnp.pad(rw_w2_b, [(0, 0), (0, 0), (RW_W, 0)])], axis=1)),
        "k_k": rows(rw_k_k), "k_a": rows(rw_k_a), "r_k": rows(rw_r_k),
        "gla_wa": _bf(jnp.stack([jnp.pad(gla_heads(gla_wa2_f), [(0, 0), (0, 128 - GLA_RANK), (0, 0)]),
                                 jnp.pad(gla_heads(gla_wa2_b), [(0, 0), (GLA_RANK, 128 - 2 * GLA_RANK), (0, 0)])],
                                axis=1)),
        "gla_ba": jnp.stack([gla_heads(gla_ba_f), gla_heads(gla_ba_b)], axis=1)[:, :, None, :],
        "ret_norm": rows(ret_norm),
        "gla_norm": rows(jnp.tile(_pad_last(gla_norm, GLA_DVP), (1, GLA_HEADS))),
        "ln_w": rows(rw_ln_w), "ln_b": rows(rw_ln_b),
        "w_out": _bf(jnp.concatenate([
            w_out[:, 0:RET_W],
            jnp.pad(w_out[:, RET_W:RET_W + n_gv].reshape(depth, GLA_HEADS, GLA_DV, d),
                    [(0, 0), (0, 0), (0, GLA_DVP - GLA_DV), (0, 0)]).reshape(depth, GLA_VWP, d),
            w_out[:, RET_W + n_gv:]], axis=1)),
        "norm_mix_post": rows(norm_mix_post), "norm_ffn_pre": rows(norm_ffn_pre),
        "w_gate": _bf(w_ffn_gate), "w_up": _bf(w_ffn_up), "w_down": _bf(w_ffn_down),
        "norm_ffn_post": rows(norm_ffn_post),
    }

    ctx_in, xa = ctx, x
    for l in range(depth):
        u_ret, u_gla, s1, s2 = _proj_call(ctx_in, xa, modsel, p, sum_w, l)
        ra = _ret_call(u_ret, cos, sin)
        ga = _gla_call(u_gla, p, l)
        wo = _rw_call(s1)
        tile_off = CTX_LEN // ROW_TILE if l == depth - 1 else 0
        xa = _post_call(ctx_in, xa, modsel, ra, u_ret, ga, u_gla, wo, s2, p, shared, l, tile_off)
        ctx_in = None
    return xa
```

```python
import functools

import numpy as np
import jax
import jax.numpy as jnp
from jax import lax
from jax.experimental import pallas as pl
from jax.experimental.pallas import tpu as pltpu

F32 = jnp.float32
BF16 = jnp.bfloat16

D_MODEL = 1024
CTX_LEN = 256
GRID_W = 64
ROPE_THETA = 10000.0

RET_HEADS, RET_DH = 4, 64
RET_W = RET_HEADS * RET_DH
GLA_HEADS, GLA_DK, GLA_DV = 4, 48, 96
GLA_DKP, GLA_DVP = 64, 128
GLA_KWP, GLA_VWP = GLA_HEADS * GLA_DKP, GLA_HEADS * GLA_DVP
GLA_RANK = 16
GLA_TAU = 16.0
RW_HEADS, RW_DH = 6, 64
RW_W = RW_HEADS * RW_DH
FFN_HIDDEN = 2816

URET_W = 4 * RET_W
UGLA_W = 2 * GLA_KWP + 2 * GLA_VWP + 128
URW_W = 3 * RW_W + 128 + 128 + 128
NP_IN = URET_W + UGLA_W + URW_W
MIXP_W = RET_W + GLA_VWP + RW_W

CHUNK = 64
SCAN_BLOCK = 128
RET_CHUNK = SCAN_BLOCK
ROW_TILE = 256
VMEM_LIMIT = 56 * 1024 * 1024


def _bf(x):
    return x.astype(BF16)


def _dot(a, b):
    return jnp.dot(a, b, preferred_element_type=F32)


def _dot_nt(a, b):
    return lax.dot_general(a, b, (((1,), (1,)), ((), ())), preferred_element_type=F32)


def _dot_tn(a, b):
    return lax.dot_general(a, b, (((0,), (0,)), ((), ())), preferred_element_type=F32)


def _split2(x):
    hi = _bf(x)
    return hi, _bf(x - hi.astype(F32))


def _group_dot(x, m):
    return _dot(_bf(x), m)


def _cumsum_dot(m, x):
    hi = _bf(x)
    r1 = x - hi.astype(F32)
    mid = _bf(r1)
    lo = _bf(r1 - mid.astype(F32))
    return _dot(m, hi) + (_dot(m, mid) + _dot(m, lo))


def _dot_3pass(a, b):
    ah, al = _split2(a)
    bh, bl = _split2(b)
    return _dot(ah, bh) + (_dot(ah, bl) + _dot(al, bh))


def _sigmoid(x):
    return 1.0 / (1.0 + jnp.exp(-x))


def _silu(x):
    return x * _sigmoid(x)


def _softplus(x):
    return jnp.maximum(x, 0.0) + jnp.log(1.0 + jnp.exp(-jnp.abs(x)))


def _rms(x, g, eps=1e-6):
    return x * lax.rsqrt(jnp.mean(x * x, axis=-1, keepdims=True) + eps) * g


def _order_masks(c, reverse):
    t = lax.broadcasted_iota(jnp.int32, (c, c), 0)
    s = lax.broadcasted_iota(jnp.int32, (c, c), 1)
    if reverse:
        return s >= t, s > t
    return s <= t, s < t


def _params(*sem):
    return pltpu.CompilerParams(dimension_semantics=sem, vmem_limit_bytes=VMEM_LIMIT)


def _const_spec(shape):
    nd = len(shape)
    return pl.BlockSpec(shape, lambda *_: (0,) * nd)


def _resident_spec(shape):
    nd = len(shape)
    return pl.BlockSpec(shape, lambda *_: (0,) * nd, pipeline_mode=pl.Buffered(1))


def _layer_spec(a, l):
    nd = a.ndim - 1
    return pl.BlockSpec((None,) + a.shape[1:], lambda *_: (l,) + (0,) * nd, pipeline_mode=pl.Buffered(1))


def _mod_kernel(c_ref, w_ref, b_ref, o_ref):
    c = c_ref[...]
    o_ref[...] = _dot_3pass(_silu(c), w_ref[...]) + b_ref[...]


def _mod_call(cc, w_mod, b_mod):
    depth, d, n = w_mod.shape
    tn = 1024
    return pl.pallas_call(
        _mod_kernel,
        grid=(depth, n // tn),
        in_specs=[
            pl.BlockSpec((8, d), lambda l, j: (0, 0)),
            pl.BlockSpec((None, d, tn), lambda l, j: (l, 0, j)),
            pl.BlockSpec((None, 1, tn), lambda l, j: (l, 0, j)),
        ],
        out_specs=pl.BlockSpec((None, 8, tn), lambda l, j: (l, 0, j)),
        out_shape=jax.ShapeDtypeStruct((depth, 8, n), F32),
        compiler_params=_params("arbitrary", "arbitrary"),
        name="mod",
    )(cc, w_mod, b_mod.reshape(depth, 1, n))


RW_S1_W = 7 * RW_W
RW_S2_W = 2 * RW_W
HALO = 16


def _tile_rows(i, c_ref, x_ref):
    return x_ref[...] if c_ref is None else jnp.where(i == 0, c_ref[...], x_ref[...])


def _proj_kernel(c_ref, x_ref, xp_ref, xn_ref, mod_ref, g_ref, w_ref, cw_ref, a0_ref, a2_ref, g2_ref, w0_ref,
                 w2_ref, kk_ref, ka_ref, rk_ref, bsum_ref, ur_ref, ug_ref, s1_ref, s2_ref, *, n_tiles):
    i = pl.program_id(1)
    d, tm, te = D_MODEL, ROW_TILE, ROW_TILE + 2 * HALO
    xe = jnp.concatenate([xp_ref[...], _tile_rows(i, c_ref, x_ref), xn_ref[...]], axis=0)
    h = _rms(xe, g_ref[...])
    h = h * (1.0 + mod_ref[:, d:2 * d]) + mod_ref[:, 0:d]
    rows = lax.broadcasted_iota(jnp.int32, (te, 1), 0)
    keep = jnp.logical_and(jnp.logical_or(rows >= HALO, i >= 2),
                           jnp.logical_or(rows < HALO + tm, jnp.logical_and(i >= 1, i < n_tiles - 1)))
    h = _bf(jnp.where(keep, h, 0.0))
    ue = _dot(h, w_ref[:, URET_W + UGLA_W:NP_IN])
    hm = h[HALO:HALO + tm]
    ur_ref[...] = _dot(hm, w_ref[:, 0:URET_W])
    ug_ref[...] = _dot(hm, w_ref[:, URET_W:URET_W + UGLA_W])

    mid = lambda a: a[HALO:HALO + tm]
    uc = (mid(pltpu.roll(ue, 1, 0)) * cw_ref[0:1, :] + mid(ue) * cw_ref[1:2, :]
          + mid(pltpu.roll(ue, te - 1, 0)) * cw_ref[2:3, :])
    w = RW_W
    r, k, v = uc[:, 0:w], uc[:, w:2 * w], uc[:, 2 * w:3 * w]
    lr_w = uc[:, 3 * w:3 * w + 128]
    lr_a = uc[:, 3 * w + 128:3 * w + 256]
    lr_g = uc[:, 3 * w + 256:3 * w + 384]
    a = _sigmoid(a0_ref[...] + _dot(_bf(lr_a), a2_ref[...]))
    gate = _dot(_bf(_sigmoid(lr_g)), g2_ref[...])
    kk = k * kk_ref[...]
    bsum = bsum_ref[...]
    kk = kk * lax.rsqrt(jnp.maximum(_group_dot(kk * kk, bsum), 1e-24))
    k2 = k * (1.0 + (a - 1.0) * ka_ref[...])
    ld = -float(np.exp(-0.5)) * _sigmoid(w0_ref[...] + _dot(_bf(jnp.tanh(lr_w)), w2_ref[...]))
    bonus = _group_dot(r * k2 * rk_ref[...], bsum) * v
    s1_ref[:, 0:w] = r
    s1_ref[:, w:2 * w] = k2
    s1_ref[:, 2 * w:3 * w] = v
    s1_ref[:, 3 * w:4 * w] = -kk
    s1_ref[:, 4 * w:5 * w] = kk * a
    s1_ref[:, 5 * w:7 * w] = ld
    s2_ref[:, 0:w] = gate
    s2_ref[:, w:2 * w] = bonus


def _stream_inputs(ctx, x):
    d = x.shape[2]
    off = 0 if ctx is None else CTX_LEN // ROW_TILE

    def specs(tile):
        x_spec = pl.BlockSpec((None, ROW_TILE, d), lambda b, i: (b, jnp.maximum(tile(i) - off, 0), 0))
        if ctx is None:
            return [x_spec]
        return [pl.BlockSpec((None, ROW_TILE, d), lambda b, i: (b, 0, 0)), x_spec]

    return ([x] if ctx is None else [ctx, x]), specs, off


def _proj_call(ctx, x, modsel, p, sum_w, l):
    bsz, tx, d = x.shape
    arrays, tile_specs, off = _stream_inputs(ctx, x)
    nt = tx // ROW_TILE + off
    per = ROW_TILE // HALO
    nh = tx // HALO
    row = lambda w_: pl.BlockSpec((None, ROW_TILE, w_), lambda b, i: (b, i, 0))
    prev = pl.BlockSpec((None, HALO, d), lambda b, i: (b, jnp.clip((i - off) * per - 1, 0, nh - 1), 0))
    nxt = pl.BlockSpec((None, HALO, d), lambda b, i: (b, jnp.clip((i - off + 1) * per, 0, nh - 1), 0))
    mod = pl.BlockSpec((None, None, None, 1, 6 * d), lambda b, i: (l, b, jnp.minimum(i, 1), 0, 0))
    consts = [p[k] for k in ("norm_mix_pre", "w_in", "conv", "a0", "a2", "g2", "w0", "w2", "k_k", "k_a", "r_k")]
    widths = (URET_W, UGLA_W, RW_S1_W, RW_S2_W)
    kern = functools.partial(_proj_kernel, n_tiles=nt)
    return pl.pallas_call(
        kern if ctx is not None else functools.partial(kern, None),
        grid=(bsz, nt),
        in_specs=(tile_specs(lambda i: i) + [prev, nxt, mod] + [_layer_spec(a, l) for a in consts]
                  + [_resident_spec(sum_w.shape)]),
        out_specs=[row(w_) for w_ in widths],
        out_shape=[jax.ShapeDtypeStruct((bsz, nt * ROW_TILE, w_), F32) for w_ in widths],
        compiler_params=_params("arbitrary", "arbitrary"),
        name="proj",
    )(*arrays, x, x, modsel, *consts, sum_w)


def _chunk_maps(ta, chunk):
    ns = ta // chunk
    nc = CTX_LEN // chunk

    def fwd(s):
        return s

    def bwd(s):
        return jnp.where(s < nc, nc - 1 - s, ns + nc - 1 - s)

    return ns, fwd, bwd


def _chunk_specs(bsz, ta, chunk, width):
    _, fwd, bwd = _chunk_maps(ta, chunk)
    return tuple(pl.BlockSpec((bsz, chunk, width), lambda s, m=m: (0, m(s), 0)) for m in (fwd, bwd))


def _rope_swap(x):
    lane = lax.broadcasted_iota(jnp.int32, x.shape, 1)
    w = x.shape[1]
    return jnp.where(lane % 32 < 16, pltpu.roll(x, w - 16, 1), pltpu.roll(x, 16, 1))


def _ret_body(uf_ref, ub_ref, cf_ref, sf_ref, cb_ref, sb_ref, dm_ref, qd_ref, kd_ref, cd_ref,
              of_ref, ob_ref, st_ref):
    bsz = uf_ref.shape[0]
    groups = []
    for d, (u_ref, cos_ref, sin_ref) in enumerate(((uf_ref, cf_ref, sf_ref), (ub_ref, cb_ref, sb_ref))):
        cos, sin = cos_ref[...], sin_ref[...]
        for b in range(bsz):
            q = u_ref[b, :, 0:RET_W]
            k = u_ref[b, :, RET_W:2 * RET_W]
            q = q * cos + _rope_swap(q) * sin
            k = (k * cos + _rope_swap(k) * sin) * (RET_DH ** -0.5)
            groups.append(dict(d=d, b=b, q=_bf(q), k=_bf(k), qd=_bf(q * qd_ref[d]), kd=_bf(k * kd_ref[d]),
                               v=_bf(u_ref[b, :, 2 * RET_W:3 * RET_W])))
    chains = [(p, h, slice(h * RET_DH, (h + 1) * RET_DH)) for p in groups for h in range(RET_HEADS)]
    yield
    sc = [_bf(_dot_nt(p["q"][:, sl], p["k"][:, sl]) * dm_ref[p["d"], h]) for p, h, sl in chains]
    yield
    s0 = [st_ref[p["d"], p["b"], h] for p, h, _ in chains]
    outs = [_dot(sci, p["v"][:, sl]) + _dot_nt(p["qd"][:, sl], _bf(s)) for sci, s, (p, _, sl) in zip(sc, s0, chains)]
    yield
    for s, (p, h, sl) in zip(s0, chains):
        st_ref[p["d"], p["b"], h] = s * cd_ref[p["d"], h] + _dot_tn(p["v"][:, sl], p["kd"][:, sl])
    for gi, p in enumerate(groups):
        o_ref = ob_ref if p["d"] else of_ref
        o_ref[p["b"]] = jnp.concatenate(outs[gi * RET_HEADS:(gi + 1) * RET_HEADS], axis=1)


def _ret_consts():
    c = RET_CHUNK
    lg = np.log1p(-np.exp2(-5.0 - np.arange(RET_HEADS, dtype=np.float64)))
    pos = np.arange(c, dtype=np.float64)
    rel = pos[:, None] - pos[None, :]
    dm = np.zeros((2, RET_HEADS, c, c))
    qd = np.zeros((2, c, RET_W))
    kd = np.zeros((2, c, RET_W))
    cd = np.zeros((2, RET_HEADS, 1, RET_DH))
    for d in range(2):
        lgd = lg if d == 0 else lg[::-1]
        p = pos if d == 0 else c - 1.0 - pos
        for h in range(RET_HEADS):
            intra = np.where(rel >= 0, np.exp(lgd[h] * np.maximum(rel, 0.0)), 0.0)
            dm[d, h] = intra if d == 0 else intra.T
            qd[d, :, h * RET_DH:(h + 1) * RET_DH] = np.exp(lgd[h] * (p + 1.0))[:, None]
            kd[d, :, h * RET_DH:(h + 1) * RET_DH] = np.exp(lgd[h] * (c - 1.0 - p))[:, None]
            cd[d, h] = np.exp(lgd[h] * c)
    return tuple(jnp.asarray(a, F32) for a in (dm, qd, kd, cd))


def _gla_body(uf_ref, ub_ref, wa_ref, ba_ref, of_ref, ob_ref, st_ref):
    bsz = uf_ref.shape[0]
    c = CHUNK
    nsub = uf_ref.shape[1] // c
    half = c // 2
    groups = []
    for d, u_ref in enumerate((uf_ref, ub_ref)):
        rev = d == 1
        incl, _ = _order_masks(c, rev)
        incl_b = incl.astype(BF16)
        pos = lambda p: c - 1 - p if rev else p
        row = lax.broadcasted_iota(jnp.int32, (c, 1), 0)
        first = (row >= half) if rev else (row < half)
        ti = lax.broadcasted_iota(jnp.int32, (c, c), 0)
        si = lax.broadcasted_iota(jnp.int32, (c, c), 1)
        if rev:
            straddle = jnp.logical_and(ti < half, si >= half)
        else:
            straddle = jnp.logical_and(ti >= half, si < half)
        for j in range(nsub):
            rows = pl.ds((nsub - 1 - j if rev else j) * c, c)
            for b in range(bsz):
                lr = _bf(u_ref[b, rows, 2 * GLA_KWP + 2 * GLA_VWP:UGLA_W])
                z = _dot(lr, wa_ref[d]) + ba_ref[d]
                la = -_softplus(-z) * (1.0 / GLA_TAU)
                g = _cumsum_dot(incl_b, la)
                g_end = jnp.sum(la, axis=0, keepdims=True)
                g_row = lambda p: g[pos(p):pos(p) + 1, :]
                piv = jnp.where(first, g_row(half // 2 - 1), g_row(half + half // 2 - 1))
                g_cut = g_row(half - 1)
                q = u_ref[b, rows, 0:GLA_KWP] * (GLA_DK ** -0.5)
                k = u_ref[b, rows, GLA_KWP:2 * GLA_KWP]
                groups.append(dict(
                    d=d, b=b, j=j, rows=rows, incl=incl, straddle=straddle, e_end=jnp.exp(g_end),
                    qt=_bf(q * jnp.exp(g - piv)), kt=_bf(k * jnp.exp(piv - g)),
                    qo=_bf(q * jnp.exp(jnp.minimum(g - g_cut, 0.0))),
                    ko=_bf(k * jnp.exp(jnp.minimum(g_cut - g, 0.0))),
                    qs=_bf(q * jnp.exp(g)), ks=_bf(k * jnp.exp(g_end - g)),
                    v=_bf(u_ref[b, rows, 2 * GLA_KWP:2 * GLA_KWP + GLA_VWP])))
    chains = [(p, h, slice(h * GLA_DKP, (h + 1) * GLA_DKP), slice(h * GLA_DVP, (h + 1) * GLA_DVP))
              for p in groups for h in range(GLA_HEADS)]
    yield
    same = [_dot_nt(p["qt"][:, sk], p["kt"][:, sk]) for p, _, sk, _ in chains]
    yield
    cross = [_dot_nt(p["qo"][:, sk], p["ko"][:, sk]) for p, _, sk, _ in chains]
    sc = [_bf(jnp.where(p["incl"], jnp.where(p["straddle"], x, s), 0.0))
          for s, x, (p, _, _, _) in zip(same, cross, chains)]
    yield
    intra = [_dot(sci, p["v"][:, sv]) for sci, (p, _, _, sv) in zip(sc, chains)]
    yield
    kv = [_dot_tn(p["v"][:, sv], p["ks"][:, sk]) for p, _, sk, sv in chains]
    state = {(d, b, h): st_ref[d, b, h] for d in range(2) for b in range(bsz) for h in range(GLA_HEADS)}
    outs = {}
    for j in range(nsub):
        yield
        step = [(i, ch) for i, ch in enumerate(chains) if ch[0]["j"] == j]
        s0 = [state[(p["d"], p["b"], h)] for _, (p, h, _, _) in step]
        for s, (i, (p, h, sk, _)) in zip(s0, step):
            outs[(p["d"], p["b"], j, h)] = intra[i] + _dot_nt(p["qs"][:, sk], _bf(s))
            state[(p["d"], p["b"], h)] = s * p["e_end"][:, sk] + kv[i]
    for key, s in state.items():
        st_ref[key] = s
    for p in groups:
        o_ref = ob_ref if p["d"] else of_ref
        o_ref[p["b"], p["rows"], :] = jnp.concatenate(
            [outs[(p["d"], p["b"], p["j"], h)] for h in range(GLA_HEADS)], axis=1)


def _rw_body(sf_ref, sb_ref, of_ref, ob_ref, st_ref):
    bsz = sf_ref.shape[0]
    c, w, dh = CHUNK, RW_W, RW_DH
    nsub = sf_ref.shape[1] // c
    groups = []
    for d, s_ref in enumerate((sf_ref, sb_ref)):
        incl, _ = _order_masks(c, d == 1)
        incl_b = incl.astype(BF16)
        tt = lax.broadcasted_iota(jnp.int32, (2 * c, 2 * c), 0)
        ss = lax.broadcasted_iota(jnp.int32, (2 * c, 2 * c), 1) & (c - 1)
        t_in = tt & (c - 1)
        before = (ss > t_in) if d == 1 else (ss < t_in)
        pair_mask = jnp.logical_or(before, jnp.logical_and(tt >= c, ss == t_in))
        t64 = lax.broadcasted_iota(jnp.int32, (c, c), 0)
        s64 = lax.broadcasted_iota(jnp.int32, (c, c), 1)
        eye = (t64 == s64).astype(F32)
        late, early = (s64, t64) if d == 1 else (t64, s64)
        level_masks = [
            jnp.logical_and((t64 >> (j + 1)) == (s64 >> (j + 1)),
                            jnp.logical_and(((late >> j) & 1) == 1, ((early >> j) & 1) == 0))
            for j in range(6)]
        for j in range(nsub):
            rows = pl.ds((nsub - 1 - j if d == 1 else j) * c, c)
            for b in range(bsz):
                la = s_ref[b, rows, (5 + d) * w:(6 + d) * w]
                g = _cumsum_dot(incl_b, la)
                g_end = jnp.sum(la, axis=0, keepdims=True)
                e_g = jnp.exp(g)
                e_gp = jnp.exp(g - la)
                e_inv = jnp.exp(-g_end)
                e_k = jnp.exp(g_end - g)
                e_end = jnp.exp(g_end)
                r0 = s_ref[b, rows, 0:w] * e_g
                a0 = s_ref[b, rows, 3 * w:4 * w] * e_gp
                rt, at = r0 * e_inv, a0 * e_inv
                kt = s_ref[b, rows, w:2 * w] * e_k
                bt = s_ref[b, rows, 4 * w:5 * w] * e_k
                groups.append(dict(
                    d=d, b=b, j=j, rows=rows, pair_mask=pair_mask, level_masks=level_masks, eye=eye,
                    e_end=e_end, a0=a0, r0=_bf(r0),
                    lhs=_bf(jnp.concatenate([at, rt], axis=0)),
                    rhs=_bf(jnp.concatenate([bt, kt], axis=0)),
                    v=_bf(s_ref[b, rows, 2 * w:3 * w])))

    chains = [(q, h, slice(h * dh, (h + 1) * dh)) for q in groups for h in range(RW_HEADS)]
    yield
    pair = [jnp.where(q["pair_mask"], _dot_nt(q["lhs"][:, sl], q["rhs"][:, sl]), 0.0)
            for q, _, sl in chains]
    yield
    n = [p[0:c, 0:c] for p in pair]
    x = [jnp.concatenate([_dot(_bf(p[0:c, c:2 * c]), q["v"][:, sl]), q["a0"][:, sl]], axis=1)
         for p, (q, _, sl) in zip(pair, chains)]
    tm = [jnp.where(q["level_masks"][0], ni, q["eye"]) for ni, (q, _, _) in zip(n, chains)]
    for lvl in range(1, 6):
        yield
        tb = [_bf(t) for t in tm]
        left = [_bf(_dot(tbi, _bf(jnp.where(q["level_masks"][lvl], ni, 0.0))))
                for tbi, ni, (q, _, _) in zip(tb, n, chains)]
        yield
        tm = [t + _dot(li, tbi) for t, li, tbi in zip(tm, left, tb)]
    yield
    x = [_dot(_bf(t), _bf(xi)) for t, xi in zip(tm, x)]
    mr = [jnp.concatenate([_bf(xi[:, dh:2 * dh]), q["r0"][:, sl]], axis=0) for xi, (q, _, sl) in zip(x, chains)]
    arbk = [_bf(p[c:2 * c, :]) for p in pair]
    state = {(d, b, h): st_ref[d, b, h] for d in range(2) for b in range(bsz) for h in range(RW_HEADS)}
    outs = {}
    for j in range(nsub):
        yield
        step = [(i, ch) for i, ch in enumerate(chains) if ch[0]["j"] == j]
        s0 = [state[(q["d"], q["b"], h)] for _, (q, h, _) in step]
        ms = [_dot_nt(mr[i], _bf(s)) for s, (i, _) in zip(s0, step)]
        uv = [jnp.concatenate([_bf(x[i][:, 0:dh] + m[0:c]), q["v"][:, sl]], axis=0)
              for m, (i, (q, _, sl)) in zip(ms, step)]
        for s, m, w, (i, (q, h, sl)) in zip(s0, ms, uv, step):
            outs[(q["d"], q["b"], j, h)] = m[c:2 * c] + _dot(arbk[i], w)
            state[(q["d"], q["b"], h)] = s * q["e_end"][:, sl] + _dot_tn(w, q["rhs"][:, sl])
    for key, s in state.items():
        st_ref[key] = s
    for q in groups:
        o_ref = ob_ref if q["d"] else of_ref
        o_ref[q["b"], q["rows"], :] = jnp.concatenate(
            [outs[(q["d"], q["b"], q["j"], h)] for h in range(RW_HEADS)], axis=1)


def _scan_kernel(*refs):
    ret_in, gla_in, rw_in = refs[0:10], refs[10:14], refs[14:16]
    (ret_of, ret_ob, gla_of, gla_ob, rw_of, rw_ob, ret_st, gla_st, rw_st) = refs[16:]

    @pl.when(pl.program_id(0) == 0)
    def _():
        for st in (ret_st, gla_st, rw_st):
            st[...] = jnp.zeros_like(st)

    bodies = [_rw_body(*rw_in, rw_of, rw_ob, rw_st), _gla_body(*gla_in, gla_of, gla_ob, gla_st),
              _ret_body(*ret_in, ret_of, ret_ob, ret_st)]
    while bodies:
        bodies = [b for b in bodies if next(b, StopIteration) is not StopIteration]


def _scan_call(u_ret, cos, sin, u_gla, s1, p, l):
    bsz, ta, _ = u_ret.shape
    c = SCAN_BLOCK
    ns, fwd, bwd = _chunk_maps(ta, c)
    both = lambda w_: list(_chunk_specs(bsz, ta, c, w_))
    tf, tb = (pl.BlockSpec((c, RET_W), lambda s, m=m: (m(s), 0)) for m in (fwd, bwd))
    ret_consts = _ret_consts()
    wa, ba = p["gla_wa"], p["gla_ba"]
    out_w = (RET_W, RET_W, GLA_VWP, GLA_VWP, RW_W, RW_W)
    outs = pl.pallas_call(
        _scan_kernel,
        grid=(ns,),
        in_specs=(both(URET_W) + [tf, tf, tb, tb] + [_resident_spec(a.shape) for a in ret_consts]
                  + both(UGLA_W) + [_layer_spec(wa, l), _layer_spec(ba, l)] + both(RW_S1_W)),
        out_specs=both(RET_W) + both(GLA_VWP) + both(RW_W),
        out_shape=[jax.ShapeDtypeStruct((bsz, ta, w_), F32) for w_ in out_w],
        scratch_shapes=[pltpu.VMEM((2, bsz, RET_HEADS, RET_DH, RET_DH), F32),
                        pltpu.VMEM((2, bsz, GLA_HEADS, GLA_DVP, GLA_DKP), F32),
                        pltpu.VMEM((2, bsz, RW_HEADS, RW_DH, RW_DH), F32)],
        compiler_params=_params("arbitrary"),
        name="mix_scan",
    )(u_ret, u_ret, cos, sin, cos, sin, *ret_consts, u_gla, u_gla, wa, ba, s1, s1)
    return outs[0:2], outs[2:4], outs[4:6]


def _post_kernel(c_ref, x_ref, mod_ref, modf_ref, raf_ref, rab_ref, rg_ref, gaf_ref, gab_ref, gg_ref,
                 waf_ref, wab_ref, s2_ref, rn_ref, gn_ref, lw_ref, lb_ref,
                 avg_r_ref, sum_g_ref, avg_w_ref, wo_ref, pn_ref,
                 pre_ref, wg_ref, wu_ref, wd_ref, post_ref, o_ref, x1_ref, *, cur_tile):
    d = D_MODEL

    @pl.when(jnp.logical_and(pl.program_id(0) == 0, pl.program_id(1) == 0))
    def _():
        x1_ref[...] = jnp.zeros_like(x1_ref)

    x_prev = x1_ref[...]
    o = raf_ref[...] + rab_ref[...]
    avg = avg_r_ref[...]
    mu = _group_dot(o, avg)
    oc = o - mu
    var = _group_dot(oc * oc, avg)
    ya = _silu(rg_ref[...]) * (oc * lax.rsqrt(var + 1e-5) * rn_ref[...])
    o = gaf_ref[...] + gab_ref[...]
    ms = _group_dot(o * o, sum_g_ref[...]) * (1.0 / GLA_DV)
    yb = _silu(gg_ref[...]) * (o * lax.rsqrt(ms + 1e-5) * gn_ref[...])
    o = waf_ref[...] + wab_ref[...]
    avg = avg_w_ref[...]
    mu = _group_dot(o, avg)
    oc = o - mu
    var = _group_dot(oc * oc, avg)
    yc = (oc * lax.rsqrt(var + 64e-5) * lw_ref[...] + lb_ref[...] + s2_ref[:, RW_W:2 * RW_W]) * s2_ref[:, 0:RW_W]
    y = (_dot(_bf(ya), wo_ref[0:RET_W, :]) + _dot(_bf(yb), wo_ref[RET_W:RET_W + GLA_VWP, :])
         + _dot(_bf(yc), wo_ref[RET_W + GLA_VWP:MIXP_W, :]))
    x_res = _tile_rows(cur_tile(pl.program_id(1)), c_ref, x_ref)
    x1_ref[...] = x_res + mod_ref[:, 2 * d:3 * d] * _rms(y, pn_ref[...])
    h = _rms(x_prev, pre_ref[...])
    h = _bf(h * (1.0 + modf_ref[:, 4 * d:5 * d]) + modf_ref[:, 3 * d:4 * d])
    act = _bf(_silu(_dot(h, wg_ref[...])) * _dot(h, wu_ref[...]))
    f = _dot(act, wd_ref[...])
    o_ref[...] = x_prev + modf_ref[:, 5 * d:6 * d] * _rms(f, post_ref[...])


def _post_call(ctx, x, modsel, ra, u_ret, ga, u_gla, wa, s2, p, shared, l, tile_off):
    bsz, tx, d = x.shape
    arrays, tile_specs, off = _stream_inputs(ctx, x)
    nt = tx // ROW_TILE + off - tile_off
    cur = lambda i: jnp.minimum(i, nt - 1) + tile_off
    prv = lambda i: jnp.maximum(i - 1, 0)
    row = lambda w_, cb=0: pl.BlockSpec((None, ROW_TILE, w_), lambda b, i: (b, cur(i), cb))
    kern = functools.partial(_post_kernel, cur_tile=cur)
    mod = pl.BlockSpec((None, None, None, 1, 6 * d), lambda b, i: (l, b, jnp.minimum(cur(i), 1), 0, 0))
    modf = pl.BlockSpec((None, None, None, 1, 6 * d),
                        lambda b, i: (l, b, jnp.minimum(prv(i) + tile_off, 1), 0, 0))
    spec = lambda k: _layer_spec(p[k], l) if k in p else _resident_spec(shared[k].shape)
    names = ["ret_norm", "gla_norm", "ln_w", "ln_b", "avg_r", "sum_g", "avg_w", "w_out", "norm_mix_post",
             "norm_ffn_pre", "w_gate", "w_up", "w_down", "norm_ffn_post"]
    consts = [p[k] if k in p else shared[k] for k in names]
    return pl.pallas_call(
        kern if ctx is not None else functools.partial(kern, None),
        grid=(bsz, nt + 1),
        in_specs=tile_specs(cur) + [mod, modf, row(RET_W), row(RET_W), row(RET_W, 3),
                  row(GLA_VWP), row(GLA_VWP), row(GLA_VWP, 2),
                  row(RW_W), row(RW_W), row(RW_S2_W)] + [spec(k) for k in names],
        out_specs=pl.BlockSpec((None, ROW_TILE, d), lambda b, i: (b, prv(i), 0)),
        out_shape=jax.ShapeDtypeStruct((bsz, nt * ROW_TILE, d), F32),
        scratch_shapes=[pltpu.VMEM((ROW_TILE, d), F32)],
        compiler_params=_params("arbitrary", "arbitrary"),
        name="post",
    )(*arrays, modsel, modsel, ra[0], ra[1], u_ret, ga[0], ga[1], u_gla, wa[0], wa[1], s2, *consts)


def _pad_last(a, n):
    return jnp.pad(a, [(0, 0)] * (a.ndim - 1) + [(0, n - a.shape[-1])])


def _pad_heads(a, heads, dh, dhp):
    lead = a.shape[:-1]
    a = a.reshape(lead + (heads, dh))
    return _pad_last(a, dhp).reshape(lead + (heads * dhp,))


def _arrange_rw(a):
    w = RW_W
    return jnp.concatenate([a[..., 0:3 * w + 128], _pad_last(a[..., 3 * w + 128:3 * w + 192], 128),
                            a[..., 3 * w + 192:3 * w + 320]], axis=-1)


def _arrange_in(a):
    g0 = URET_W
    kw, vw = GLA_HEADS * GLA_DK, GLA_HEADS * GLA_DV
    r0 = g0 + 2 * kw + 2 * vw + 2 * GLA_RANK
    return jnp.concatenate([
        a[..., 0:g0],
        _pad_heads(a[..., g0:g0 + kw], GLA_HEADS, GLA_DK, GLA_DKP),
        _pad_heads(a[..., g0 + kw:g0 + 2 * kw], GLA_HEADS, GLA_DK, GLA_DKP),
        _pad_heads(a[..., g0 + 2 * kw:g0 + 2 * kw + vw], GLA_HEADS, GLA_DV, GLA_DVP),
        _pad_heads(a[..., g0 + 2 * kw + vw:g0 + 2 * kw + 2 * vw], GLA_HEADS, GLA_DV, GLA_DVP),
        _pad_last(a[..., g0 + 2 * kw + 2 * vw:r0], 128),
        _arrange_rw(a[..., r0:]),
    ], axis=-1)


def _block_diag_ones(n, blk, val=1.0):
    i = np.arange(n)
    return jnp.asarray(np.where(i[:, None] // blk == i[None, :] // blk, val, 0.0), BF16)


def _rope_tables(t_lat):
    rows = t_lat // GRID_W
    row = jnp.repeat(jnp.arange(rows, dtype=F32), GRID_W)
    col = jnp.tile(jnp.arange(GRID_W, dtype=F32), rows)
    nf = RET_DH // 4
    inv = ROPE_THETA ** (-jnp.arange(nf, dtype=F32) / nf)
    ar, ac = row[:, None] * inv, col[:, None] * inv
    cos = jnp.concatenate([jnp.cos(ar), jnp.cos(ar), jnp.cos(ac), jnp.cos(ac)], axis=1)
    sin = jnp.concatenate([-jnp.sin(ar), jnp.sin(ar), -jnp.sin(ac), jnp.sin(ac)], axis=1)
    cos = jnp.concatenate([jnp.ones((CTX_LEN, RET_DH), F32), cos], axis=0)
    sin = jnp.concatenate([jnp.zeros((CTX_LEN, RET_DH), F32), sin], axis=0)
    return jnp.tile(cos, (1, RET_HEADS)), jnp.tile(sin, (1, RET_HEADS))


def kernel(x, c, ctx, c_ctx, w_mod, b_mod, norm_mix_pre, norm_mix_post, norm_ffn_pre, norm_ffn_post,
           w_in, ret_norm, gla_wa2_f, gla_ba_f, gla_wa2_b, gla_ba_b, gla_norm, rw_conv,
           rw_w0_f, rw_w2_f, rw_w0_b, rw_w2_b, rw_a0, rw_a2, rw_g2, rw_k_k, rw_k_a, rw_r_k,
           rw_ln_w, rw_ln_b, w_out, w_ffn_gate, w_ffn_up, w_ffn_down):
    bsz, t_lat, d = x.shape
    depth = w_mod.shape[0]
    assert d == D_MODEL and ctx.shape[1] == CTX_LEN and t_lat % ROW_TILE == 0 and bsz <= 7

    cc = jnp.zeros((8, d), F32).at[0:bsz].set(c).at[bsz].set(c_ctx)
    mods = _mod_call(cc, w_mod, b_mod)
    cos, sin = _rope_tables(t_lat)
    avg_r = _block_diag_ones(RET_W, RET_DH, 1.0 / RET_DH)
    avg_w = _block_diag_ones(RW_W, RW_DH, 1.0 / RW_DH)
    sum_w = _block_diag_ones(RW_W, RW_DH)
    sum_g = _block_diag_ones(GLA_VWP, GLA_DVP)

    shared = {"avg_r": avg_r, "sum_g": sum_g, "avg_w": avg_w}
    modsel = jnp.stack([jnp.broadcast_to(mods[:, bsz:bsz + 1], (depth, bsz, 6 * d)), mods[:, 0:bsz]],
                       axis=2)[:, :, :, None, :]

    rows = lambda a: a.reshape(depth, 1, -1)
    gla_heads = lambda a: _pad_heads(a, GLA_HEADS, GLA_DK, GLA_DKP)
    n_gv = GLA_HEADS * GLA_DV
    p = {
        "norm_mix_pre": rows(norm_mix_pre), "w_in": _bf(_arrange_in(w_in)),
        "conv": _arrange_rw(rw_conv), "a0": rows(rw_a0),
        "a2": _bf(jnp.pad(rw_a2, [(0, 0), (0, 64), (0, 0)])), "g2": _bf(rw_g2),
        "w0": jnp.concatenate([rw_w0_f, rw_w0_b], axis=-1)[:, None, :],
        "w2": _bf(jnp.concatenate([jnp.pad(rw_w2_f, [(0, 0), (0, 0), (0, RW_W)]),
                                   jnp.pad(rw_w2_b, [(0, 0), (0, 0), (RW_W, 0)])], axis=1)),
        "k_k": rows(rw_k_k), "k_a": rows(rw_k_a), "r_k": rows(rw_r_k),
        "gla_wa": _bf(jnp.stack([jnp.pad(gla_heads(gla_wa2_f), [(0, 0), (0, 128 - GLA_RANK), (0, 0)]),
                                 jnp.pad(gla_heads(gla_wa2_b), [(0, 0), (GLA_RANK, 128 - 2 * GLA_RANK), (0, 0)])],
                                axis=1)),
        "gla_ba": jnp.stack([gla_heads(gla_ba_f), gla_heads(gla_ba_b)], axis=1)[:, :, None, :],
        "ret_norm": rows(ret_norm),
        "gla_norm": rows(jnp.tile(_pad_last(gla_norm, GLA_DVP), (1, GLA_HEADS))),
        "ln_w": rows(rw_ln_w), "ln_b": rows(rw_ln_b),
        "w_out": _bf(jnp.concatenate([
            w_out[:, 0:RET_W],
            jnp.pad(w_out[:, RET_W:RET_W + n_gv].reshape(depth, GLA_HEADS, GLA_DV, d),
                    [(0, 0), (0, 0), (0, GLA_DVP - GLA_DV), (0, 0)]).reshape(depth, GLA_VWP, d),
            w_out[:, RET_W + n_gv:]], axis=1)),
        "norm_mix_post": rows(norm_mix_post), "norm_ffn_pre": rows(norm_ffn_pre),
        "w_gate": _bf(w_ffn_gate), "w_up": _bf(w_ffn_up), "w_down": _bf(w_ffn_down),
        "norm_ffn_post": rows(norm_ffn_post),
    }

    ctx_in, xa = ctx, x
    for l in range(depth):
        u_ret, u_gla, s1, s2 = _proj_call(ctx_in, xa, modsel, p, sum_w, l)
        ra, ga, wo = _scan_call(u_ret, cos, sin, u_gla, s1, p, l)
        tile_off = CTX_LEN // ROW_TILE if l == depth - 1 else 0
        xa = _post_call(ctx_in, xa, modsel, ra, u_ret, ga, u_gla, wo, s2, p, shared, l, tile_off)
        ctx_in = None
    return xa
```

```python
import functools

import numpy as np
import jax
import jax.numpy as jnp
from jax import lax
from jax.experimental import pallas as pl
from jax.experimental.pallas import tpu as pltpu

F32 = jnp.float32
BF16 = jnp.bfloat16

D_MODEL = 1024
CTX_LEN = 256
GRID_W = 64
ROPE_THETA = 10000.0

RET_HEADS, RET_DH = 4, 64
RET_W = RET_HEADS * RET_DH
GLA_HEADS, GLA_DK, GLA_DV = 4, 48, 96
GLA_DKP, GLA_DVP = 64, 128
GLA_KWP, GLA_VWP = GLA_HEADS * GLA_DKP, GLA_HEADS * GLA_DVP
GLA_RANK = 16
GLA_TAU = 16.0
RW_HEADS, RW_DH = 6, 64
RW_W = RW_HEADS * RW_DH
FFN_HIDDEN = 2816

URET_W = 4 * RET_W
UGLA_W = 2 * GLA_KWP + 2 * GLA_VWP + 128
URW_W = 3 * RW_W + 128 + 128 + 128
NP_IN = URET_W + UGLA_W + URW_W
MIXP_W = RET_W + GLA_VWP + RW_W

CHUNK = 64
SCAN_BLOCK = 128
RET_CHUNK = SCAN_BLOCK
ROW_TILE = 256
VMEM_LIMIT = 56 * 1024 * 1024


def _bf(x):
    return x.astype(BF16)


def _dot(a, b):
    return jnp.dot(a, b, preferred_element_type=F32)


def _dot_nt(a, b):
    return lax.dot_general(a, b, (((1,), (1,)), ((), ())), preferred_element_type=F32)


def _dot_tn(a, b):
    return lax.dot_general(a, b, (((0,), (0,)), ((), ())), preferred_element_type=F32)


def _split2(x):
    hi = _bf(x)
    return hi, _bf(x - hi.astype(F32))


def _group_dot(x, m):
    return _dot(_bf(x), m)


def _cumsum_dot(m, x):
    hi, lo = _split2(x)
    return _dot(m, hi) + _dot(m, lo)


def _dot_3pass(a, b):
    ah, al = _split2(a)
    bh, bl = _split2(b)
    return _dot(ah, bh) + (_dot(ah, bl) + _dot(al, bh))


def _sigmoid(x):
    return 1.0 / (1.0 + jnp.exp(-x))


def _silu(x):
    return x * _sigmoid(x)


def _softplus(x):
    return jnp.maximum(x, 0.0) + jnp.log(1.0 + jnp.exp(-jnp.abs(x)))


def _rms(x, g, eps=1e-6):
    return x * lax.rsqrt(jnp.mean(x * x, axis=-1, keepdims=True) + eps) * g


def _order_masks(c, reverse):
    t = lax.broadcasted_iota(jnp.int32, (c, c), 0)
    s = lax.broadcasted_iota(jnp.int32, (c, c), 1)
    if reverse:
        return s >= t, s > t
    return s <= t, s < t


def _params(*sem):
    return pltpu.CompilerParams(dimension_semantics=sem, vmem_limit_bytes=VMEM_LIMIT)


def _resident_spec(shape):
    nd = len(shape)
    return pl.BlockSpec(shape, lambda *_: (0,) * nd, pipeline_mode=pl.Buffered(1))


def _layer_spec(a, l):
    nd = a.ndim - 1
    return pl.BlockSpec((None,) + a.shape[1:], lambda *_: (l,) + (0,) * nd, pipeline_mode=pl.Buffered(1))


def _mod_kernel(c_ref, w_ref, b_ref, o_ref):
    c = c_ref[...]
    o_ref[...] = _dot_3pass(_silu(c), w_ref[...]) + b_ref[...]


def _mod_call(cc, w_mod, b_mod):
    depth, d, n = w_mod.shape
    tn = 1024
    return pl.pallas_call(
        _mod_kernel,
        grid=(depth, n // tn),
        in_specs=[
            pl.BlockSpec((8, d), lambda l, j: (0, 0)),
            pl.BlockSpec((None, d, tn), lambda l, j: (l, 0, j)),
            pl.BlockSpec((None, 1, tn), lambda l, j: (l, 0, j)),
        ],
        out_specs=pl.BlockSpec((None, 8, tn), lambda l, j: (l, 0, j)),
        out_shape=jax.ShapeDtypeStruct((depth, 8, n), F32),
        compiler_params=_params("arbitrary", "arbitrary"),
        name="mod",
    )(cc, w_mod, b_mod.reshape(depth, 1, n))


RW_S1_W = 7 * RW_W
RW_S2_W = 2 * RW_W
HALO = 16


def _tile_rows(i, c_ref, x_ref):
    return x_ref[...] if c_ref is None else jnp.where(i == 0, c_ref[...], x_ref[...])


def _proj_kernel(c_ref, x_ref, xp_ref, xn_ref, mod_ref, g_ref, w_ref, cw_ref, a0_ref, a2_ref, g2_ref, w0_ref,
                 w2_ref, kk_ref, ka_ref, rk_ref, bsum_ref, ur_ref, ug_ref, s1_ref, s2_ref, *, n_tiles):
    i = pl.program_id(1)
    d, tm, te = D_MODEL, ROW_TILE, ROW_TILE + 2 * HALO
    xe = jnp.concatenate([xp_ref[...], _tile_rows(i, c_ref, x_ref), xn_ref[...]], axis=0)
    h = _rms(xe, g_ref[...])
    h = h * (1.0 + mod_ref[:, d:2 * d]) + mod_ref[:, 0:d]
    rows = lax.broadcasted_iota(jnp.int32, (te, 1), 0)
    keep = jnp.logical_and(jnp.logical_or(rows >= HALO, i >= 2),
                           jnp.logical_or(rows < HALO + tm, jnp.logical_and(i >= 1, i < n_tiles - 1)))
    h = _bf(jnp.where(keep, h, 0.0))
    ue = _dot(h, w_ref[:, URET_W + UGLA_W:NP_IN])
    hm = h[HALO:HALO + tm]
    ur_ref[...] = _dot(hm, w_ref[:, 0:URET_W])
    ug_ref[...] = _dot(hm, w_ref[:, URET_W:URET_W + UGLA_W])

    mid = lambda a: a[HALO:HALO + tm]
    uc = (mid(pltpu.roll(ue, 1, 0)) * cw_ref[0:1, :] + mid(ue) * cw_ref[1:2, :]
          + mid(pltpu.roll(ue, te - 1, 0)) * cw_ref[2:3, :])
    w = RW_W
    r, k, v = uc[:, 0:w], uc[:, w:2 * w], uc[:, 2 * w:3 * w]
    lr_w = uc[:, 3 * w:3 * w + 128]
    lr_a = uc[:, 3 * w + 128:3 * w + 256]
    lr_g = uc[:, 3 * w + 256:3 * w + 384]
    a = _sigmoid(a0_ref[...] + _dot(_bf(lr_a), a2_ref[...]))
    gate = _dot(_bf(_sigmoid(lr_g)), g2_ref[...])
    kk = k * kk_ref[...]
    bsum = bsum_ref[...]
    kk = kk * lax.rsqrt(jnp.maximum(_group_dot(kk * kk, bsum), 1e-24))
    k2 = k * (1.0 + (a - 1.0) * ka_ref[...])
    ld = -float(np.exp(-0.5)) * _sigmoid(w0_ref[...] + _dot(_bf(jnp.tanh(lr_w)), w2_ref[...]))
    bonus = _group_dot(r * k2 * rk_ref[...], bsum) * v
    s1_ref[:, 0:w] = r
    s1_ref[:, w:2 * w] = k2
    s1_ref[:, 2 * w:3 * w] = v
    s1_ref[:, 3 * w:4 * w] = -kk
    s1_ref[:, 4 * w:5 * w] = kk * a
    s1_ref[:, 5 * w:7 * w] = ld
    s2_ref[:, 0:w] = gate
    s2_ref[:, w:2 * w] = bonus


def _stream_inputs(ctx, x):
    d = x.shape[2]
    off = 0 if ctx is None else CTX_LEN // ROW_TILE

    def specs(tile):
        x_spec = pl.BlockSpec((None, ROW_TILE, d), lambda b, i: (b, jnp.maximum(tile(i) - off, 0), 0))
        if ctx is None:
            return [x_spec]
        return [pl.BlockSpec((None, ROW_TILE, d), lambda b, i: (b, 0, 0)), x_spec]

    return ([x] if ctx is None else [ctx, x]), specs, off


def _proj_call(ctx, x, modsel, p, sum_w, l):
    bsz, tx, d = x.shape
    arrays, tile_specs, off = _stream_inputs(ctx, x)
    nt = tx // ROW_TILE + off
    per = ROW_TILE // HALO
    nh = tx // HALO
    row = lambda w_: pl.BlockSpec((None, ROW_TILE, w_), lambda b, i: (b, i, 0))
    prev = pl.BlockSpec((None, HALO, d), lambda b, i: (b, jnp.clip((i - off) * per - 1, 0, nh - 1), 0))
    nxt = pl.BlockSpec((None, HALO, d), lambda b, i: (b, jnp.clip((i - off + 1) * per, 0, nh - 1), 0))
    mod = pl.BlockSpec((None, None, None, 1, 6 * d), lambda b, i: (l, b, jnp.minimum(i, 1), 0, 0))
    consts = [p[k] for k in ("norm_mix_pre", "w_in", "conv", "a0", "a2", "g2", "w0", "w2", "k_k", "k_a", "r_k")]
    widths = (URET_W, UGLA_W, RW_S1_W, RW_S2_W)
    kern = functools.partial(_proj_kernel, n_tiles=nt)
    return pl.pallas_call(
        kern if ctx is not None else functools.partial(kern, None),
        grid=(bsz, nt),
        in_specs=(tile_specs(lambda i: i) + [prev, nxt, mod] + [_layer_spec(a, l) for a in consts]
                  + [_resident_spec(sum_w.shape)]),
        out_specs=[row(w_) for w_ in widths],
        out_shape=[jax.ShapeDtypeStruct((bsz, nt * ROW_TILE, w_), F32) for w_ in widths],
        compiler_params=_params("arbitrary", "arbitrary"),
        name="proj",
    )(*arrays, x, x, modsel, *consts, sum_w)


def _chunk_maps(ta, chunk):
    ns = ta // chunk
    nc = CTX_LEN // chunk

    def fwd(s):
        return s

    def bwd(s):
        return jnp.where(s < nc, nc - 1 - s, ns + nc - 1 - s)

    return ns, fwd, bwd


def _chunk_specs(bsz, ta, chunk, width):
    _, fwd, bwd = _chunk_maps(ta, chunk)
    return tuple(pl.BlockSpec((bsz, chunk, width), lambda s, m=m: (0, m(s), 0)) for m in (fwd, bwd))


def _rope_swap(x):
    lane = lax.broadcasted_iota(jnp.int32, x.shape, 1)
    w = x.shape[1]
    return jnp.where(lane % 32 < 16, pltpu.roll(x, w - 16, 1), pltpu.roll(x, 16, 1))


def _ret_body(uf_ref, ub_ref, cf_ref, sf_ref, cb_ref, sb_ref, dm_ref, qd_ref, kd_ref, cd_ref,
              of_ref, ob_ref, st_ref):
    bsz = uf_ref.shape[0]
    groups = []
    for d, (u_ref, cos_ref, sin_ref) in enumerate(((uf_ref, cf_ref, sf_ref), (ub_ref, cb_ref, sb_ref))):
        cos, sin = cos_ref[...], sin_ref[...]
        for b in range(bsz):
            q = u_ref[b, :, 0:RET_W]
            k = u_ref[b, :, RET_W:2 * RET_W]
            q = q * cos + _rope_swap(q) * sin
            k = (k * cos + _rope_swap(k) * sin) * (RET_DH ** -0.5)
            groups.append(dict(d=d, b=b, q=_bf(q), k=_bf(k), qd=_bf(q * qd_ref[d]), kd=_bf(k * kd_ref[d]),
                               v=_bf(u_ref[b, :, 2 * RET_W:3 * RET_W])))
    chains = [(p, h, slice(h * RET_DH, (h + 1) * RET_DH)) for p in groups for h in range(RET_HEADS)]
    yield
    sc = [_bf(_dot_nt(p["q"][:, sl], p["k"][:, sl]) * dm_ref[p["d"], h]) for p, h, sl in chains]
    yield
    s0 = [st_ref[p["d"], p["b"], h] for p, h, _ in chains]
    outs = [_dot(sci, p["v"][:, sl]) + _dot_nt(p["qd"][:, sl], _bf(s)) for sci, s, (p, _, sl) in zip(sc, s0, chains)]
    yield
    for s, (p, h, sl) in zip(s0, chains):
        st_ref[p["d"], p["b"], h] = s * cd_ref[p["d"], h] + _dot_tn(p["v"][:, sl], p["kd"][:, sl])
    for gi, p in enumerate(groups):
        o_ref = ob_ref if p["d"] else of_ref
        o_ref[p["b"]] = jnp.concatenate(outs[gi * RET_HEADS:(gi + 1) * RET_HEADS], axis=1)


def _ret_consts():
    c = RET_CHUNK
    lg = np.log1p(-np.exp2(-5.0 - np.arange(RET_HEADS, dtype=np.float64)))
    pos = np.arange(c, dtype=np.float64)
    rel = pos[:, None] - pos[None, :]
    dm = np.zeros((2, RET_HEADS, c, c))
    qd = np.zeros((2, c, RET_W))
    kd = np.zeros((2, c, RET_W))
    cd = np.zeros((2, RET_HEADS, 1, RET_DH))
    for d in range(2):
        lgd = lg if d == 0 else lg[::-1]
        p = pos if d == 0 else c - 1.0 - pos
        for h in range(RET_HEADS):
            intra = np.where(rel >= 0, np.exp(lgd[h] * np.maximum(rel, 0.0)), 0.0)
            dm[d, h] = intra if d == 0 else intra.T
            qd[d, :, h * RET_DH:(h + 1) * RET_DH] = np.exp(lgd[h] * (p + 1.0))[:, None]
            kd[d, :, h * RET_DH:(h + 1) * RET_DH] = np.exp(lgd[h] * (c - 1.0 - p))[:, None]
            cd[d, h] = np.exp(lgd[h] * c)
    return tuple(jnp.asarray(a, F32) for a in (dm, qd, kd, cd))


def _gla_body(uf_ref, ub_ref, wa_ref, ba_ref, of_ref, ob_ref, st_ref):
    bsz = uf_ref.shape[0]
    c = CHUNK
    nsub = uf_ref.shape[1] // c
    half = c // 2
    groups = []
    for d, u_ref in enumerate((uf_ref, ub_ref)):
        rev = d == 1
        incl, _ = _order_masks(c, rev)
        incl_b = incl.astype(BF16)
        pos = lambda p: c - 1 - p if rev else p
        row = lax.broadcasted_iota(jnp.int32, (c, 1), 0)
        first = (row >= half) if rev else (row < half)
        ti = lax.broadcasted_iota(jnp.int32, (c, c), 0)
        si = lax.broadcasted_iota(jnp.int32, (c, c), 1)
        if rev:
            straddle = jnp.logical_and(ti < half, si >= half)
        else:
            straddle = jnp.logical_and(ti >= half, si < half)
        for j in range(nsub):
            rows = pl.ds((nsub - 1 - j if rev else j) * c, c)
            for b in range(bsz):
                lr = _bf(u_ref[b, rows, 2 * GLA_KWP + 2 * GLA_VWP:UGLA_W])
                z = _dot(lr, wa_ref[d]) + ba_ref[d]
                la = -_softplus(-z) * (1.0 / GLA_TAU)
                g = _cumsum_dot(incl_b, la)
                g_end = jnp.sum(la, axis=0, keepdims=True)
                g_row = lambda p: g[pos(p):pos(p) + 1, :]
                piv = jnp.where(first, g_row(half // 2 - 1), g_row(half + half // 2 - 1))
                g_cut = g_row(half - 1)
                q = u_ref[b, rows, 0:GLA_KWP] * (GLA_DK ** -0.5)
                k = u_ref[b, rows, GLA_KWP:2 * GLA_KWP]
                groups.append(dict(
                    d=d, b=b, j=j, rows=rows, incl=incl, straddle=straddle, e_end=jnp.exp(g_end),
                    qt=_bf(q * jnp.exp(g - piv)), kt=_bf(k * jnp.exp(piv - g)),
                    qo=_bf(q * jnp.exp(jnp.minimum(g - g_cut, 0.0))),
                    ko=_bf(k * jnp.exp(jnp.minimum(g_cut - g, 0.0))),
                    qs=_bf(q * jnp.exp(g)), ks=_bf(k * jnp.exp(g_end - g)),
                    v=_bf(u_ref[b, rows, 2 * GLA_KWP:2 * GLA_KWP + GLA_VWP])))
    chains = [(p, h, slice(h * GLA_DKP, (h + 1) * GLA_DKP), slice(h * GLA_DVP, (h + 1) * GLA_DVP))
              for p in groups for h in range(GLA_HEADS)]
    yield
    same = [_dot_nt(p["qt"][:, sk], p["kt"][:, sk]) for p, _, sk, _ in chains]
    yield
    cross = [_dot_nt(p["qo"][:, sk], p["ko"][:, sk]) for p, _, sk, _ in chains]
    sc = [_bf(jnp.where(p["incl"], jnp.where(p["straddle"], x, s), 0.0))
          for s, x, (p, _, _, _) in zip(same, cross, chains)]
    yield
    intra = [_dot(sci, p["v"][:, sv]) for sci, (p, _, _, sv) in zip(sc, chains)]
    yield
    kv = [_dot_tn(p["v"][:, sv], p["ks"][:, sk]) for p, _, sk, sv in chains]
    state = {(d, b, h): st_ref[d, b, h] for d in range(2) for b in range(bsz) for h in range(GLA_HEADS)}
    outs = {}
    for j in range(nsub):
        yield
        step = [(i, ch) for i, ch in enumerate(chains) if ch[0]["j"] == j]
        s0 = [state[(p["d"], p["b"], h)] for _, (p, h, _, _) in step]
        for s, (i, (p, h, sk, _)) in zip(s0, step):
            outs[(p["d"], p["b"], j, h)] = intra[i] + _dot_nt(p["qs"][:, sk], _bf(s))
            state[(p["d"], p["b"], h)] = s * p["e_end"][:, sk] + kv[i]
    for key, s in state.items():
        st_ref[key] = s
    for p in groups:
        o_ref = ob_ref if p["d"] else of_ref
        o_ref[p["b"], p["rows"], :] = jnp.concatenate(
            [outs[(p["d"], p["b"], p["j"], h)] for h in range(GLA_HEADS)], axis=1)


def _rw_body(sf_ref, sb_ref, of_ref, ob_ref, st_ref):
    bsz = sf_ref.shape[0]
    c, w, dh = CHUNK, RW_W, RW_DH
    nsub = sf_ref.shape[1] // c
    groups = []
    for d, s_ref in enumerate((sf_ref, sb_ref)):
        incl, _ = _order_masks(c, d == 1)
        incl_b = incl.astype(BF16)
        tt = lax.broadcasted_iota(jnp.int32, (2 * c, 2 * c), 0)
        ss = lax.broadcasted_iota(jnp.int32, (2 * c, 2 * c), 1) & (c - 1)
        t_in = tt & (c - 1)
        before = (ss > t_in) if d == 1 else (ss < t_in)
        pair_mask = jnp.logical_or(before, jnp.logical_and(tt >= c, ss == t_in))
        t64 = lax.broadcasted_iota(jnp.int32, (c, c), 0)
        s64 = lax.broadcasted_iota(jnp.int32, (c, c), 1)
        eye = (t64 == s64).astype(F32)
        late, early = (s64, t64) if d == 1 else (t64, s64)
        level_masks = [
            jnp.logical_and((t64 >> (j + 1)) == (s64 >> (j + 1)),
                            jnp.logical_and(((late >> j) & 1) == 1, ((early >> j) & 1) == 0))
            for j in range(6)]
        for j in range(nsub):
            rows = pl.ds((nsub - 1 - j if d == 1 else j) * c, c)
            for b in range(bsz):
                la = s_ref[b, rows, (5 + d) * w:(6 + d) * w]
                g = _cumsum_dot(incl_b, la)
                g_end = jnp.sum(la, axis=0, keepdims=True)
                e_k = jnp.exp(g_end - g)
                rt = s_ref[b, rows, 0:w] * jnp.exp(g - g_end)
                at = s_ref[b, rows, 3 * w:4 * w] * jnp.exp(g - la - g_end)
                kt = s_ref[b, rows, w:2 * w] * e_k
                bt = s_ref[b, rows, 4 * w:5 * w] * e_k
                groups.append(dict(
                    d=d, b=b, j=j, rows=rows, pair_mask=pair_mask, level_masks=level_masks, eye=eye,
                    e_end=jnp.exp(g_end),
                    lhs=_bf(jnp.concatenate([at, rt], axis=0)),
                    rhs=_bf(jnp.concatenate([bt, kt], axis=0)),
                    v=_bf(s_ref[b, rows, 2 * w:3 * w])))

    chains = [(q, h, slice(h * dh, (h + 1) * dh)) for q in groups for h in range(RW_HEADS)]
    yield
    pair = [jnp.where(q["pair_mask"], _dot_nt(q["lhs"][:, sl], q["rhs"][:, sl]), 0.0)
            for q, _, sl in chains]
    yield
    arbk = [_bf(p[c:2 * c, :]) for p in pair]
    n = [_bf(p[0:c, 0:c]) for p in pair]
    x = [jnp.concatenate([_bf(_dot(_bf(p[0:c, c:2 * c]), q["v"][:, sl])), q["lhs"][0:c, sl]], axis=1)
         for p, (q, _, sl) in zip(pair, chains)]
    tm = [jnp.where(q["level_masks"][0], p[0:c, 0:c], q["eye"]) for p, (q, _, _) in zip(pair, chains)]
    del pair
    for lvl in range(1, 6):
        yield
        tb = [_bf(t) for t in tm]
        left = [_bf(_dot(tbi, jnp.where(q["level_masks"][lvl], ni, jnp.zeros_like(ni))))
                for tbi, ni, (q, _, _) in zip(tb, n, chains)]
        yield
        tm = [t + _dot(li, tbi) for t, li, tbi in zip(tm, left, tb)]
    yield
    x = [_dot(_bf(t), xi) for t, xi in zip(tm, x)]
    mr = [jnp.concatenate([_bf(xi[:, dh:2 * dh]), q["lhs"][c:2 * c, sl]], axis=0)
          for xi, (q, _, sl) in zip(x, chains)]
    state = {(d, b, h): st_ref[d, b, h] for d in range(2) for b in range(bsz) for h in range(RW_HEADS)}
    outs = {}
    for j in range(nsub):
        yield
        step = [(i, ch) for i, ch in enumerate(chains) if ch[0]["j"] == j]
        s0 = [state[(q["d"], q["b"], h)] * q["e_end"][:, sl] for _, (q, h, sl) in step]
        ms = [_dot_nt(mr[i], _bf(s)) for s, (i, _) in zip(s0, step)]
        uv = [jnp.concatenate([_bf(x[i][:, 0:dh] + m[0:c]), q["v"][:, sl]], axis=0)
              for m, (i, (q, _, sl)) in zip(ms, step)]
        for s, m, w, (i, (q, h, sl)) in zip(s0, ms, uv, step):
            outs[(q["d"], q["b"], j, h)] = m[c:2 * c] + _dot(arbk[i], w)
            state[(q["d"], q["b"], h)] = s + _dot_tn(w, q["rhs"][:, sl])
    for key, s in state.items():
        st_ref[key] = s
    for q in groups:
        o_ref = ob_ref if q["d"] else of_ref
        o_ref[q["b"], q["rows"], :] = jnp.concatenate(
            [outs[(q["d"], q["b"], q["j"], h)] for h in range(RW_HEADS)], axis=1)


def _scan_kernel(*refs):
    ret_in, gla_in, rw_in = refs[0:10], refs[10:14], refs[14:16]
    (ret_of, ret_ob, gla_of, gla_ob, rw_of, rw_ob, ret_st, gla_st, rw_st) = refs[16:]

    @pl.when(pl.program_id(0) == 0)
    def _():
        for st in (ret_st, gla_st, rw_st):
            st[...] = jnp.zeros_like(st)

    bodies = [_rw_body(*rw_in, rw_of, rw_ob, rw_st), _gla_body(*gla_in, gla_of, gla_ob, gla_st),
              _ret_body(*ret_in, ret_of, ret_ob, ret_st)]
    while bodies:
        bodies = [b for b in bodies if next(b, StopIteration) is not StopIteration]


def _scan_call(u_ret, cos, sin, u_gla, s1, p, l):
    bsz, ta, _ = u_ret.shape
    c = SCAN_BLOCK
    ns, fwd, bwd = _chunk_maps(ta, c)
    both = lambda w_: list(_chunk_specs(bsz, ta, c, w_))
    tf, tb = (pl.BlockSpec((c, RET_W), lambda s, m=m: (m(s), 0)) for m in (fwd, bwd))
    ret_consts = _ret_consts()
    wa, ba = p["gla_wa"], p["gla_ba"]
    out_w = (RET_W, RET_W, GLA_VWP, GLA_VWP, RW_W, RW_W)
    outs = pl.pallas_call(
        _scan_kernel,
        grid=(ns,),
        in_specs=(both(URET_W) + [tf, tf, tb, tb] + [_resident_spec(a.shape) for a in ret_consts]
                  + both(UGLA_W) + [_layer_spec(wa, l), _layer_spec(ba, l)] + both(RW_S1_W)),
        out_specs=both(RET_W) + both(GLA_VWP) + both(RW_W),
        out_shape=[jax.ShapeDtypeStruct((bsz, ta, w_), F32) for w_ in out_w],
        scratch_shapes=[pltpu.VMEM((2, bsz, RET_HEADS, RET_DH, RET_DH), F32),
                        pltpu.VMEM((2, bsz, GLA_HEADS, GLA_DVP, GLA_DKP), F32),
                        pltpu.VMEM((2, bsz, RW_HEADS, RW_DH, RW_DH), F32)],
        compiler_params=_params("arbitrary"),
        name="mix_scan",
    )(u_ret, u_ret, cos, sin, cos, sin, *ret_consts, u_gla, u_gla, wa, ba, s1, s1)
    return outs[0:2], outs[2:4], outs[4:6]


def _post_kernel(c_ref, x_ref, mod_ref, modf_ref, raf_ref, rab_ref, rg_ref, gaf_ref, gab_ref, gg_ref,
                 waf_ref, wab_ref, s2_ref, rn_ref, gn_ref, lw_ref, lb_ref,
                 avg_r_ref, sum_g_ref, avg_w_ref, wo_ref, pn_ref,
                 pre_ref, wg_ref, wu_ref, wd_ref, post_ref, o_ref, x1_ref, *, cur_tile):
    d = D_MODEL

    @pl.when(jnp.logical_and(pl.program_id(0) == 0, pl.program_id(1) == 0))
    def _():
        x1_ref[...] = jnp.zeros_like(x1_ref)

    x_prev = x1_ref[...]
    o = raf_ref[...] + rab_ref[...]
    avg = avg_r_ref[...]
    mu = _group_dot(o, avg)
    oc = o - mu
    var = _group_dot(oc * oc, avg)
    ya = _silu(rg_ref[...]) * (oc * lax.rsqrt(var + 1e-5) * rn_ref[...])
    o = gaf_ref[...] + gab_ref[...]
    ms = _group_dot(o * o, sum_g_ref[...]) * (1.0 / GLA_DV)
    yb = _silu(gg_ref[...]) * (o * lax.rsqrt(ms + 1e-5) * gn_ref[...])
    o = waf_ref[...] + wab_ref[...]
    avg = avg_w_ref[...]
    mu = _group_dot(o, avg)
    oc = o - mu
    var = _group_dot(oc * oc, avg)
    yc = (oc * lax.rsqrt(var + 64e-5) * lw_ref[...] + lb_ref[...] + s2_ref[:, RW_W:2 * RW_W]) * s2_ref[:, 0:RW_W]
    y = (_dot(_bf(ya), wo_ref[0:RET_W, :]) + _dot(_bf(yb), wo_ref[RET_W:RET_W + GLA_VWP, :])
         + _dot(_bf(yc), wo_ref[RET_W + GLA_VWP:MIXP_W, :]))
    x_res = _tile_rows(cur_tile(pl.program_id(1)), c_ref, x_ref)
    x1_ref[...] = x_res + mod_ref[:, 2 * d:3 * d] * _rms(y, pn_ref[...])
    h = _rms(x_prev, pre_ref[...])
    h = _bf(h * (1.0 + modf_ref[:, 4 * d:5 * d]) + modf_ref[:, 3 * d:4 * d])
    act = _bf(_silu(_dot(h, wg_ref[...])) * _dot(h, wu_ref[...]))
    f = _dot(act, wd_ref[...])
    o_ref[...] = x_prev + modf_ref[:, 5 * d:6 * d] * _rms(f, post_ref[...])


def _post_call(ctx, x, modsel, ra, u_ret, ga, u_gla, wa, s2, p, shared, l, tile_off):
    bsz, tx, d = x.shape
    arrays, tile_specs, off = _stream_inputs(ctx, x)
    nt = tx // ROW_TILE + off - tile_off
    cur = lambda i: jnp.minimum(i, nt - 1) + tile_off
    prv = lambda i: jnp.maximum(i - 1, 0)
    row = lambda w_, cb=0: pl.BlockSpec((None, ROW_TILE, w_), lambda b, i: (b, cur(i), cb))
    kern = functools.partial(_post_kernel, cur_tile=cur)
    mod = pl.BlockSpec((None, None, None, 1, 6 * d), lambda b, i: (l, b, jnp.minimum(cur(i), 1), 0, 0))
    modf = pl.BlockSpec((None, None, None, 1, 6 * d),
                        lambda b, i: (l, b, jnp.minimum(prv(i) + tile_off, 1), 0, 0))
    spec = lambda k: _layer_spec(p[k], l) if k in p else _resident_spec(shared[k].shape)
    names = ["ret_norm", "gla_norm", "ln_w", "ln_b", "avg_r", "sum_g", "avg_w", "w_out", "norm_mix_post",
             "norm_ffn_pre", "w_gate", "w_up", "w_down", "norm_ffn_post"]
    consts = [p[k] if k in p else shared[k] for k in names]
    return pl.pallas_call(
        kern if ctx is not None else functools.partial(kern, None),
        grid=(bsz, nt + 1),
        in_specs=tile_specs(cur) + [mod, modf, row(RET_W), row(RET_W), row(RET_W, 3),
                  row(GLA_VWP), row(GLA_VWP), row(GLA_VWP, 2),
                  row(RW_W), row(RW_W), row(RW_S2_W)] + [spec(k) for k in names],
        out_specs=pl.BlockSpec((None, ROW_TILE, d), lambda b, i: (b, prv(i), 0)),
        out_shape=jax.ShapeDtypeStruct((bsz, nt * ROW_TILE, d), F32),
        scratch_shapes=[pltpu.VMEM((ROW_TILE, d), F32)],
        compiler_params=_params("arbitrary", "arbitrary"),
        name="post",
    )(*arrays, modsel, modsel, ra[0], ra[1], u_ret, ga[0], ga[1], u_gla, wa[0], wa[1], s2, *consts)


def _pad_last(a, n):
    return jnp.pad(a, [(0, 0)] * (a.ndim - 1) + [(0, n - a.shape[-1])])


def _pad_heads(a, heads, dh, dhp):
    lead = a.shape[:-1]
    a = a.reshape(lead + (heads, dh))
    return _pad_last(a, dhp).reshape(lead + (heads * dhp,))


def _arrange_rw(a):
    w = RW_W
    return jnp.concatenate([a[..., 0:3 * w + 128], _pad_last(a[..., 3 * w + 128:3 * w + 192], 128),
                            a[..., 3 * w + 192:3 * w + 320]], axis=-1)


def _arrange_in(a):
    g0 = URET_W
    kw, vw = GLA_HEADS * GLA_DK, GLA_HEADS * GLA_DV
    r0 = g0 + 2 * kw + 2 * vw + 2 * GLA_RANK
    return jnp.concatenate([
        a[..., 0:g0],
        _pad_heads(a[..., g0:g0 + kw], GLA_HEADS, GLA_DK, GLA_DKP),
        _pad_heads(a[..., g0 + kw:g0 + 2 * kw], GLA_HEADS, GLA_DK, GLA_DKP),
        _pad_heads(a[..., g0 + 2 * kw:g0 + 2 * kw + vw], GLA_HEADS, GLA_DV, GLA_DVP),
        _pad_heads(a[..., g0 + 2 * kw + vw:g0 + 2 * kw + 2 * vw], GLA_HEADS, GLA_DV, GLA_DVP),
        _pad_last(a[..., g0 + 2 * kw + 2 * vw:r0], 128),
        _arrange_rw(a[..., r0:]),
    ], axis=-1)


def _block_diag_ones(n, blk, val=1.0):
    i = np.arange(n)
    return jnp.asarray(np.where(i[:, None] // blk == i[None, :] // blk, val, 0.0), BF16)


def _rope_tables(t_lat):
    rows = t_lat // GRID_W
    row = jnp.repeat(jnp.arange(rows, dtype=F32), GRID_W)
    col = jnp.tile(jnp.arange(GRID_W, dtype=F32), rows)
    nf = RET_DH // 4
    inv = ROPE_THETA ** (-jnp.arange(nf, dtype=F32) / nf)
    ar, ac = row[:, None] * inv, col[:, None] * inv
    cos = jnp.concatenate([jnp.cos(ar), jnp.cos(ar), jnp.cos(ac), jnp.cos(ac)], axis=1)
    sin = jnp.concatenate([-jnp.sin(ar), jnp.sin(ar), -jnp.sin(ac), jnp.sin(ac)], axis=1)
    cos = jnp.concatenate([jnp.ones((CTX_LEN, RET_DH), F32), cos], axis=0)
    sin = jnp.concatenate([jnp.zeros((CTX_LEN, RET_DH), F32), sin], axis=0)
    return jnp.tile(cos, (1, RET_HEADS)), jnp.tile(sin, (1, RET_HEADS))


def kernel(x, c, ctx, c_ctx, w_mod, b_mod, norm_mix_pre, norm_mix_post, norm_ffn_pre, norm_ffn_post,
           w_in, ret_norm, gla_wa2_f, gla_ba_f, gla_wa2_b, gla_ba_b, gla_norm, rw_conv,
           rw_w0_f, rw_w2_f, rw_w0_b, rw_w2_b, rw_a0, rw_a2, rw_g2, rw_k_k, rw_k_a, rw_r_k,
           rw_ln_w, rw_ln_b, w_out, w_ffn_gate, w_ffn_up, w_ffn_down):
    bsz, t_lat, d = x.shape
    depth = w_mod.shape[0]
    assert d == D_MODEL and ctx.shape[1] == CTX_LEN and t_lat % ROW_TILE == 0 and bsz <= 7

    cc = jnp.zeros((8, d), F32).at[0:bsz].set(c).at[bsz].set(c_ctx)
    mods = _mod_call(cc, w_mod, b_mod)
    cos, sin = _rope_tables(t_lat)
    avg_r = _block_diag_ones(RET_W, RET_DH, 1.0 / RET_DH)
    avg_w = _block_diag_ones(RW_W, RW_DH, 1.0 / RW_DH)
    sum_w = _block_diag_ones(RW_W, RW_DH)
    sum_g = _block_diag_ones(GLA_VWP, GLA_DVP)

    shared = {"avg_r": avg_r, "sum_g": sum_g, "avg_w": avg_w}
    modsel = jnp.stack([jnp.broadcast_to(mods[:, bsz:bsz + 1], (depth, bsz, 6 * d)), mods[:, 0:bsz]],
                       axis=2)[:, :, :, None, :]

    rows = lambda a: a.reshape(depth, 1, -1)
    gla_heads = lambda a: _pad_heads(a, GLA_HEADS, GLA_DK, GLA_DKP)
    n_gv = GLA_HEADS * GLA_DV
    p = {
        "norm_mix_pre": rows(norm_mix_pre), "w_in": _bf(_arrange_in(w_in)),
        "conv": _arrange_rw(rw_conv), "a0": rows(rw_a0),
        "a2": _bf(jnp.pad(rw_a2, [(0, 0), (0, 64), (0, 0)])), "g2": _bf(rw_g2),
        "w0": jnp.concatenate([rw_w0_f, rw_w0_b], axis=-1)[:, None, :],
        "w2": _bf(jnp.concatenate([jnp.pad(rw_w2_f, [(0, 0), (0, 0), (0, RW_W)]),
                                   jnp.pad(rw_w2_b, [(0, 0), (0, 0), (RW_W, 0)])], axis=1)),
        "k_k": rows(rw_k_k), "k_a": rows(rw_k_a), "r_k": rows(rw_r_k),
        "gla_wa": _bf(jnp.stack([jnp.pad(gla_heads(gla_wa2_f), [(0, 0), (0, 128 - GLA_RANK), (0, 0)]),
                                 jnp.pad(gla_heads(gla_wa2_b), [(0, 0), (GLA_RANK, 128 - 2 * GLA_RANK), (0, 0)])],
                                axis=1)),
        "gla_ba": jnp.stack([gla_heads(gla_ba_f), gla_heads(gla_ba_b)], axis=1)[:, :, None, :],
        "ret_norm": rows(ret_norm),
        "gla_norm": rows(jnp.tile(_pad_last(gla_norm, GLA_DVP), (1, GLA_HEADS))),
        "ln_w": rows(rw_ln_w), "ln_b": rows(rw_ln_b),
        "w_out": _bf(jnp.concatenate([
            w_out[:, 0:RET_W],
            jnp.pad(w_out[:, RET_W:RET_W + n_gv].reshape(depth, GLA_HEADS, GLA_DV, d),
                    [(0, 0), (0, 0), (0, GLA_DVP - GLA_DV), (0, 0)]).reshape(depth, GLA_VWP, d),
            w_out[:, RET_W + n_gv:]], axis=1)),
        "norm_mix_post": rows(norm_mix_post), "norm_ffn_pre": rows(norm_ffn_pre),
        "w_gate": _bf(w_ffn_gate), "w_up": _bf(w_ffn_up), "w_down": _bf(w_ffn_down),
        "norm_ffn_post": rows(norm_ffn_post),
    }

    ctx_in, xa = ctx, x
    for l in range(depth):
        u_ret, u_gla, s1, s2 = _proj_call(ctx_in, xa, modsel, p, sum_w, l)
        ra, ga, wo = _scan_call(u_ret, cos, sin, u_gla, s1, p, l)
        tile_off = CTX_LEN // ROW_TILE if l == depth - 1 else 0
        xa = _post_call(ctx_in, xa, modsel, ra, u_ret, ga, u_gla, wo, s2, p, shared, l, tile_off)
        ctx_in = None
    return xa
```

```python
import functools

import numpy as np
import jax
import jax.numpy as jnp
from jax import lax
from jax.experimental import pallas as pl
from jax.experimental.pallas import tpu as pltpu

F32 = jnp.float32
BF16 = jnp.bfloat16

D_MODEL = 1024
CTX_LEN = 256
GRID_W = 64
ROPE_THETA = 10000.0

RET_HEADS, RET_DH = 4, 64
RET_W = RET_HEADS * RET_DH
GLA_HEADS, GLA_DK, GLA_DV = 4, 48, 96
GLA_DKP, GLA_DVP = 64, 128
GLA_KWP, GLA_VWP = GLA_HEADS * GLA_DKP, GLA_HEADS * GLA_DVP
GLA_RANK = 16
GLA_TAU = 16.0
RW_HEADS, RW_DH = 6, 64
RW_W = RW_HEADS * RW_DH
FFN_HIDDEN = 2816

URET_W = 4 * RET_W
UGLA_W = 2 * GLA_KWP + 2 * GLA_VWP + 128
URW_W = 3 * RW_W + 128 + 128 + 128
NP_IN = URET_W + UGLA_W + URW_W
MIXP_W = RET_W + GLA_VWP + RW_W

CHUNK = 64
SCAN_BLOCK = 128
RET_CHUNK = SCAN_BLOCK
ROW_TILE = 256
VMEM_LIMIT = 56 * 1024 * 1024


def _bf(x):
    return x.astype(BF16)


def _dot(a, b):
    return jnp.dot(a, b, preferred_element_type=F32)


def _dot_nt(a, b):
    return lax.dot_general(a, b, (((1,), (1,)), ((), ())), preferred_element_type=F32)


def _dot_tn(a, b):
    return lax.dot_general(a, b, (((0,), (0,)), ((), ())), preferred_element_type=F32)


def _split2(x):
    hi = _bf(x)
    return hi, _bf(x - hi.astype(F32))


def _group_dot(x, m):
    return _dot(_bf(x), m)


def _cumsum_dot(m, x):
    hi, lo = _split2(x)
    return _dot(m, hi) + _dot(m, lo)


def _dot_3pass(a, b):
    ah, al = _split2(a)
    bh, bl = _split2(b)
    return _dot(ah, bh) + (_dot(ah, bl) + _dot(al, bh))


def _sigmoid(x):
    return 1.0 / (1.0 + jnp.exp(-x))


def _silu(x):
    return x * _sigmoid(x)


def _softplus(x):
    return jnp.maximum(x, 0.0) + jnp.log(1.0 + jnp.exp(-jnp.abs(x)))


def _rms(x, g, eps=1e-6):
    return x * lax.rsqrt(jnp.mean(x * x, axis=-1, keepdims=True) + eps) * g


def _order_masks(c, reverse):
    t = lax.broadcasted_iota(jnp.int32, (c, c), 0)
    s = lax.broadcasted_iota(jnp.int32, (c, c), 1)
    if reverse:
        return s >= t, s > t
    return s <= t, s < t


def _params(*sem):
    return pltpu.CompilerParams(dimension_semantics=sem, vmem_limit_bytes=VMEM_LIMIT)


def _resident_spec(shape):
    nd = len(shape)
    return pl.BlockSpec(shape, lambda *_: (0,) * nd, pipeline_mode=pl.Buffered(1))


def _layer_spec(a, l):
    nd = a.ndim - 1
    return pl.BlockSpec((None,) + a.shape[1:], lambda *_: (l,) + (0,) * nd, pipeline_mode=pl.Buffered(1))


def _mod_kernel(c_ref, w_ref, b_ref, o_ref):
    c = c_ref[...]
    o_ref[...] = _dot_3pass(_silu(c), w_ref[...]) + b_ref[...]


def _mod_call(cc, w_mod, b_mod):
    depth, d, n = w_mod.shape
    tn = 1024
    return pl.pallas_call(
        _mod_kernel,
        grid=(depth, n // tn),
        in_specs=[
            pl.BlockSpec((8, d), lambda l, j: (0, 0)),
            pl.BlockSpec((None, d, tn), lambda l, j: (l, 0, j)),
            pl.BlockSpec((None, 1, tn), lambda l, j: (l, 0, j)),
        ],
        out_specs=pl.BlockSpec((None, 8, tn), lambda l, j: (l, 0, j)),
        out_shape=jax.ShapeDtypeStruct((depth, 8, n), F32),
        compiler_params=_params("arbitrary", "arbitrary"),
        name="mod",
    )(cc, w_mod, b_mod.reshape(depth, 1, n))


RW_S1_W = 7 * RW_W
RW_S2_W = 2 * RW_W
HALO = 16


def _tile_rows(i, c_ref, x_ref):
    return x_ref[...] if c_ref is None else jnp.where(i == 0, c_ref[...], x_ref[...])


def _proj_kernel(c_ref, x_ref, xp_ref, xn_ref, mod_ref, g_ref, w_ref, cw_ref, a0_ref, a2_ref, g2_ref, w0_ref,
                 w2_ref, kk_ref, ka_ref, rk_ref, bsum_ref, ur_ref, ug_ref, s1_ref, s2_ref, *, n_tiles):
    i = pl.program_id(1)
    d, tm, te = D_MODEL, ROW_TILE, ROW_TILE + 2 * HALO
    xe = jnp.concatenate([xp_ref[...], _tile_rows(i, c_ref, x_ref), xn_ref[...]], axis=0)
    h = _rms(xe, g_ref[...])
    h = h * (1.0 + mod_ref[:, d:2 * d]) + mod_ref[:, 0:d]
    rows = lax.broadcasted_iota(jnp.int32, (te, 1), 0)
    keep = jnp.logical_and(jnp.logical_or(rows >= HALO, i >= 2),
                           jnp.logical_or(rows < HALO + tm, jnp.logical_and(i >= 1, i < n_tiles - 1)))
    h = _bf(jnp.where(keep, h, 0.0))
    ue = _dot(h, w_ref[:, URET_W + UGLA_W:NP_IN])
    hm = h[HALO:HALO + tm]
    ur_ref[...] = _dot(hm, w_ref[:, 0:URET_W])
    ug_ref[...] = _dot(hm, w_ref[:, URET_W:URET_W + UGLA_W])

    mid = lambda a: a[HALO:HALO + tm]
    uc = (mid(pltpu.roll(ue, 1, 0)) * cw_ref[0:1, :] + mid(ue) * cw_ref[1:2, :]
          + mid(pltpu.roll(ue, te - 1, 0)) * cw_ref[2:3, :])
    w = RW_W
    r, k, v = uc[:, 0:w], uc[:, w:2 * w], uc[:, 2 * w:3 * w]
    lr_w = uc[:, 3 * w:3 * w + 128]
    lr_a = uc[:, 3 * w + 128:3 * w + 256]
    lr_g = uc[:, 3 * w + 256:3 * w + 384]
    a = _sigmoid(a0_ref[...] + _dot(_bf(lr_a), a2_ref[...]))
    gate = _dot(_bf(_sigmoid(lr_g)), g2_ref[...])
    kk = k * kk_ref[...]
    bsum = bsum_ref[...]
    kk = kk * lax.rsqrt(jnp.maximum(_group_dot(kk * kk, bsum), 1e-24))
    k2 = k * (1.0 + (a - 1.0) * ka_ref[...])
    ld = -float(np.exp(-0.5)) * _sigmoid(w0_ref[...] + _dot(_bf(jnp.tanh(lr_w)), w2_ref[...]))
    bonus = _group_dot(r * k2 * rk_ref[...], bsum) * v
    s1_ref[:, 0:w] = r
    s1_ref[:, w:2 * w] = k2
    s1_ref[:, 2 * w:3 * w] = v
    s1_ref[:, 3 * w:4 * w] = -kk
    s1_ref[:, 4 * w:5 * w] = kk * a
    s1_ref[:, 5 * w:7 * w] = ld
    s2_ref[:, 0:w] = gate
    s2_ref[:, w:2 * w] = bonus


def _stream_inputs(ctx, x):
    d = x.shape[2]
    off = 0 if ctx is None else CTX_LEN // ROW_TILE

    def specs(tile):
        x_spec = pl.BlockSpec((None, ROW_TILE, d), lambda b, i: (b, jnp.maximum(tile(i) - off, 0), 0))
        if ctx is None:
            return [x_spec]
        return [pl.BlockSpec((None, ROW_TILE, d), lambda b, i: (b, 0, 0)), x_spec]

    return ([x] if ctx is None else [ctx, x]), specs, off


def _proj_call(ctx, x, modsel, p, sum_w, l):
    bsz, tx, d = x.shape
    arrays, tile_specs, off = _stream_inputs(ctx, x)
    nt = tx // ROW_TILE + off
    per = ROW_TILE // HALO
    nh = tx // HALO
    row = lambda w_: pl.BlockSpec((None, ROW_TILE, w_), lambda b, i: (b, i, 0))
    prev = pl.BlockSpec((None, HALO, d), lambda b, i: (b, jnp.clip((i - off) * per - 1, 0, nh - 1), 0))
    nxt = pl.BlockSpec((None, HALO, d), lambda b, i: (b, jnp.clip((i - off + 1) * per, 0, nh - 1), 0))
    mod = pl.BlockSpec((None, None, None, 1, 6 * d), lambda b, i: (l, b, jnp.minimum(i, 1), 0, 0))
    consts = [p[k] for k in ("norm_mix_pre", "w_in", "conv", "a0", "a2", "g2", "w0", "w2", "k_k", "k_a", "r_k")]
    widths = (URET_W, UGLA_W, RW_S1_W, RW_S2_W)
    kern = functools.partial(_proj_kernel, n_tiles=nt)
    return pl.pallas_call(
        kern if ctx is not None else functools.partial(kern, None),
        grid=(bsz, nt),
        in_specs=(tile_specs(lambda i: i) + [prev, nxt, mod] + [_layer_spec(a, l) for a in consts]
                  + [_resident_spec(sum_w.shape)]),
        out_specs=[row(w_) for w_ in widths],
        out_shape=[jax.ShapeDtypeStruct((bsz, nt * ROW_TILE, w_), F32) for w_ in widths],
        compiler_params=_params("arbitrary", "arbitrary"),
        name="proj",
    )(*arrays, x, x, modsel, *consts, sum_w)


def _chunk_maps(ta, chunk):
    ns = ta // chunk
    nc = CTX_LEN // chunk

    def fwd(s):
        return s

    def bwd(s):
        return jnp.where(s < nc, nc - 1 - s, ns + nc - 1 - s)

    return ns, fwd, bwd


def _chunk_specs(bsz, ta, chunk, width):
    _, fwd, bwd = _chunk_maps(ta, chunk)
    return tuple(pl.BlockSpec((bsz, chunk, width), lambda s, m=m: (0, m(s), 0)) for m in (fwd, bwd))


def _rope_swap(x):
    lane = lax.broadcasted_iota(jnp.int32, x.shape, 1)
    w = x.shape[1]
    return jnp.where(lane % 32 < 16, pltpu.roll(x, w - 16, 1), pltpu.roll(x, 16, 1))


def _ret_body(uf_ref, ub_ref, cf_ref, sf_ref, cb_ref, sb_ref, dm_ref, qd_ref, kd_ref, cd_ref,
              of_ref, ob_ref, st_ref):
    bsz = uf_ref.shape[0]
    groups = []
    for d, (u_ref, cos_ref, sin_ref) in enumerate(((uf_ref, cf_ref, sf_ref), (ub_ref, cb_ref, sb_ref))):
        cos, sin = cos_ref[...], sin_ref[...]
        for b in range(bsz):
            q = u_ref[b, :, 0:RET_W]
            k = u_ref[b, :, RET_W:2 * RET_W]
            q = q * cos + _rope_swap(q) * sin
            k = (k * cos + _rope_swap(k) * sin) * (RET_DH ** -0.5)
            groups.append(dict(d=d, b=b, q=_bf(q), k=_bf(k), qd=_bf(q * qd_ref[d]), kd=_bf(k * kd_ref[d]),
                               v=_bf(u_ref[b, :, 2 * RET_W:3 * RET_W])))
    chains = [(p, h, slice(h * RET_DH, (h + 1) * RET_DH)) for p in groups for h in range(RET_HEADS)]
    yield
    sc = [_bf(_dot_nt(p["q"][:, sl], p["k"][:, sl]) * dm_ref[p["d"], h]) for p, h, sl in chains]
    yield
    s0 = [st_ref[p["d"], p["b"], h] for p, h, _ in chains]
    outs = [_dot(sci, p["v"][:, sl]) + _dot_nt(p["qd"][:, sl], _bf(s)) for sci, s, (p, _, sl) in zip(sc, s0, chains)]
    yield
    for s, (p, h, sl) in zip(s0, chains):
        st_ref[p["d"], p["b"], h] = s * cd_ref[p["d"], h] + _dot_tn(p["v"][:, sl], p["kd"][:, sl])
    for gi, p in enumerate(groups):
        o_ref = ob_ref if p["d"] else of_ref
        o_ref[p["b"]] = jnp.concatenate(outs[gi * RET_HEADS:(gi + 1) * RET_HEADS], axis=1)


def _ret_consts():
    c = RET_CHUNK
    lg = np.log1p(-np.exp2(-5.0 - np.arange(RET_HEADS, dtype=np.float64)))
    pos = np.arange(c, dtype=np.float64)
    rel = pos[:, None] - pos[None, :]
    dm = np.zeros((2, RET_HEADS, c, c))
    qd = np.zeros((2, c, RET_W))
    kd = np.zeros((2, c, RET_W))
    cd = np.zeros((2, RET_HEADS, 1, RET_DH))
    for d in range(2):
        lgd = lg if d == 0 else lg[::-1]
        p = pos if d == 0 else c - 1.0 - pos
        for h in range(RET_HEADS):
            intra = np.where(rel >= 0, np.exp(lgd[h] * np.maximum(rel, 0.0)), 0.0)
            dm[d, h] = intra if d == 0 else intra.T
            qd[d, :, h * RET_DH:(h + 1) * RET_DH] = np.exp(lgd[h] * (p + 1.0))[:, None]
            kd[d, :, h * RET_DH:(h + 1) * RET_DH] = np.exp(lgd[h] * (c - 1.0 - p))[:, None]
            cd[d, h] = np.exp(lgd[h] * c)
    return tuple(jnp.asarray(a, F32) for a in (dm, qd, kd, cd))


def _gla_body(uf_ref, lf_ref, ub_ref, lb_ref, wa_ref, ba_ref, of_ref, ob_ref, st_ref):
    bsz = uf_ref.shape[0]
    c = CHUNK
    nsub = uf_ref.shape[1] // c
    half = c // 2
    groups = []
    for d, (u_ref, l_ref) in enumerate(((uf_ref, lf_ref), (ub_ref, lb_ref))):
        rev = d == 1
        incl, _ = _order_masks(c, rev)
        incl_b = incl.astype(BF16)
        pos = lambda p: c - 1 - p if rev else p
        row = lax.broadcasted_iota(jnp.int32, (c, 1), 0)
        first = (row >= half) if rev else (row < half)
        ti = lax.broadcasted_iota(jnp.int32, (c, c), 0)
        si = lax.broadcasted_iota(jnp.int32, (c, c), 1)
        if rev:
            straddle = jnp.logical_and(ti < half, si >= half)
        else:
            straddle = jnp.logical_and(ti >= half, si < half)
        for j in range(nsub):
            rows = pl.ds((nsub - 1 - j if rev else j) * c, c)
            for b in range(bsz):
                lr = _bf(l_ref[b, rows, :])
                z = _dot(lr, wa_ref[d]) + ba_ref[d]
                la = -_softplus(-z) * (1.0 / GLA_TAU)
                g = _cumsum_dot(incl_b, la)
                g_end = jnp.sum(la, axis=0, keepdims=True)
                g_row = lambda p: g[pos(p):pos(p) + 1, :]
                piv = jnp.where(first, g_row(half // 2 - 1), g_row(half + half // 2 - 1))
                g_cut = g_row(half - 1)
                q = u_ref[b, rows, 0:GLA_KWP] * (GLA_DK ** -0.5)
                k = u_ref[b, rows, GLA_KWP:2 * GLA_KWP]
                groups.append(dict(
                    d=d, b=b, j=j, rows=rows, incl=incl, straddle=straddle, e_end=jnp.exp(g_end),
                    qt=_bf(q * jnp.exp(g - piv)), kt=_bf(k * jnp.exp(piv - g)),
                    qo=_bf(q * jnp.exp(jnp.minimum(g - g_cut, 0.0))),
                    ko=_bf(k * jnp.exp(jnp.minimum(g_cut - g, 0.0))),
                    qs=_bf(q * jnp.exp(g)), ks=_bf(k * jnp.exp(g_end - g)),
                    v=_bf(u_ref[b, rows, 2 * GLA_KWP:2 * GLA_KWP + GLA_VWP])))
    chains = [(p, h, slice(h * GLA_DKP, (h + 1) * GLA_DKP), slice(h * GLA_DVP, (h + 1) * GLA_DVP))
              for p in groups for h in range(GLA_HEADS)]
    yield
    same = [_dot_nt(p["qt"][:, sk], p["kt"][:, sk]) for p, _, sk, _ in chains]
    yield
    cross = [_dot_nt(p["qo"][:, sk], p["ko"][:, sk]) for p, _, sk, _ in chains]
    sc = [_bf(jnp.where(p["incl"], jnp.where(p["straddle"], x, s), 0.0))
          for s, x, (p, _, _, _) in zip(same, cross, chains)]
    yield
    intra = [_dot(sci, p["v"][:, sv]) for sci, (p, _, _, sv) in zip(sc, chains)]
    yield
    kv = [_dot_tn(p["v"][:, sv], p["ks"][:, sk]) for p, _, sk, sv in chains]
    state = {(d, b, h): st_ref[d, b, h] for d in range(2) for b in range(bsz) for h in range(GLA_HEADS)}
    outs = {}
    for j in range(nsub):
        yield
        step = [(i, ch) for i, ch in enumerate(chains) if ch[0]["j"] == j]
        s0 = [state[(p["d"], p["b"], h)] for _, (p, h, _, _) in step]
        for s, (i, (p, h, sk, _)) in zip(s0, step):
            outs[(p["d"], p["b"], j, h)] = intra[i] + _dot_nt(p["qs"][:, sk], _bf(s))
            state[(p["d"], p["b"], h)] = s * p["e_end"][:, sk] + kv[i]
    for key, s in state.items():
        st_ref[key] = s
    for p in groups:
        o_ref = ob_ref if p["d"] else of_ref
        o_ref[p["b"], p["rows"], :] = jnp.concatenate(
            [outs[(p["d"], p["b"], p["j"], h)] for h in range(GLA_HEADS)], axis=1)


def _rw_body(sf_ref, lf_ref, sb_ref, lb_ref, of_ref, ob_ref, st_ref):
    bsz = sf_ref.shape[0]
    c, w, dh = CHUNK, RW_W, RW_DH
    nsub = sf_ref.shape[1] // c
    groups = []
    for d, (s_ref, l_ref) in enumerate(((sf_ref, lf_ref), (sb_ref, lb_ref))):
        incl, _ = _order_masks(c, d == 1)
        incl_b = incl.astype(BF16)
        tt = lax.broadcasted_iota(jnp.int32, (2 * c, 2 * c), 0)
        ss = lax.broadcasted_iota(jnp.int32, (2 * c, 2 * c), 1) & (c - 1)
        t_in = tt & (c - 1)
        before = (ss > t_in) if d == 1 else (ss < t_in)
        pair_mask = jnp.logical_or(before, jnp.logical_and(tt >= c, ss == t_in))
        t64 = lax.broadcasted_iota(jnp.int32, (c, c), 0)
        s64 = lax.broadcasted_iota(jnp.int32, (c, c), 1)
        eye = (t64 == s64).astype(F32)
        late, early = (s64, t64) if d == 1 else (t64, s64)
        level_masks = [
            jnp.logical_and((t64 >> (j + 1)) == (s64 >> (j + 1)),
                            jnp.logical_and(((late >> j) & 1) == 1, ((early >> j) & 1) == 0))
            for j in range(6)]
        for j in range(nsub):
            rows = pl.ds((nsub - 1 - j if d == 1 else j) * c, c)
            for b in range(bsz):
                la = l_ref[b, rows, :]
                g = _cumsum_dot(incl_b, la)
                g_end = jnp.sum(la, axis=0, keepdims=True)
                e_k = jnp.exp(g_end - g)
                rt = s_ref[b, rows, 0:w] * jnp.exp(g - g_end)
                at = s_ref[b, rows, 3 * w:4 * w] * jnp.exp(g - la - g_end)
                kt = s_ref[b, rows, w:2 * w] * e_k
                bt = s_ref[b, rows, 4 * w:5 * w] * e_k
                groups.append(dict(
                    d=d, b=b, j=j, rows=rows, pair_mask=pair_mask, level_masks=level_masks, eye=eye,
                    e_end=jnp.exp(g_end),
                    lhs=_bf(jnp.concatenate([at, rt], axis=0)),
                    rhs=_bf(jnp.concatenate([bt, kt], axis=0)),
                    v=_bf(s_ref[b, rows, 2 * w:3 * w])))

    chains = [(q, h, slice(h * dh, (h + 1) * dh)) for q in groups for h in range(RW_HEADS)]
    yield
    pair = [jnp.where(q["pair_mask"], _dot_nt(q["lhs"][:, sl], q["rhs"][:, sl]), 0.0)
            for q, _, sl in chains]
    yield
    arbk = [_bf(p[c:2 * c, :]) for p in pair]
    n = [_bf(p[0:c, 0:c]) for p in pair]
    x = [jnp.concatenate([_bf(_dot(_bf(p[0:c, c:2 * c]), q["v"][:, sl])), q["lhs"][0:c, sl]], axis=1)
         for p, (q, _, sl) in zip(pair, chains)]
    tm = [jnp.where(q["level_masks"][0], p[0:c, 0:c], q["eye"]) for p, (q, _, _) in zip(pair, chains)]
    del pair
    for lvl in range(1, 6):
        yield
        tb = [_bf(t) for t in tm]
        left = [_bf(_dot(tbi, jnp.where(q["level_masks"][lvl], ni, jnp.zeros_like(ni))))
                for tbi, ni, (q, _, _) in zip(tb, n, chains)]
        yield
        tm = [t + _dot(li, tbi) for t, li, tbi in zip(tm, left, tb)]
    yield
    x = [_dot(_bf(t), xi) for t, xi in zip(tm, x)]
    mr = [jnp.concatenate([_bf(xi[:, dh:2 * dh]), q["lhs"][c:2 * c, sl]], axis=0)
          for xi, (q, _, sl) in zip(x, chains)]
    state = {(d, b, h): st_ref[d, b, h] for d in range(2) for b in range(bsz) for h in range(RW_HEADS)}
    outs = {}
    for j in range(nsub):
        yield
        step = [(i, ch) for i, ch in enumerate(chains) if ch[0]["j"] == j]
        s0 = [state[(q["d"], q["b"], h)] * q["e_end"][:, sl] for _, (q, h, sl) in step]
        ms = [_dot_nt(mr[i], _bf(s)) for s, (i, _) in zip(s0, step)]
        uv = [jnp.concatenate([_bf(x[i][:, 0:dh] + m[0:c]), q["v"][:, sl]], axis=0)
              for m, (i, (q, _, sl)) in zip(ms, step)]
        for s, m, w, (i, (q, h, sl)) in zip(s0, ms, uv, step):
            outs[(q["d"], q["b"], j, h)] = m[c:2 * c] + _dot(arbk[i], w)
            state[(q["d"], q["b"], h)] = s + _dot_tn(w, q["rhs"][:, sl])
    for key, s in state.items():
        st_ref[key] = s
    for q in groups:
        o_ref = ob_ref if q["d"] else of_ref
        o_ref[q["b"], q["rows"], :] = jnp.concatenate(
            [outs[(q["d"], q["b"], q["j"], h)] for h in range(RW_HEADS)], axis=1)


def _scan_kernel(*refs):
    ret_in, gla_in, rw_in = refs[0:10], refs[10:16], refs[16:20]
    (ret_of, ret_ob, gla_of, gla_ob, rw_of, rw_ob, ret_st, gla_st, rw_st) = refs[20:]

    @pl.when(pl.program_id(0) == 0)
    def _():
        for st in (ret_st, gla_st, rw_st):
            st[...] = jnp.zeros_like(st)

    bodies = [_rw_body(*rw_in, rw_of, rw_ob, rw_st), _gla_body(*gla_in, gla_of, gla_ob, gla_st),
              _ret_body(*ret_in, ret_of, ret_ob, ret_st)]
    while bodies:
        bodies = [b for b in bodies if next(b, StopIteration) is not StopIteration]


def _scan_call(u_ret, cos, sin, u_gla, s1, p, l):
    bsz, ta, _ = u_ret.shape
    c = SCAN_BLOCK
    ns, fwd, bwd = _chunk_maps(ta, c)
    both = lambda w_: list(_chunk_specs(bsz, ta, c, w_))
    cols = lambda w_, cb, m: pl.BlockSpec((bsz, c, w_), lambda s: (0, m(s), cb))
    tf, tb = (pl.BlockSpec((c, RET_W), lambda s, m=m: (m(s), 0)) for m in (fwd, bwd))
    ret_consts = _ret_consts()
    wa, ba = p["gla_wa"], p["gla_ba"]
    out_w = (RET_W, RET_W, GLA_VWP, GLA_VWP, RW_W, RW_W)
    gla_main, gla_lr = 2 * GLA_KWP + GLA_VWP, UGLA_W - 2 * GLA_KWP - 2 * GLA_VWP
    outs = pl.pallas_call(
        _scan_kernel,
        grid=(ns,),
        in_specs=(both(3 * RET_W) + [tf, tf, tb, tb] + [_resident_spec(a.shape) for a in ret_consts]
                  + [cols(gla_main, 0, fwd), cols(gla_lr, (UGLA_W - gla_lr) // gla_lr, fwd),
                     cols(gla_main, 0, bwd), cols(gla_lr, (UGLA_W - gla_lr) // gla_lr, bwd),
                     _layer_spec(wa, l), _layer_spec(ba, l)]
                  + [cols(5 * RW_W, 0, fwd), cols(RW_W, 5, fwd), cols(5 * RW_W, 0, bwd), cols(RW_W, 6, bwd)]),
        out_specs=both(RET_W) + both(GLA_VWP) + both(RW_W),
        out_shape=[jax.ShapeDtypeStruct((bsz, ta, w_), F32) for w_ in out_w],
        scratch_shapes=[pltpu.VMEM((2, bsz, RET_HEADS, RET_DH, RET_DH), F32),
                        pltpu.VMEM((2, bsz, GLA_HEADS, GLA_DVP, GLA_DKP), F32),
                        pltpu.VMEM((2, bsz, RW_HEADS, RW_DH, RW_DH), F32)],
        compiler_params=_params("arbitrary"),
        name="mix_scan",
    )(u_ret, u_ret, cos, sin, cos, sin, *ret_consts, u_gla, u_gla, u_gla, u_gla, wa, ba, s1, s1, s1, s1)
    return outs[0:2], outs[2:4], outs[4:6]


def _post_kernel(c_ref, x_ref, mod_ref, modf_ref, raf_ref, rab_ref, rg_ref, gaf_ref, gab_ref, gg_ref,
                 waf_ref, wab_ref, s2_ref, rn_ref, gn_ref, lw_ref, lb_ref,
                 avg_r_ref, sum_g_ref, avg_w_ref, wo_ref, pn_ref,
                 pre_ref, wg_ref, wu_ref, wd_ref, post_ref, o_ref, x1_ref, *, cur_tile):
    d = D_MODEL

    @pl.when(jnp.logical_and(pl.program_id(0) == 0, pl.program_id(1) == 0))
    def _():
        x1_ref[...] = jnp.zeros_like(x1_ref)

    x_prev = x1_ref[...]
    o = raf_ref[...] + rab_ref[...]
    avg = avg_r_ref[...]
    mu = _group_dot(o, avg)
    oc = o - mu
    var = _group_dot(oc * oc, avg)
    ya = _silu(rg_ref[...]) * (oc * lax.rsqrt(var + 1e-5) * rn_ref[...])
    o = gaf_ref[...] + gab_ref[...]
    ms = _group_dot(o * o, sum_g_ref[...]) * (1.0 / GLA_DV)
    yb = _silu(gg_ref[...]) * (o * lax.rsqrt(ms + 1e-5) * gn_ref[...])
    o = waf_ref[...] + wab_ref[...]
    avg = avg_w_ref[...]
    mu = _group_dot(o, avg)
    oc = o - mu
    var = _group_dot(oc * oc, avg)
    yc = (oc * lax.rsqrt(var + 64e-5) * lw_ref[...] + lb_ref[...] + s2_ref[:, RW_W:2 * RW_W]) * s2_ref[:, 0:RW_W]
    y = (_dot(_bf(ya), wo_ref[0:RET_W, :]) + _dot(_bf(yb), wo_ref[RET_W:RET_W + GLA_VWP, :])
         + _dot(_bf(yc), wo_ref[RET_W + GLA_VWP:MIXP_W, :]))
    x_res = _tile_rows(cur_tile(pl.program_id(1)), c_ref, x_ref)
    x1_ref[...] = x_res + mod_ref[:, 2 * d:3 * d] * _rms(y, pn_ref[...])
    h = _rms(x_prev, pre_ref[...])
    h = _bf(h * (1.0 + modf_ref[:, 4 * d:5 * d]) + modf_ref[:, 3 * d:4 * d])
    act = _bf(_silu(_dot(h, wg_ref[...])) * _dot(h, wu_ref[...]))
    f = _dot(act, wd_ref[...])
    o_ref[...] = x_prev + modf_ref[:, 5 * d:6 * d] * _rms(f, post_ref[...])


def _post_call(ctx, x, modsel, ra, u_ret, ga, u_gla, wa, s2, p, shared, l, tile_off):
    bsz, tx, d = x.shape
    arrays, tile_specs, off = _stream_inputs(ctx, x)
    nt = tx // ROW_TILE + off - tile_off
    cur = lambda i: jnp.minimum(i, nt - 1) + tile_off
    prv = lambda i: jnp.maximum(i - 1, 0)
    row = lambda w_, cb=0: pl.BlockSpec((None, ROW_TILE, w_), lambda b, i: (b, cur(i), cb))
    kern = functools.partial(_post_kernel, cur_tile=cur)
    mod = pl.BlockSpec((None, None, None, 1, 6 * d), lambda b, i: (l, b, jnp.minimum(cur(i), 1), 0, 0))
    modf = pl.BlockSpec((None, None, None, 1, 6 * d),
                        lambda b, i: (l, b, jnp.minimum(prv(i) + tile_off, 1), 0, 0))
    spec = lambda k: _layer_spec(p[k], l) if k in p else _resident_spec(shared[k].shape)
    names = ["ret_norm", "gla_norm", "ln_w", "ln_b", "avg_r", "sum_g", "avg_w", "w_out", "norm_mix_post",
             "norm_ffn_pre", "w_gate", "w_up", "w_down", "norm_ffn_post"]
    consts = [p[k] if k in p else shared[k] for k in names]
    return pl.pallas_call(
        kern if ctx is not None else functools.partial(kern, None),
        grid=(bsz, nt + 1),
        in_specs=tile_specs(cur) + [mod, modf, row(RET_W), row(RET_W), row(RET_W, 3),
                  row(GLA_VWP), row(GLA_VWP), row(GLA_VWP, 2),
                  row(RW_W), row(RW_W), row(RW_S2_W)] + [spec(k) for k in names],
        out_specs=pl.BlockSpec((None, ROW_TILE, d), lambda b, i: (b, prv(i), 0)),
        out_shape=jax.ShapeDtypeStruct((bsz, nt * ROW_TILE, d), F32),
        scratch_shapes=[pltpu.VMEM((ROW_TILE, d), F32)],
        compiler_params=_params("arbitrary", "arbitrary"),
        name="post",
    )(*arrays, modsel, modsel, ra[0], ra[1], u_ret, ga[0], ga[1], u_gla, wa[0], wa[1], s2, *consts)


def _pad_last(a, n):
    return jnp.pad(a, [(0, 0)] * (a.ndim - 1) + [(0, n - a.shape[-1])])


def _pad_heads(a, heads, dh, dhp):
    lead = a.shape[:-1]
    a = a.reshape(lead + (heads, dh))
    return _pad_last(a, dhp).reshape(lead + (heads * dhp,))


def _arrange_rw(a):
    w = RW_W
    return jnp.concatenate([a[..., 0:3 * w + 128], _pad_last(a[..., 3 * w + 128:3 * w + 192], 128),
                            a[..., 3 * w + 192:3 * w + 320]], axis=-1)


def _arrange_in(a):
    g0 = URET_W
    kw, vw = GLA_HEADS * GLA_DK, GLA_HEADS * GLA_DV
    r0 = g0 + 2 * kw + 2 * vw + 2 * GLA_RANK
    return jnp.concatenate([
        a[..., 0:g0],
        _pad_heads(a[..., g0:g0 + kw], GLA_HEADS, GLA_DK, GLA_DKP),
        _pad_heads(a[..., g0 + kw:g0 + 2 * kw], GLA_HEADS, GLA_DK, GLA_DKP),
        _pad_heads(a[..., g0 + 2 * kw:g0 + 2 * kw + vw], GLA_HEADS, GLA_DV, GLA_DVP),
        _pad_heads(a[..., g0 + 2 * kw + vw:g0 + 2 * kw + 2 * vw], GLA_HEADS, GLA_DV, GLA_DVP),
        _pad_last(a[..., g0 + 2 * kw + 2 * vw:r0], 128),
        _arrange_rw(a[..., r0:]),
    ], axis=-1)


def _block_diag_ones(n, blk, val=1.0):
    i = np.arange(n)
    return jnp.asarray(np.where(i[:, None] // blk == i[None, :] // blk, val, 0.0), BF16)


def _rope_tables(t_lat):
    rows = t_lat // GRID_W
    row = jnp.repeat(jnp.arange(rows, dtype=F32), GRID_W)
    col = jnp.tile(jnp.arange(GRID_W, dtype=F32), rows)
    nf = RET_DH // 4
    inv = ROPE_THETA ** (-jnp.arange(nf, dtype=F32) / nf)
    ar, ac = row[:, None] * inv, col[:, None] * inv
    cos = jnp.concatenate([jnp.cos(ar), jnp.cos(ar), jnp.cos(ac), jnp.cos(ac)], axis=1)
    sin = jnp.concatenate([-jnp.sin(ar), jnp.sin(ar), -jnp.sin(ac), jnp.sin(ac)], axis=1)
    cos = jnp.concatenate([jnp.ones((CTX_LEN, RET_DH), F32), cos], axis=0)
    sin = jnp.concatenate([jnp.zeros((CTX_LEN, RET_DH), F32), sin], axis=0)
    return jnp.tile(cos, (1, RET_HEADS)), jnp.tile(sin, (1, RET_HEADS))


def kernel(x, c, ctx, c_ctx, w_mod, b_mod, norm_mix_pre, norm_mix_post, norm_ffn_pre, norm_ffn_post,
           w_in, ret_norm, gla_wa2_f, gla_ba_f, gla_wa2_b, gla_ba_b, gla_norm, rw_conv,
           rw_w0_f, rw_w2_f, rw_w0_b, rw_w2_b, rw_a0, rw_a2, rw_g2, rw_k_k, rw_k_a, rw_r_k,
           rw_ln_w, rw_ln_b, w_out, w_ffn_gate, w_ffn_up, w_ffn_down):
    bsz, t_lat, d = x.shape
    depth = w_mod.shape[0]
    assert d == D_MODEL and ctx.shape[1] == CTX_LEN and t_lat % ROW_TILE == 0 and bsz <= 7

    cc = jnp.zeros((8, d), F32).at[0:bsz].set(c).at[bsz].set(c_ctx)
    mods = _mod_call(cc, w_mod, b_mod)
    cos, sin = _rope_tables(t_lat)
    avg_r = _block_diag_ones(RET_W, RET_DH, 1.0 / RET_DH)
    avg_w = _block_diag_ones(RW_W, RW_DH, 1.0 / RW_DH)
    sum_w = _block_diag_ones(RW_W, RW_DH)
    sum_g = _block_diag_ones(GLA_VWP, GLA_DVP)

    shared = {"avg_r": avg_r, "sum_g": sum_g, "avg_w": avg_w}
    modsel = jnp.stack([jnp.broadcast_to(mods[:, bsz:bsz + 1], (depth, bsz, 6 * d)), mods[:, 0:bsz]],
                       axis=2)[:, :, :, None, :]

    rows = lambda a: a.reshape(depth, 1, -1)
    gla_heads = lambda a: _pad_heads(a, GLA_HEADS, GLA_DK, GLA_DKP)
    n_gv = GLA_HEADS * GLA_DV
    p = {
        "norm_mix_pre": rows(norm_mix_pre), "w_in": _arrange_in(_bf(w_in)),
        "conv": _arrange_rw(rw_conv), "a0": rows(rw_a0),
        "a2": _bf(jnp.pad(rw_a2, [(0, 0), (0, 64), (0, 0)])), "g2": _bf(rw_g2),
        "w0": jnp.concatenate([rw_w0_f, rw_w0_b], axis=-1)[:, None, :],
        "w2": _bf(jnp.concatenate([jnp.pad(rw_w2_f, [(0, 0), (0, 0), (0, RW_W)]),
                                   jnp.pad(rw_w2_b, [(0, 0), (0, 0), (RW_W, 0)])], axis=1)),
        "k_k": rows(rw_k_k), "k_a": rows(rw_k_a), "r_k": rows(rw_r_k),
        "gla_wa": _bf(jnp.stack([jnp.pad(gla_heads(gla_wa2_f), [(0, 0), (0, 128 - GLA_RANK), (0, 0)]),
                                 jnp.pad(gla_heads(gla_wa2_b), [(0, 0), (GLA_RANK, 128 - 2 * GLA_RANK), (0, 0)])],
                                axis=1)),
        "gla_ba": jnp.stack([gla_heads(gla_ba_f), gla_heads(gla_ba_b)], axis=1)[:, :, None, :],
        "ret_norm": rows(ret_norm),
        "gla_norm": rows(jnp.tile(_pad_last(gla_norm, GLA_DVP), (1, GLA_HEADS))),
        "ln_w": rows(rw_ln_w), "ln_b": rows(rw_ln_b),
        "w_out": _bf(jnp.concatenate([
            w_out[:, 0:RET_W],
            jnp.pad(w_out[:, RET_W:RET_W + n_gv].reshape(depth, GLA_HEADS, GLA_DV, d),
                    [(0, 0), (0, 0), (0, GLA_DVP - GLA_DV), (0, 0)]).reshape(depth, GLA_VWP, d),
            w_out[:, RET_W + n_gv:]], axis=1)),
        "norm_mix_post": rows(norm_mix_post), "norm_ffn_pre": rows(norm_ffn_pre),
        "w_gate": _bf(w_ffn_gate), "w_up": _bf(w_ffn_up), "w_down": _bf(w_ffn_down),
        "norm_ffn_post": rows(norm_ffn_post),
    }

    ctx_in, xa = ctx, x
    for l in range(depth):
        u_ret, u_gla, s1, s2 = _proj_call(ctx_in, xa, modsel, p, sum_w, l)
        ra, ga, wo = _scan_call(u_ret, cos, sin, u_gla, s1, p, l)
        tile_off = CTX_LEN // ROW_TILE if l == depth - 1 else 0
        xa = _post_call(ctx_in, xa, modsel, ra, u_ret, ga, u_gla, wo, s2, p, shared, l, tile_off)
        ctx_in = None
    return xa
```

```python
import functools

import numpy as np
import jax
import jax.numpy as jnp
from jax import lax
from jax.experimental import pallas as pl
from jax.experimental.pallas import tpu as pltpu

F32 = jnp.float32
BF16 = jnp.bfloat16

D_MODEL = 1024
CTX_LEN = 256
GRID_W = 64
ROPE_THETA = 10000.0

RET_HEADS, RET_DH = 4, 64
RET_W = RET_HEADS * RET_DH
GLA_HEADS, GLA_DK, GLA_DV = 4, 48, 96
GLA_DKP, GLA_DVP = 64, 128
GLA_KWP, GLA_VWP = GLA_HEADS * GLA_DKP, GLA_HEADS * GLA_DVP
GLA_RANK = 16
GLA_TAU = 16.0
RW_HEADS, RW_DH = 6, 64
RW_W = RW_HEADS * RW_DH
FFN_HIDDEN = 2816

URET_W = 4 * RET_W
UGLA_W = 2 * GLA_KWP + 2 * GLA_VWP + 128
URW_W = 3 * RW_W + 128 + 128 + 128
NP_IN = URET_W + UGLA_W + URW_W
MIXP_W = RET_W + GLA_VWP + RW_W

CHUNK = 64
SCAN_BLOCK = 128
RET_CHUNK = SCAN_BLOCK
ROW_TILE = 256
VMEM_LIMIT = 56 * 1024 * 1024


def _bf(x):
    return x.astype(BF16)


def _dot(a, b):
    return jnp.dot(a, b, preferred_element_type=F32)


def _dot_nt(a, b):
    return lax.dot_general(a, b, (((1,), (1,)), ((), ())), preferred_element_type=F32)


def _dot_tn(a, b):
    return lax.dot_general(a, b, (((0,), (0,)), ((), ())), preferred_element_type=F32)


def _split2(x):
    hi = _bf(x)
    return hi, _bf(x - hi.astype(F32))


def _group_dot(x, m):
    return _dot(_bf(x), m)


def _cumsum_dot(m, x):
    hi, lo = _split2(x)
    return _dot(m, hi) + _dot(m, lo)


def _dot_3pass(a, b):
    ah, al = _split2(a)
    bh, bl = _split2(b)
    return _dot(ah, bh) + (_dot(ah, bl) + _dot(al, bh))


def _sigmoid(x):
    return 1.0 / (1.0 + jnp.exp(-x))


def _silu(x):
    return x * _sigmoid(x)


def _softplus(x):
    return jnp.maximum(x, 0.0) + jnp.log(1.0 + jnp.exp(-jnp.abs(x)))


def _rms(x, g, eps=1e-6):
    return x * lax.rsqrt(jnp.mean(x * x, axis=-1, keepdims=True) + eps) * g


def _order_masks(c, reverse):
    t = lax.broadcasted_iota(jnp.int32, (c, c), 0)
    s = lax.broadcasted_iota(jnp.int32, (c, c), 1)
    if reverse:
        return s >= t, s > t
    return s <= t, s < t


def _params(*sem):
    return pltpu.CompilerParams(dimension_semantics=sem, vmem_limit_bytes=VMEM_LIMIT)


def _resident_spec(shape):
    nd = len(shape)
    return pl.BlockSpec(shape, lambda *_: (0,) * nd, pipeline_mode=pl.Buffered(1))


def _layer_spec(a, l):
    nd = a.ndim - 1
    return pl.BlockSpec((None,) + a.shape[1:], lambda *_: (l,) + (0,) * nd, pipeline_mode=pl.Buffered(1))


def _mod_kernel(c_ref, w_ref, b_ref, o_ref):
    c = c_ref[...]
    o_ref[...] = _dot_3pass(_silu(c), w_ref[...]) + b_ref[...]


def _mod_call(cc, w_mod, b_mod):
    depth, d, n = w_mod.shape
    tn = 1024
    return pl.pallas_call(
        _mod_kernel,
        grid=(depth, n // tn),
        in_specs=[
            pl.BlockSpec((8, d), lambda l, j: (0, 0)),
            pl.BlockSpec((None, d, tn), lambda l, j: (l, 0, j)),
            pl.BlockSpec((None, 1, tn), lambda l, j: (l, 0, j)),
        ],
        out_specs=pl.BlockSpec((None, 8, tn), lambda l, j: (l, 0, j)),
        out_shape=jax.ShapeDtypeStruct((depth, 8, n), F32),
        compiler_params=_params("arbitrary", "arbitrary"),
        name="mod",
    )(cc, w_mod, b_mod.reshape(depth, 1, n))


RW_S1_W = 7 * RW_W
RW_S2_W = 2 * RW_W
HALO = 16


def _tile_rows(i, c_ref, x_ref):
    return x_ref[...] if c_ref is None else jnp.where(i == 0, c_ref[...], x_ref[...])


def _proj_kernel(c_ref, x_ref, xp_ref, xn_ref, mod_ref, g_ref, w_ref, cw_ref, a0_ref, a2_ref, g2_ref, w0_ref,
                 w2_ref, kk_ref, ka_ref, rk_ref, bsum_ref, ur_ref, ug_ref, s1_ref, s2_ref, *, n_tiles):
    i = pl.program_id(1)
    d, tm, te = D_MODEL, ROW_TILE, ROW_TILE + 2 * HALO
    xe = jnp.concatenate([xp_ref[...], _tile_rows(i, c_ref, x_ref), xn_ref[...]], axis=0)
    h = _rms(xe, g_ref[...])
    h = h * (1.0 + mod_ref[:, d:2 * d]) + mod_ref[:, 0:d]
    rows = lax.broadcasted_iota(jnp.int32, (te, 1), 0)
    keep = jnp.logical_and(jnp.logical_or(rows >= HALO, i >= 2),
                           jnp.logical_or(rows < HALO + tm, jnp.logical_and(i >= 1, i < n_tiles - 1)))
    h = _bf(jnp.where(keep, h, 0.0))
    ue = _dot(h, w_ref[:, URET_W + UGLA_W:NP_IN])
    hm = h[HALO:HALO + tm]
    ur_ref[...] = _dot(hm, w_ref[:, 0:URET_W])
    ug_ref[...] = _dot(hm, w_ref[:, URET_W:URET_W + UGLA_W])

    mid = lambda a: a[HALO:HALO + tm]
    uc = (mid(pltpu.roll(ue, 1, 0)) * cw_ref[0:1, :] + mid(ue) * cw_ref[1:2, :]
          + mid(pltpu.roll(ue, te - 1, 0)) * cw_ref[2:3, :])
    w = RW_W
    r, k, v = uc[:, 0:w], uc[:, w:2 * w], uc[:, 2 * w:3 * w]
    lr_w = uc[:, 3 * w:3 * w + 128]
    lr_a = uc[:, 3 * w + 128:3 * w + 256]
    lr_g = uc[:, 3 * w + 256:3 * w + 384]
    a = _sigmoid(a0_ref[...] + _dot(_bf(lr_a), a2_ref[...]))
    gate = _dot(_bf(_sigmoid(lr_g)), g2_ref[...])
    kk = k * kk_ref[...]
    bsum = bsum_ref[...]
    kk = kk * lax.rsqrt(jnp.maximum(_group_dot(kk * kk, bsum), 1e-24))
    k2 = k * (1.0 + (a - 1.0) * ka_ref[...])
    ld = -float(np.exp(-0.5)) * _sigmoid(w0_ref[...] + _dot(_bf(jnp.tanh(lr_w)), w2_ref[...]))
    bonus = _group_dot(r * k2 * rk_ref[...], bsum) * v
    s1_ref[:, 0:w] = r
    s1_ref[:, w:2 * w] = k2
    s1_ref[:, 2 * w:3 * w] = v
    s1_ref[:, 3 * w:4 * w] = -kk
    s1_ref[:, 4 * w:5 * w] = kk * a
    s1_ref[:, 5 * w:7 * w] = ld
    s2_ref[:, 0:w] = gate
    s2_ref[:, w:2 * w] = bonus


def _stream_inputs(ctx, x):
    d = x.shape[2]
    off = 0 if ctx is None else CTX_LEN // ROW_TILE

    def specs(tile):
        x_spec = pl.BlockSpec((None, ROW_TILE, d), lambda b, i: (b, jnp.maximum(tile(i) - off, 0), 0))
        if ctx is None:
            return [x_spec]
        return [pl.BlockSpec((None, ROW_TILE, d), lambda b, i: (b, 0, 0)), x_spec]

    return ([x] if ctx is None else [ctx, x]), specs, off


def _proj_call(ctx, x, modsel, p, sum_w, l):
    bsz, tx, d = x.shape
    arrays, tile_specs, off = _stream_inputs(ctx, x)
    nt = tx // ROW_TILE + off
    per = ROW_TILE // HALO
    nh = tx // HALO
    row = lambda w_: pl.BlockSpec((None, ROW_TILE, w_), lambda b, i: (b, i, 0))
    prev = pl.BlockSpec((None, HALO, d), lambda b, i: (b, jnp.clip((i - off) * per - 1, 0, nh - 1), 0))
    nxt = pl.BlockSpec((None, HALO, d), lambda b, i: (b, jnp.clip((i - off + 1) * per, 0, nh - 1), 0))
    mod = pl.BlockSpec((None, None, None, 1, 6 * d), lambda b, i: (l, b, jnp.minimum(i, 1), 0, 0))
    consts = [p[k] for k in ("norm_mix_pre", "w_in", "conv", "a0", "a2", "g2", "w0", "w2", "k_k", "k_a", "r_k")]
    widths = (URET_W, UGLA_W, RW_S1_W, RW_S2_W)
    kern = functools.partial(_proj_kernel, n_tiles=nt)
    return pl.pallas_call(
        kern if ctx is not None else functools.partial(kern, None),
        grid=(bsz, nt),
        in_specs=(tile_specs(lambda i: i) + [prev, nxt, mod] + [_layer_spec(a, l) for a in consts]
                  + [_resident_spec(sum_w.shape)]),
        out_specs=[row(w_) for w_ in widths],
        out_shape=[jax.ShapeDtypeStruct((bsz, nt * ROW_TILE, w_), F32) for w_ in widths],
        compiler_params=_params("arbitrary", "arbitrary"),
        name="proj",
    )(*arrays, x, x, modsel, *consts, sum_w)


def _chunk_maps(ta, chunk):
    ns = ta // chunk
    nc = CTX_LEN // chunk

    def fwd(s):
        return s

    def bwd(s):
        return jnp.where(s < nc, nc - 1 - s, ns + nc - 1 - s)

    return ns, fwd, bwd


def _chunk_specs(bsz, ta, chunk, width):
    _, fwd, bwd = _chunk_maps(ta, chunk)
    return tuple(pl.BlockSpec((bsz, chunk, width), lambda s, m=m: (0, m(s), 0)) for m in (fwd, bwd))


def _rope_swap(x):
    lane = lax.broadcasted_iota(jnp.int32, x.shape, 1)
    w = x.shape[1]
    return jnp.where(lane % 32 < 16, pltpu.roll(x, w - 16, 1), pltpu.roll(x, 16, 1))


def _ret_body(uf_ref, ub_ref, cf_ref, sf_ref, cb_ref, sb_ref, dm_ref, qd_ref, kd_ref, cd_ref,
              of_ref, ob_ref, st_ref):
    bsz = uf_ref.shape[0]
    groups = []
    for d, (u_ref, cos_ref, sin_ref) in enumerate(((uf_ref, cf_ref, sf_ref), (ub_ref, cb_ref, sb_ref))):
        cos, sin = cos_ref[...], sin_ref[...]
        for b in range(bsz):
            q = u_ref[b, :, 0:RET_W]
            k = u_ref[b, :, RET_W:2 * RET_W]
            q = q * cos + _rope_swap(q) * sin
            k = (k * cos + _rope_swap(k) * sin) * (RET_DH ** -0.5)
            groups.append(dict(d=d, b=b, q=_bf(q), k=_bf(k), qd=_bf(q * qd_ref[d]), kd=_bf(k * kd_ref[d]),
                               v=_bf(u_ref[b, :, 2 * RET_W:3 * RET_W])))
    c = uf_ref.shape[1]
    lane_head = lax.broadcasted_iota(jnp.int32, (c, RET_W), 1) // RET_DH
    zero = jnp.zeros((c, RET_W), BF16)
    own = (lax.broadcasted_iota(jnp.int32, (RET_W, RET_W), 0) // RET_DH
           == lax.broadcasted_iota(jnp.int32, (RET_W, RET_W), 1) // RET_DH)
    yield
    sc = [jnp.concatenate(
        [_bf(_dot_nt(p["q"], jnp.where(lane_head == h, p["k"], zero)) * dm_ref[p["d"], h]) for h in range(RET_HEADS)],
        axis=1) for p in groups]
    yield
    vs = [jnp.concatenate([jnp.where(lane_head == h, p["v"], zero) for h in range(RET_HEADS)], axis=0)
          for p in groups]
    s0 = [st_ref[p["d"], p["b"]] for p in groups]
    outs = [_dot(sci, vi) + _dot_nt(p["qd"], _bf(s)) for sci, vi, s, p in zip(sc, vs, s0, groups)]
    yield
    for s, p in zip(s0, groups):
        st_ref[p["d"], p["b"]] = s * cd_ref[p["d"]] + jnp.where(own, _dot_tn(p["v"], p["kd"]), 0.0)
    for o, p in zip(outs, groups):
        (ob_ref if p["d"] else of_ref)[p["b"]] = o


def _ret_consts():
    c = RET_CHUNK
    lg = np.log1p(-np.exp2(-5.0 - np.arange(RET_HEADS, dtype=np.float64)))
    pos = np.arange(c, dtype=np.float64)
    rel = pos[:, None] - pos[None, :]
    dm = np.zeros((2, RET_HEADS, c, c))
    qd = np.zeros((2, c, RET_W))
    kd = np.zeros((2, c, RET_W))
    cd = np.zeros((2, RET_W, RET_W))
    for d in range(2):
        lgd = lg if d == 0 else lg[::-1]
        p = pos if d == 0 else c - 1.0 - pos
        for h in range(RET_HEADS):
            intra = np.where(rel >= 0, np.exp(lgd[h] * np.maximum(rel, 0.0)), 0.0)
            dm[d, h] = intra if d == 0 else intra.T
            qd[d, :, h * RET_DH:(h + 1) * RET_DH] = np.exp(lgd[h] * (p + 1.0))[:, None]
            kd[d, :, h * RET_DH:(h + 1) * RET_DH] = np.exp(lgd[h] * (c - 1.0 - p))[:, None]
            cd[d, h * RET_DH:(h + 1) * RET_DH, :] = np.exp(lgd[h] * c)
    return tuple(jnp.asarray(a, F32) for a in (dm, qd, kd, cd))


def _gla_body(uf_ref, ub_ref, wa_ref, ba_ref, of_ref, ob_ref, st_ref):
    bsz = uf_ref.shape[0]
    c = CHUNK
    nsub = uf_ref.shape[1] // c
    half = c // 2
    groups = []
    for d, u_ref in enumerate((uf_ref, ub_ref)):
        rev = d == 1
        incl, _ = _order_masks(c, rev)
        incl_b = incl.astype(BF16)
        pos = lambda p: c - 1 - p if rev else p
        row = lax.broadcasted_iota(jnp.int32, (c, 1), 0)
        first = (row >= half) if rev else (row < half)
        ti = lax.broadcasted_iota(jnp.int32, (c, c), 0)
        si = lax.broadcasted_iota(jnp.int32, (c, c), 1)
        if rev:
            straddle = jnp.logical_and(ti < half, si >= half)
        else:
            straddle = jnp.logical_and(ti >= half, si < half)
        for j in range(nsub):
            rows = pl.ds((nsub - 1 - j if rev else j) * c, c)
            for b in range(bsz):
                lr = _bf(u_ref[b, rows, 2 * GLA_KWP + 2 * GLA_VWP:UGLA_W])
                z = _dot(lr, wa_ref[d]) + ba_ref[d]
                la = -_softplus(-z) * (1.0 / GLA_TAU)
                g = _cumsum_dot(incl_b, la)
                g_end = jnp.sum(la, axis=0, keepdims=True)
                g_row = lambda p: g[pos(p):pos(p) + 1, :]
                piv = jnp.where(first, g_row(half // 2 - 1), g_row(half + half // 2 - 1))
                g_cut = g_row(half - 1)
                q = u_ref[b, rows, 0:GLA_KWP] * (GLA_DK ** -0.5)
                k = u_ref[b, rows, GLA_KWP:2 * GLA_KWP]
                groups.append(dict(
                    d=d, b=b, j=j, rows=rows, incl=incl, straddle=straddle, e_end=jnp.exp(g_end),
                    qt=_bf(q * jnp.exp(g - piv)), kt=_bf(k * jnp.exp(piv - g)),
                    qo=_bf(q * jnp.exp(jnp.minimum(g - g_cut, 0.0))),
                    ko=_bf(k * jnp.exp(jnp.minimum(g_cut - g, 0.0))),
                    qs=_bf(q * jnp.exp(g)), ks=_bf(k * jnp.exp(g_end - g)),
                    v=_bf(u_ref[b, rows, 2 * GLA_KWP:2 * GLA_KWP + GLA_VWP])))
    chains = [(p, h, slice(h * GLA_DKP, (h + 1) * GLA_DKP), slice(h * GLA_DVP, (h + 1) * GLA_DVP))
              for p in groups for h in range(GLA_HEADS)]
    yield
    same = [_dot_nt(p["qt"][:, sk], p["kt"][:, sk]) for p, _, sk, _ in chains]
    yield
    cross = [_dot_nt(p["qo"][:, sk], p["ko"][:, sk]) for p, _, sk, _ in chains]
    sc = [_bf(jnp.where(p["incl"], jnp.where(p["straddle"], x, s), 0.0))
          for s, x, (p, _, _, _) in zip(same, cross, chains)]
    yield
    intra = [_dot(sci, p["v"][:, sv]) for sci, (p, _, _, sv) in zip(sc, chains)]
    yield
    kv = [_dot_tn(p["v"][:, sv], p["ks"][:, sk]) for p, _, sk, sv in chains]
    state = {(d, b, h): st_ref[d, b, h] for d in range(2) for b in range(bsz) for h in range(GLA_HEADS)}
    outs = {}
    for j in range(nsub):
        yield
        step = [(i, ch) for i, ch in enumerate(chains) if ch[0]["j"] == j]
        s0 = [state[(p["d"], p["b"], h)] for _, (p, h, _, _) in step]
        for s, (i, (p, h, sk, _)) in zip(s0, step):
            outs[(p["d"], p["b"], j, h)] = intra[i] + _dot_nt(p["qs"][:, sk], _bf(s))
            state[(p["d"], p["b"], h)] = s * p["e_end"][:, sk] + kv[i]
    for key, s in state.items():
        st_ref[key] = s
    for p in groups:
        o_ref = ob_ref if p["d"] else of_ref
        o_ref[p["b"], p["rows"], :] = jnp.concatenate(
            [outs[(p["d"], p["b"], p["j"], h)] for h in range(GLA_HEADS)], axis=1)


def _rw_body(sf_ref, sb_ref, of_ref, ob_ref, st_ref):
    bsz = sf_ref.shape[0]
    c, w, dh = CHUNK, RW_W, RW_DH
    nsub = sf_ref.shape[1] // c
    groups = []
    for d, s_ref in enumerate((sf_ref, sb_ref)):
        incl, _ = _order_masks(c, d == 1)
        incl_b = incl.astype(BF16)
        tt = lax.broadcasted_iota(jnp.int32, (2 * c, 2 * c), 0)
        ss = lax.broadcasted_iota(jnp.int32, (2 * c, 2 * c), 1) & (c - 1)
        t_in = tt & (c - 1)
        before = (ss > t_in) if d == 1 else (ss < t_in)
        pair_mask = jnp.logical_or(before, jnp.logical_and(tt >= c, ss == t_in))
        t64 = lax.broadcasted_iota(jnp.int32, (c, c), 0)
        s64 = lax.broadcasted_iota(jnp.int32, (c, c), 1)
        eye = (t64 == s64).astype(F32)
        late, early = (s64, t64) if d == 1 else (t64, s64)
        level_masks = [
            jnp.logical_and((t64 >> (j + 1)) == (s64 >> (j + 1)),
                            jnp.logical_and(((late >> j) & 1) == 1, ((early >> j) & 1) == 0))
            for j in range(6)]
        for j in range(nsub):
            rows = pl.ds((nsub - 1 - j if d == 1 else j) * c, c)
            for b in range(bsz):
                la = s_ref[b, rows, (5 + d) * w:(6 + d) * w]
                g = _cumsum_dot(incl_b, la)
                g_end = jnp.sum(la, axis=0, keepdims=True)
                e_k = jnp.exp(g_end - g)
                rt = s_ref[b, rows, 0:w] * jnp.exp(g - g_end)
                at = s_ref[b, rows, 3 * w:4 * w] * jnp.exp(g - la - g_end)
                kt = s_ref[b, rows, w:2 * w] * e_k
                bt = s_ref[b, rows, 4 * w:5 * w] * e_k
                groups.append(dict(
                    d=d, b=b, j=j, rows=rows, pair_mask=pair_mask, level_masks=level_masks, eye=eye,
                    e_end=jnp.exp(g_end),
                    lhs=_bf(jnp.concatenate([at, rt], axis=0)),
                    rhs=_bf(jnp.concatenate([bt, kt], axis=0)),
                    v=_bf(s_ref[b, rows, 2 * w:3 * w])))

    chains = [(q, h, slice(h * dh, (h + 1) * dh)) for q in groups for h in range(RW_HEADS)]
    yield
    pair = [jnp.where(q["pair_mask"], _dot_nt(q["lhs"][:, sl], q["rhs"][:, sl]), 0.0)
            for q, _, sl in chains]
    yield
    arbk = [_bf(p[c:2 * c, :]) for p in pair]
    n = [_bf(p[0:c, 0:c]) for p in pair]
    x = [jnp.concatenate([_bf(_dot(_bf(p[0:c, c:2 * c]), q["v"][:, sl])), q["lhs"][0:c, sl]], axis=1)
         for p, (q, _, sl) in zip(pair, chains)]
    tm = [jnp.where(q["level_masks"][0], p[0:c, 0:c], q["eye"]) for p, (q, _, _) in zip(pair, chains)]
    del pair
    for lvl in range(1, 6):
        yield
        tb = [_bf(t) for t in tm]
        left = [_bf(_dot(tbi, jnp.where(q["level_masks"][lvl], ni, jnp.zeros_like(ni))))
                for tbi, ni, (q, _, _) in zip(tb, n, chains)]
        yield
        tm = [t + _dot(li, tbi) for t, li, tbi in zip(tm, left, tb)]
    yield
    x = [_dot(_bf(t), xi) for t, xi in zip(tm, x)]
    mr = [jnp.concatenate([_bf(xi[:, dh:2 * dh]), q["lhs"][c:2 * c, sl]], axis=0)
          for xi, (q, _, sl) in zip(x, chains)]
    state = {(d, b, h): st_ref[d, b, h] for d in range(2) for b in range(bsz) for h in range(RW_HEADS)}
    outs = {}
    for j in range(nsub):
        yield
        step = [(i, ch) for i, ch in enumerate(chains) if ch[0]["j"] == j]
        s0 = [state[(q["d"], q["b"], h)] * q["e_end"][:, sl] for _, (q, h, sl) in step]
        ms = [_dot_nt(mr[i], _bf(s)) for s, (i, _) in zip(s0, step)]
        uv = [jnp.concatenate([_bf(x[i][:, 0:dh] + m[0:c]), q["v"][:, sl]], axis=0)
              for m, (i, (q, _, sl)) in zip(ms, step)]
        for s, m, w, (i, (q, h, sl)) in zip(s0, ms, uv, step):
            outs[(q["d"], q["b"], j, h)] = m[c:2 * c] + _dot(arbk[i], w)
            state[(q["d"], q["b"], h)] = s + _dot_tn(w, q["rhs"][:, sl])
    for key, s in state.items():
        st_ref[key] = s
    for q in groups:
        o_ref = ob_ref if q["d"] else of_ref
        o_ref[q["b"], q["rows"], :] = jnp.concatenate(
            [outs[(q["d"], q["b"], q["j"], h)] for h in range(RW_HEADS)], axis=1)


def _scan_kernel(*refs):
    ret_in, gla_in, rw_in = refs[0:10], refs[10:14], refs[14:16]
    (ret_of, ret_ob, gla_of, gla_ob, rw_of, rw_ob, ret_st, gla_st, rw_st) = refs[16:]

    @pl.when(pl.program_id(0) == 0)
    def _():
        for st in (ret_st, gla_st, rw_st):
            st[...] = jnp.zeros_like(st)

    bodies = [_rw_body(*rw_in, rw_of, rw_ob, rw_st), _gla_body(*gla_in, gla_of, gla_ob, gla_st),
              _ret_body(*ret_in, ret_of, ret_ob, ret_st)]
    while bodies:
        bodies = [b for b in bodies if next(b, StopIteration) is not StopIteration]


def _scan_call(u_ret, cos, sin, u_gla, s1, p, l):
    bsz, ta, _ = u_ret.shape
    c = SCAN_BLOCK
    ns, fwd, bwd = _chunk_maps(ta, c)
    both = lambda w_: list(_chunk_specs(bsz, ta, c, w_))
    tf, tb = (pl.BlockSpec((c, RET_W), lambda s, m=m: (m(s), 0)) for m in (fwd, bwd))
    ret_consts = _ret_consts()
    wa, ba = p["gla_wa"], p["gla_ba"]
    out_w = (RET_W, RET_W, GLA_VWP, GLA_VWP, RW_W, RW_W)
    outs = pl.pallas_call(
        _scan_kernel,
        grid=(ns,),
        in_specs=(both(URET_W) + [tf, tf, tb, tb] + [_resident_spec(a.shape) for a in ret_consts]
                  + both(UGLA_W) + [_layer_spec(wa, l), _layer_spec(ba, l)] + both(RW_S1_W)),
        out_specs=both(RET_W) + both(GLA_VWP) + both(RW_W),
        out_shape=[jax.ShapeDtypeStruct((bsz, ta, w_), F32) for w_ in out_w],
        scratch_shapes=[pltpu.VMEM((2, bsz, RET_W, RET_W), F32),
                        pltpu.VMEM((2, bsz, GLA_HEADS, GLA_DVP, GLA_DKP), F32),
                        pltpu.VMEM((2, bsz, RW_HEADS, RW_DH, RW_DH), F32)],
        compiler_params=_params("arbitrary"),
        name="mix_scan",
    )(u_ret, u_ret, cos, sin, cos, sin, *ret_consts, u_gla, u_gla, wa, ba, s1, s1)
    return outs[0:2], outs[2:4], outs[4:6]


def _post_kernel(c_ref, x_ref, mod_ref, modf_ref, raf_ref, rab_ref, rg_ref, gaf_ref, gab_ref, gg_ref,
                 waf_ref, wab_ref, s2_ref, rn_ref, gn_ref, lw_ref, lb_ref,
                 avg_r_ref, sum_g_ref, avg_w_ref, wo_ref, pn_ref,
                 pre_ref, wg_ref, wu_ref, wd_ref, post_ref, o_ref, x1_ref, *, cur_tile):
    d = D_MODEL

    @pl.when(jnp.logical_and(pl.program_id(0) == 0, pl.program_id(1) == 0))
    def _():
        x1_ref[...] = jnp.zeros_like(x1_ref)

    x_prev = x1_ref[...]
    o = raf_ref[...] + rab_ref[...]
    avg = avg_r_ref[...]
    mu = _group_dot(o, avg)
    oc = o - mu
    var = _group_dot(oc * oc, avg)
    ya = _silu(rg_ref[...]) * (oc * lax.rsqrt(var + 1e-5) * rn_ref[...])
    o = gaf_ref[...] + gab_ref[...]
    ms = _group_dot(o * o, sum_g_ref[...]) * (1.0 / GLA_DV)
    yb = _silu(gg_ref[...]) * (o * lax.rsqrt(ms + 1e-5) * gn_ref[...])
    o = waf_ref[...] + wab_ref[...]
    avg = avg_w_ref[...]
    mu = _group_dot(o, avg)
    oc = o - mu
    var = _group_dot(oc * oc, avg)
    yc = (oc * lax.rsqrt(var + 64e-5) * lw_ref[...] + lb_ref[...] + s2_ref[:, RW_W:2 * RW_W]) * s2_ref[:, 0:RW_W]
    y = (_dot(_bf(ya), wo_ref[0:RET_W, :]) + _dot(_bf(yb), wo_ref[RET_W:RET_W + GLA_VWP, :])
         + _dot(_bf(yc), wo_ref[RET_W + GLA_VWP:MIXP_W, :]))
    x_res = _tile_rows(cur_tile(pl.program_id(1)), c_ref, x_ref)
    x1_ref[...] = x_res + mod_ref[:, 2 * d:3 * d] * _rms(y, pn_ref[...])
    h = _rms(x_prev, pre_ref[...])
    h = _bf(h * (1.0 + modf_ref[:, 4 * d:5 * d]) + modf_ref[:, 3 * d:4 * d])
    act = _bf(_silu(_dot(h, wg_ref[...])) * _dot(h, wu_ref[...]))
    f = _dot(act, wd_ref[...])
    o_ref[...] = x_prev + modf_ref[:, 5 * d:6 * d] * _rms(f, post_ref[...])


def _post_call(ctx, x, modsel, ra, u_ret, ga, u_gla, wa, s2, p, shared, l, tile_off):
    bsz, tx, d = x.shape
    arrays, tile_specs, off = _stream_inputs(ctx, x)
    nt = tx // ROW_TILE + off - tile_off
    cur = lambda i: jnp.minimum(i, nt - 1) + tile_off
    prv = lambda i: jnp.maximum(i - 1, 0)
    row = lambda w_, cb=0: pl.BlockSpec((None, ROW_TILE, w_), lambda b, i: (b, cur(i), cb))
    kern = functools.partial(_post_kernel, cur_tile=cur)
    mod = pl.BlockSpec((None, None, None, 1, 6 * d), lambda b, i: (l, b, jnp.minimum(cur(i), 1), 0, 0))
    modf = pl.BlockSpec((None, None, None, 1, 6 * d),
                        lambda b, i: (l, b, jnp.minimum(prv(i) + tile_off, 1), 0, 0))
    spec = lambda k: _layer_spec(p[k], l) if k in p else _resident_spec(shared[k].shape)
    names = ["ret_norm", "gla_norm", "ln_w", "ln_b", "avg_r", "sum_g", "avg_w", "w_out", "norm_mix_post",
             "norm_ffn_pre", "w_gate", "w_up", "w_down", "norm_ffn_post"]
    consts = [p[k] if k in p else shared[k] for k in names]
    return pl.pallas_call(
        kern if ctx is not None else functools.partial(kern, None),
        grid=(bsz, nt + 1),
        in_specs=tile_specs(cur) + [mod, modf, row(RET_W), row(RET_W), row(RET_W, 3),
                  row(GLA_VWP), row(GLA_VWP), row(GLA_VWP, 2),
                  row(RW_W), row(RW_W), row(RW_S2_W)] + [spec(k) for k in names],
        out_specs=pl.BlockSpec((None, ROW_TILE, d), lambda b, i: (b, prv(i), 0)),
        out_shape=jax.ShapeDtypeStruct((bsz, nt * ROW_TILE, d), F32),
        scratch_shapes=[pltpu.VMEM((ROW_TILE, d), F32)],
        compiler_params=_params("arbitrary", "arbitrary"),
        name="post",
    )(*arrays, modsel, modsel, ra[0], ra[1], u_ret, ga[0], ga[1], u_gla, wa[0], wa[1], s2, *consts)


def _pad_last(a, n):
    return jnp.pad(a, [(0, 0)] * (a.ndim - 1) + [(0, n - a.shape[-1])])


def _pad_heads(a, heads, dh, dhp):
    lead = a.shape[:-1]
    a = a.reshape(lead + (heads, dh))
    return _pad_last(a, dhp).reshape(lead + (heads * dhp,))


def _arrange_rw(a):
    w = RW_W
    return jnp.concatenate([a[..., 0:3 * w + 128], _pad_last(a[..., 3 * w + 128:3 * w + 192], 128),
                            a[..., 3 * w + 192:3 * w + 320]], axis=-1)


def _arrange_in(a):
    g0 = URET_W
    kw, vw = GLA_HEADS * GLA_DK, GLA_HEADS * GLA_DV
    r0 = g0 + 2 * kw + 2 * vw + 2 * GLA_RANK
    return jnp.concatenate([
        a[..., 0:g0],
        _pad_heads(a[..., g0:g0 + kw], GLA_HEADS, GLA_DK, GLA_DKP),
        _pad_heads(a[..., g0 + kw:g0 + 2 * kw], GLA_HEADS, GLA_DK, GLA_DKP),
        _pad_heads(a[..., g0 + 2 * kw:g0 + 2 * kw + vw], GLA_HEADS, GLA_DV, GLA_DVP),
        _pad_heads(a[..., g0 + 2 * kw + vw:g0 + 2 * kw + 2 * vw], GLA_HEADS, GLA_DV, GLA_DVP),
        _pad_last(a[..., g0 + 2 * kw + 2 * vw:r0], 128),
        _arrange_rw(a[..., r0:]),
    ], axis=-1)


def _block_diag_ones(n, blk, val=1.0):
    i = np.arange(n)
    return jnp.asarray(np.where(i[:, None] // blk == i[None, :] // blk, val, 0.0), BF16)


def _rope_tables(t_lat):
    rows = t_lat // GRID_W
    row = jnp.repeat(jnp.arange(rows, dtype=F32), GRID_W)
    col = jnp.tile(jnp.arange(GRID_W, dtype=F32), rows)
    nf = RET_DH // 4
    inv = ROPE_THETA ** (-jnp.arange(nf, dtype=F32) / nf)
    ar, ac = row[:, None] * inv, col[:, None] * inv
    cos = jnp.concatenate([jnp.cos(ar), jnp.cos(ar), jnp.cos(ac), jnp.cos(ac)], axis=1)
    sin = jnp.concatenate([-jnp.sin(ar), jnp.sin(ar), -jnp.sin(ac), jnp.sin(ac)], axis=1)
    cos = jnp.concatenate([jnp.ones((CTX_LEN, RET_DH), F32), cos], axis=0)
    sin = jnp.concatenate([jnp.zeros((CTX_LEN, RET_DH), F32), sin], axis=0)
    return jnp.tile(cos, (1, RET_HEADS)), jnp.tile(sin, (1, RET_HEADS))


def kernel(x, c, ctx, c_ctx, w_mod, b_mod, norm_mix_pre, norm_mix_post, norm_ffn_pre, norm_ffn_post,
           w_in, ret_norm, gla_wa2_f, gla_ba_f, gla_wa2_b, gla_ba_b, gla_norm, rw_conv,
           rw_w0_f, rw_w2_f, rw_w0_b, rw_w2_b, rw_a0, rw_a2, rw_g2, rw_k_k, rw_k_a, rw_r_k,
           rw_ln_w, rw_ln_b, w_out, w_ffn_gate, w_ffn_up, w_ffn_down):
    bsz, t_lat, d = x.shape
    depth = w_mod.shape[0]
    assert d == D_MODEL and ctx.shape[1] == CTX_LEN and t_lat % ROW_TILE == 0 and bsz <= 7

    cc = jnp.zeros((8, d), F32).at[0:bsz].set(c).at[bsz].set(c_ctx)
    mods = _mod_call(cc, w_mod, b_mod)
    cos, sin = _rope_tables(t_lat)
    avg_r = _block_diag_ones(RET_W, RET_DH, 1.0 / RET_DH)
    avg_w = _block_diag_ones(RW_W, RW_DH, 1.0 / RW_DH)
    sum_w = _block_diag_ones(RW_W, RW_DH)
    sum_g = _block_diag_ones(GLA_VWP, GLA_DVP)

    shared = {"avg_r": avg_r, "sum_g": sum_g, "avg_w": avg_w}
    modsel = jnp.stack([jnp.broadcast_to(mods[:, bsz:bsz + 1], (depth, bsz, 6 * d)), mods[:, 0:bsz]],
                       axis=2)[:, :, :, None, :]

    rows = lambda a: a.reshape(depth, 1, -1)
    gla_heads = lambda a: _pad_heads(a, GLA_HEADS, GLA_DK, GLA_DKP)
    n_gv = GLA_HEADS * GLA_DV
    p = {
        "norm_mix_pre": rows(norm_mix_pre), "w_in": _bf(_arrange_in(w_in)),
        "conv": _arrange_rw(rw_conv), "a0": rows(rw_a0),
        "a2": _bf(jnp.pad(rw_a2, [(0, 0), (0, 64), (0, 0)])), "g2": _bf(rw_g2),
        "w0": jnp.concatenate([rw_w0_f, rw_w0_b], axis=-1)[:, None, :],
        "w2": _bf(jnp.concatenate([jnp.pad(rw_w2_f, [(0, 0), (0, 0), (0, RW_W)]),
                                   jnp.pad(rw_w2_b, [(0, 0), (0, 0), (RW_W, 0)])], axis=1)),
        "k_k": rows(rw_k_k), "k_a": rows(rw_k_a), "r_k": rows(rw_r_k),
        "gla_wa": _bf(jnp.stack([jnp.pad(gla_heads(gla_wa2_f), [(0, 0), (0, 128 - GLA_RANK), (0, 0)]),
                                 jnp.pad(gla_heads(gla_wa2_b), [(0, 0), (GLA_RANK, 128 - 2 * GLA_RANK), (0, 0)])],
                                axis=1)),
        "gla_ba": jnp.stack([gla_heads(gla_ba_f), gla_heads(gla_ba_b)], axis=1)[:, :, None, :],
        "ret_norm": rows(ret_norm),
        "gla_norm": rows(jnp.tile(_pad_last(gla_norm, GLA_DVP), (1, GLA_HEADS))),
        "ln_w": rows(rw_ln_w), "ln_b": rows(rw_ln_b),
        "w_out": _bf(jnp.concatenate([
            w_out[:, 0:RET_W],
            jnp.pad(w_out[:, RET_W:RET_W + n_gv].reshape(depth, GLA_HEADS, GLA_DV, d),
                    [(0, 0), (0, 0), (0, GLA_DVP - GLA_DV), (0, 0)]).reshape(depth, GLA_VWP, d),
            w_out[:, RET_W + n_gv:]], axis=1)),
        "norm_mix_post": rows(norm_mix_post), "norm_ffn_pre": rows(norm_ffn_pre),
        "w_gate": _bf(w_ffn_gate), "w_up": _bf(w_ffn_up), "w_down": _bf(w_ffn_down),
        "norm_ffn_post": rows(norm_ffn_post),
    }

    ctx_in, xa = ctx, x
    for l in range(depth):
        u_ret, u_gla, s1, s2 = _proj_call(ctx_in, xa, modsel, p, sum_w, l)
        ra, ga, wo = _scan_call(u_ret, cos, sin, u_gla, s1, p, l)
        tile_off = CTX_LEN // ROW_TILE if l == depth - 1 else 0
        xa = _post_call(ctx_in, xa, modsel, ra, u_ret, ga, u_gla, wo, s2, p, shared, l, tile_off)
        ctx_in = None
    return xa
```
